```python
import jax, jax.numpy as jnp
from jax import lax
import numpy as np

D_MODEL = 1024
BATCH = 4
SEQ = 4096
DEPTH = 4
DEC_BATCH = 128
DEC_SEQ = 4
PAST_LEN = 8192
PAGE_SIZE = 128

WINDOW = 128
N_HEADS = 8
KV_HEADS = 2
HEAD_DIM = 64
Q_PER_KV = N_HEADS // KV_HEADS
ATT_WIDTH = N_HEADS * HEAD_DIM
HG_HEADS = 4
HG_DK = 128
HG_DV = 128
HG_KEY_WIDTH = HG_HEADS * HG_DK
HG_VAL_WIDTH = HG_HEADS * HG_DV
HG_CHUNK = 64
D_FF = 4 * D_MODEL
N_BRANCH = 2
SPLIT_SIZES = (ATT_WIDTH, KV_HEADS * HEAD_DIM, KV_HEADS * HEAD_DIM,
               HG_KEY_WIDTH, HG_KEY_WIDTH, HG_VAL_WIDTH, HG_VAL_WIDTH, N_BRANCH * D_MODEL)
IN_COLS = sum(SPLIT_SIZES)
DEEPNORM_ALPHA = (2 * DEPTH) ** 0.25
DEEPNORM_BETA = (8 * DEPTH) ** -0.25
LN_EPS = 1e-5
RMS_EPS = 1e-6
NEG_BIG = -1e30
LB_FLOOR = 1e-30

kernel_name = "hybrid_swa_sink_hgrn2_deepnorm_step"


def _layer_norm(x, g, b):
    xf = x.astype(jnp.float32)
    mu = xf.mean(-1, keepdims=True)
    var = jnp.square(xf - mu).mean(-1, keepdims=True)
    y = (xf - mu) * lax.rsqrt(var + LN_EPS) * g.astype(jnp.float32) + b.astype(jnp.float32)
    return y.astype(x.dtype)


def _rms_norm(x, g):
    xf = x.astype(jnp.float32)
    return xf * lax.rsqrt(jnp.mean(jnp.square(xf), -1, keepdims=True) + RMS_EPS) * g.astype(jnp.float32)


def _sink_attention(q, k, v, mask, sink):
    s = jnp.einsum('...qkgd,...jkd->...kgqj', q.astype(jnp.float32), k.astype(jnp.float32)) * (HEAD_DIM ** -0.5)
    s = jnp.where(mask, s, NEG_BIG)
    snk = jnp.broadcast_to(sink.astype(jnp.float32).reshape(KV_HEADS, Q_PER_KV, 1, 1), s.shape[:-1] + (1,))
    p = jax.nn.softmax(jnp.concatenate([s, snk], axis=-1), axis=-1)[..., :-1]
    return jnp.einsum('...kgqj,...jkd->...qkgd', p, v.astype(jnp.float32))


def _window_attn_prompt(q, k, v, sink):
    B, S = q.shape[:2]
    nb = S // WINDOW
    qb = q.reshape(B, nb, WINDOW, KV_HEADS, Q_PER_KV, HEAD_DIM)

    def with_prev(t):
        t = t.reshape(B, nb, WINDOW, KV_HEADS, HEAD_DIM)
        prev = jnp.pad(t, ((0, 0), (1, 0), (0, 0), (0, 0), (0, 0)))[:, :-1]
        return jnp.concatenate([prev, t], axis=2)

    blk = jnp.arange(nb)[:, None, None]
    qi = jnp.arange(WINDOW)[None, :, None]
    kj = jnp.arange(2 * WINDOW)[None, None, :]
    diff = WINDOW + qi - kj
    kpos = (blk - 1) * WINDOW + kj
    mask = (diff >= 0) & (diff <= WINDOW) & (kpos >= 0)
    o = _sink_attention(qb, with_prev(k), with_prev(v), mask[None, :, None, None], sink)
    return o.reshape(B, S, ATT_WIDTH)


def _window_attn_sample(q, k_new, v_new, k_cache, v_cache, sink):
    B, T = q.shape[:2]
    W = k_cache.shape[1]
    kk = jnp.concatenate([k_cache.astype(k_new.dtype), k_new], axis=1)
    vv = jnp.concatenate([v_cache.astype(v_new.dtype), v_new], axis=1)
    qi = jnp.arange(T)[:, None]
    kj = jnp.arange(W + T)[None, :]
    diff = W + qi - kj
    mask = (diff >= 0) & (diff <= WINDOW)
    o = _sink_attention(q, kk, vv, mask, sink)
    return o.reshape(B, T, ATT_WIDTH), kk[:, T:], vv[:, T:]


def _hgrn_chunk(s0, q, k, v, logf):
    s0 = s0.astype(jnp.float32)
    C = q.shape[2]
    b = jnp.cumsum(logf, axis=2)
    o = jnp.einsum('bhtd,bhde->bhte', q * jnp.exp(b), s0)
    causal = jnp.tril(jnp.ones((C, C), dtype=bool))[None, None, :, :, None]
    diff = b[:, :, :, None, :] - b[:, :, None, :, :]
    decay = jnp.exp(jnp.where(causal, diff, NEG_BIG))
    att = jnp.einsum('bhtd,bhtsd,bhsd->bhts', q, decay, k)
    o = o + jnp.einsum('bhts,bhse->bhte', att, v)
    b_end = b[:, :, -1:, :]
    s_new = jnp.exp(b_end[:, :, 0, :])[..., None] * s0 + jnp.einsum('bhsd,bhse->bhde', k * jnp.exp(b_end - b), v)
    return s_new, o


def _hgrn_prompt(q, k, v, logf):
    B, S = q.shape[:2]
    nc = S // HG_CHUNK

    def to_chunks(t):
        return t.reshape(B, nc, HG_CHUNK, HG_HEADS, t.shape[-1]).transpose(1, 0, 3, 2, 4)

    s0 = jnp.zeros((B, HG_HEADS, HG_DK, HG_DV), jnp.float32)
    s_fin, o = lax.scan(lambda s, xs: _hgrn_chunk(s, *xs), s0,
                        (to_chunks(q), to_chunks(k), to_chunks(v), to_chunks(logf)))
    o = o.transpose(1, 0, 3, 2, 4).reshape(B, S, HG_HEADS, HG_DV)
    return o, s_fin


def _hgrn_sample(q, k, v, logf, state):
    tr = lambda t: t.transpose(0, 2, 1, 3)
    s_new, o = _hgrn_chunk(state, tr(q), tr(k), tr(v), tr(logf))
    return tr(o), s_new


def _layer(x, cache_k, cache_v, state, lb, w_in, b_gate, attn_sink, hg_norm_w,
           w_up_attn, w_up_hgrn, w_out, ln1_g, ln1_b, w_ff1, w_ff2, ln2_g, ln2_b):
    Bn, T, _ = x.shape
    z = x @ w_in
    offsets = [int(o) for o in np.cumsum(SPLIT_SIZES)[:-1]]
    aq, ak, av, hq, hf, hi, hg, gates = jnp.split(z, offsets, axis=-1)
    aq = aq.reshape(Bn, T, KV_HEADS, Q_PER_KV, HEAD_DIM)
    ak = ak.reshape(Bn, T, KV_HEADS, HEAD_DIM)
    av = av.reshape(Bn, T, KV_HEADS, HEAD_DIM)
    hf32 = hf.astype(jnp.float32)
    logf = jnp.logaddexp(jnp.log(jnp.maximum(lb, LB_FLOOR)), jnp.log1p(-lb) + jax.nn.log_sigmoid(hf32))
    hk = (1.0 - lb) * jax.nn.sigmoid(-hf32)
    hq32 = jax.nn.silu(hq.astype(jnp.float32))
    heads = lambda t, d: t.reshape(Bn, T, HG_HEADS, d)
    hq32, hk, logf = heads(hq32, HG_DK), heads(hk, HG_DK), heads(logf, HG_DK)
    hv = heads(hi.astype(jnp.float32), HG_DV)
    if cache_k is None:
        a_out = _window_attn_prompt(aq, ak, av, attn_sink)
        W = min(WINDOW, T)
        new_k, new_v = ak[:, T - W:], av[:, T - W:]
        h_o, new_s = _hgrn_prompt(hq32, hk, hv, logf)
    else:
        a_out, new_k, new_v = _window_attn_sample(aq, ak, av, cache_k, cache_v, attn_sink)
        h_o, new_s = _hgrn_sample(hq32, hk, hv, logf, state)
    h_out = (_rms_norm(h_o, hg_norm_w).reshape(Bn, T, HG_VAL_WIDTH)
             * jax.nn.silu(hg.astype(jnp.float32)))
    g = jax.nn.sigmoid(gates.astype(jnp.float32) + b_gate.astype(jnp.float32))
    g_a, g_h = g[..., :D_MODEL], g[..., D_MODEL:]
    merged = (g_a * (a_out.astype(x.dtype) @ w_up_attn).astype(jnp.float32)
              + g_h * (h_out.astype(x.dtype) @ w_up_hgrn).astype(jnp.float32))
    m = merged.astype(x.dtype) @ w_out
    x = _layer_norm(DEEPNORM_ALPHA * x + m, ln1_g, ln1_b)
    ff = jnp.square(jax.nn.relu(x @ w_ff1)) @ w_ff2
    x = _layer_norm(DEEPNORM_ALPHA * x + ff, ln2_g, ln2_b)
    return x, new_k, new_v, new_s


def setup_inputs(seed: int = 0) -> dict:
    key = jax.random.key(seed)
    ks = jax.random.split(key, 24)
    n = lambda k, shape, s=1.0: jax.random.normal(k, shape, jnp.float32) * s
    cw = min(WINDOW, PAST_LEN)
    return {
        "x_prompt": n(ks[0], (BATCH, SEQ, D_MODEL)),
        "x_sample": n(ks[1], (DEC_BATCH, DEC_SEQ, D_MODEL)),
        "cache_k": n(ks[2], (DEPTH, DEC_BATCH, cw, KV_HEADS, HEAD_DIM)),
        "cache_v": n(ks[3], (DEPTH, DEC_BATCH, cw, KV_HEADS, HEAD_DIM)),
        "state_hgrn": n(ks[4], (DEPTH, DEC_BATCH, HG_HEADS, HG_DK, HG_DV), 0.5),
        "w_in": n(ks[5], (DEPTH, D_MODEL, IN_COLS), D_MODEL ** -0.5),
        "b_gate": n(ks[6], (DEPTH, N_BRANCH * D_MODEL), 0.1),
        "attn_sink": n(ks[7], (DEPTH, N_HEADS), 0.5),
        "hgrn_lb_logits": n(ks[8], (DEPTH, HG_KEY_WIDTH), 0.5),
        "hgrn_norm_w": 1.0 + n(ks[9], (DEPTH, HG_DV), 0.1),
        "w_up_attn": n(ks[10], (DEPTH, ATT_WIDTH, D_MODEL), ATT_WIDTH ** -0.5),
        "w_up_hgrn": n(ks[11], (DEPTH, HG_VAL_WIDTH, D_MODEL), HG_VAL_WIDTH ** -0.5),
        "w_out": n(ks[12], (DEPTH, D_MODEL, D_MODEL), D_MODEL ** -0.5 * DEEPNORM_BETA),
        "ln1_g": 1.0 + n(ks[13], (DEPTH, D_MODEL), 0.1),
        "ln1_b": n(ks[14], (DEPTH, D_MODEL), 0.1),
        "w_ff1": n(ks[15], (DEPTH, D_MODEL, D_FF), D_MODEL ** -0.5),
        "w_ff2": n(ks[16], (DEPTH, D_FF, D_MODEL), D_FF ** -0.5 * DEEPNORM_BETA),
        "ln2_g": 1.0 + n(ks[17], (DEPTH, D_MODEL), 0.1),
        "ln2_b": n(ks[18], (DEPTH, D_MODEL), 0.1),
    }


def reference(x_prompt, x_sample, cache_k, cache_v, state_hgrn, w_in, b_gate, attn_sink,
              hgrn_lb_logits, hgrn_norm_w, w_up_attn, w_up_hgrn, w_out, ln1_g, ln1_b,
              w_ff1, w_ff2, ln2_g, ln2_b):
    p = jax.nn.softmax(hgrn_lb_logits.astype(jnp.float32), axis=0)
    lb_all = jnp.cumsum(p, axis=0) - p[0]
    xp, xs = x_prompt, x_sample
    pk, pv, ps, sk, sv, ss = [], [], [], [], [], []
    for l in range(DEPTH):
        wl = (w_in[l], b_gate[l], attn_sink[l], hgrn_norm_w[l], w_up_attn[l], w_up_hgrn[l],
              w_out[l], ln1_g[l], ln1_b[l], w_ff1[l], w_ff2[l], ln2_g[l], ln2_b[l])
        xp, k1, v1, s1 = _layer(xp, None, None, None, lb_all[l], *wl)
        xs, k2, v2, s2 = _layer(xs, cache_k[l], cache_v[l], state_hgrn[l], lb_all[l], *wl)
        pk.append(k1); pv.append(v1); ps.append(s1)
        sk.append(k2); sv.append(v2); ss.append(s2)
    return (xp, xs, jnp.stack(pk), jnp.stack(pv), jnp.stack(ps), jnp.stack(sk), jnp.stack(sv), jnp.stack(ss))
```

```python
import functools

import jax
import jax.numpy as jnp
from jax import lax
from jax.experimental import pallas as pl
from jax.experimental.pallas import tpu as pltpu

F32 = jnp.float32
BF16 = jnp.bfloat16

D_MODEL = 1024
WINDOW = 128
N_HEADS = 8
KV_HEADS = 2
HEAD_DIM = 64
ATT_W = N_HEADS * HEAD_DIM
KV_W = KV_HEADS * HEAD_DIM
HG_HEADS = 4
HG_D = 128
HG_W = HG_HEADS * HG_D
GATE_W = 2 * D_MODEL
D_FF = 4 * D_MODEL
DEC_SEQ = 4
LN_EPS = 1e-5
RMS_EPS = 1e-6
NEG_BIG = -1e30
LB_FLOOR = 1e-30

G_OFF, AQ_OFF, HQ_OFF, HF_OFF, HI_OFF, HG_OFF, AK_OFF, AV_OFF = 0, 2048, 2560, 3072, 3584, 4096, 4608, 4736
IN_COLS = 4864
_SRC = dict(aq=(0, 512), ak=(512, 640), av=(640, 768), hq=(768, 1280), hf=(1280, 1792),
            hi=(1792, 2304), hg=(2304, 2816), gates=(2816, 4864))

CHUNK = 128
SEQ_PER_STEP = 4
ROWS_PER_STEP = SEQ_PER_STEP * DEC_SEQ
V7X_VMEM_LIMIT = 56 * 1024 * 1024


def _cparams(*sem):
    return pltpu.CompilerParams(dimension_semantics=sem, vmem_limit_bytes=V7X_VMEM_LIMIT)


def _resident(shape, index_map):
    return pl.BlockSpec(shape, index_map, pipeline_mode=pl.Buffered(1))


def _in_proj_kernel(x_ref, w_ref, z_ref):
    xb = x_ref[...].astype(BF16)
    for lo in range(0, IN_COLS, 512):
        hi = min(lo + 512, IN_COLS)
        z_ref[:, lo:hi] = jnp.dot(xb, w_ref[:, lo:hi], preferred_element_type=F32)


def _in_proj(x, w_all, layer):
    t = x.shape[0]
    tm = min(512, t)
    assert t % tm == 0
    return pl.pallas_call(
        _in_proj_kernel,
        grid=(t // tm,),
        in_specs=[pl.BlockSpec((tm, D_MODEL), lambda i: (i, 0)),
                  _resident((None, D_MODEL, IN_COLS), lambda i: (layer, 0, 0))],
        out_specs=pl.BlockSpec((tm, IN_COLS), lambda i: (i, 0)),
        out_shape=jax.ShapeDtypeStruct((t, IN_COLS), F32),
        compiler_params=_cparams("parallel"),
        name="in_proj",
    )(x, w_all)


def _split_kv(t):
    low = lax.broadcasted_iota(jnp.int32, t.shape, 1) < HEAD_DIM
    h0_lo = jnp.where(low, t, 0.0)
    h1_hi = jnp.where(low, 0.0, t)
    h0_hi = pltpu.roll(h0_lo, HEAD_DIM, 1)
    h1_lo = pltpu.roll(h1_hi, HEAD_DIM, 1)
    return ((h0_lo.astype(BF16), h0_hi.astype(BF16)), (h1_lo.astype(BF16), h1_hi.astype(BF16)))


def _window_attention(q, k, v, mask, sink_ref):
    r = q.shape[0]
    qb = (q * (HEAD_DIM ** -0.5)).astype(BF16)
    ks = _split_kv(k)
    vs = _split_kv(v)
    mask2 = jnp.concatenate([mask, mask], axis=0)
    top = lax.broadcasted_iota(jnp.int32, (2 * r, 1), 0) < r
    cols = [None] * 4
    for g in range(KV_HEADS):
        c0, c1 = 2 * g, 2 * g + 1
        qg = jnp.concatenate([qb[:, c0 * 128:(c0 + 1) * 128], qb[:, c1 * 128:(c1 + 1) * 128]], axis=0)
        acc = None
        for half in range(2):
            sink = jnp.where(top, sink_ref[0, 2 * c0 + half], sink_ref[0, 2 * c1 + half])
            s = lax.dot_general(qg, ks[g][half], (((1,), (1,)), ((), ())), preferred_element_type=F32)
            s = jnp.where(mask2, s, NEG_BIG)
            m = jnp.maximum(jnp.max(s, axis=-1, keepdims=True), sink)
            p = jnp.exp(s - m)
            den = jnp.sum(p, axis=-1, keepdims=True) + jnp.exp(sink - m)
            o = jnp.dot(p.astype(BF16), vs[g][half], preferred_element_type=F32) / den
            acc = o if acc is None else acc + o
        cols[c0] = acc[:r]
        cols[c1] = acc[r:]
    return jnp.concatenate(cols, axis=1)


def _band_mask(rows, qi):
    kj = lax.broadcasted_iota(jnp.int32, (rows, 2 * WINDOW), 1)
    return jnp.logical_and(kj >= qi, kj <= qi + WINDOW)


def _attn_prompt_kernel(sink_ref, q_ref, kp_ref, kc_ref, vp_ref, vc_ref, o_ref):
    i = pl.program_id(1)
    k = jnp.concatenate([kp_ref[...], kc_ref[...]], axis=0)
    v = jnp.concatenate([vp_ref[...], vc_ref[...]], axis=0)
    qi = lax.broadcasted_iota(jnp.int32, (WINDOW, 2 * WINDOW), 0)
    kj = lax.broadcasted_iota(jnp.int32, (WINDOW, 2 * WINDOW), 1)
    mask = jnp.logical_and(_band_mask(WINDOW, qi), jnp.logical_or(kj >= WINDOW, i > 0))
    o_ref[...] = _window_attention(q_ref[...], k, v, mask, sink_ref).astype(o_ref.dtype)


def _attn_prompt(z, sink, batch, seq):
    nb = seq // WINDOW
    cur = lambda off, w: (lambda b, i: (b * nb + i, off // w))
    prev = lambda off, w: (lambda b, i: (b * nb + jnp.maximum(i - 1, 0), off // w))
    return pl.pallas_call(
        _attn_prompt_kernel,
        grid=(batch, nb),
        in_specs=[pl.BlockSpec(memory_space=pltpu.SMEM),
                  pl.BlockSpec((WINDOW, ATT_W), cur(AQ_OFF, ATT_W)),
                  pl.BlockSpec((WINDOW, KV_W), prev(AK_OFF, KV_W)),
                  pl.BlockSpec((WINDOW, KV_W), cur(AK_OFF, KV_W)),
                  pl.BlockSpec((WINDOW, KV_W), prev(AV_OFF, KV_W)),
                  pl.BlockSpec((WINDOW, KV_W), cur(AV_OFF, KV_W))],
        out_specs=pl.BlockSpec((WINDOW, ATT_W), lambda b, i: (b * nb + i, 0)),
        out_shape=jax.ShapeDtypeStruct((batch * seq, ATT_W), BF16),
        compiler_params=_cparams("parallel", "parallel"),
        name="attn_prompt",
    )(sink, z, z, z, z, z)


def _rows_to_top(t, j):
    shifted = t if j == 0 else pltpu.roll(t, ROWS_PER_STEP - DEC_SEQ * j, 0)
    return shifted[:8]


def _attn_sample_kernel(sink_ref, q_ref, kn_ref, vn_ref, ck_ref, cv_ref, o_ref, nk_ref, nv_ref):
    q16, kn16, vn16 = q_ref[...], kn_ref[...], vn_ref[...]
    row8 = lax.broadcasted_iota(jnp.int32, (8, 2 * WINDOW), 0)
    mask = _band_mask(8, jnp.bitwise_and(row8, DEC_SEQ - 1))
    row16 = lax.broadcasted_iota(jnp.int32, (ROWS_PER_STEP, ATT_W), 0)
    rowc = lax.broadcasted_iota(jnp.int32, (WINDOW, KV_W), 0)
    zeros = jnp.zeros((WINDOW - 8, KV_W), F32)
    out = jnp.zeros((ROWS_PER_STEP, ATT_W), F32)
    for j in range(SEQ_PER_STEP):
        qj = _rows_to_top(q16, j)
        knj = _rows_to_top(kn16, j)
        vnj = _rows_to_top(vn16, j)
        ck = ck_ref[j]
        cv = cv_ref[j]
        k = jnp.concatenate([ck, knj, zeros], axis=0)
        v = jnp.concatenate([cv, vnj, zeros], axis=0)
        oj = _window_attention(qj, k, v, mask, sink_ref)
        oj16 = jnp.concatenate([oj, oj], axis=0)
        if j:
            oj16 = pltpu.roll(oj16, DEC_SEQ * j, 0)
        out = jnp.where(jnp.right_shift(row16, 2) == j, oj16, out)
        for new, cache, dst in ((knj, ck, nk_ref), (vnj, cv, nv_ref)):
            tail = pltpu.roll(jnp.concatenate([new] * (WINDOW // 8), axis=0), DEC_SEQ, 0)
            dst[j] = jnp.where(rowc >= WINDOW - DEC_SEQ, tail, pltpu.roll(cache, WINDOW - DEC_SEQ, 0))
    o_ref[...] = out.astype(o_ref.dtype)


def _attn_sample(z, sink, cache_k, cache_v, layer):
    db = cache_k.shape[1]
    steps = db // SEQ_PER_STEP
    cache_spec = pl.BlockSpec((None, SEQ_PER_STEP, WINDOW, KV_W), lambda s: (layer, s, 0, 0))
    win_spec = pl.BlockSpec((SEQ_PER_STEP, WINDOW, KV_W), lambda s: (s, 0, 0))
    return pl.pallas_call(
        _attn_sample_kernel,
        grid=(steps,),
        in_specs=[pl.BlockSpec(memory_space=pltpu.SMEM),
                  pl.BlockSpec((ROWS_PER_STEP, ATT_W), lambda s: (s, AQ_OFF // ATT_W)),
                  pl.BlockSpec((ROWS_PER_STEP, KV_W), lambda s: (s, AK_OFF // KV_W)),
                  pl.BlockSpec((ROWS_PER_STEP, KV_W), lambda s: (s, AV_OFF // KV_W)),
                  cache_spec, cache_spec],
        out_specs=[pl.BlockSpec((ROWS_PER_STEP, ATT_W), lambda s: (s, 0)), win_spec, win_spec],
        out_shape=[jax.ShapeDtypeStruct((db * DEC_SEQ, ATT_W), BF16),
                   jax.ShapeDtypeStruct((db, WINDOW, KV_W), F32),
                   jax.ShapeDtypeStruct((db, WINDOW, KV_W), F32)],
        compiler_params=_cparams("parallel"),
        name="attn_sample",
    )(sink, z, z, z, cache_k, cache_v)


def _lower_bound(logits_ref, layer):
    lg = logits_ref[...]
    e = jnp.exp(lg - jnp.max(lg, axis=0, keepdims=True))
    p = e / jnp.sum(e, axis=0, keepdims=True)
    lb = jnp.zeros((1, HG_W), F32)
    for i in range(1, layer + 1):
        lb = lb + p[i:i + 1]
    return lb


def _hgrn_gates(zq, zf, lb):
    e = jnp.exp(-jnp.abs(zf))
    r = 1.0 / (1.0 + e)
    pos = zf >= 0
    sig_p = jnp.where(pos, r, e * r)
    sig_n = jnp.where(pos, e * r, r)
    logf = jnp.log(jnp.maximum(lb, LB_FLOOR) + (1.0 - lb) * sig_p)
    k = (1.0 - lb) * sig_n
    q = zq * jax.nn.sigmoid(zq)
    return q, k, logf


def _hgrn_out(o, zg, normw):
    ms = jnp.mean(o * o, axis=-1, keepdims=True)
    return o * lax.rsqrt(ms + RMS_EPS) * normw * (zg * jax.nn.sigmoid(zg))


def _cumsum_rows(x, tri):
    hi = x.astype(BF16)
    r1 = x - hi.astype(F32)
    mid = r1.astype(BF16)
    lo = (r1 - mid.astype(F32)).astype(BF16)
    dot = lambda t: jnp.dot(tri, t, preferred_element_type=F32)
    return dot(hi) + dot(mid) + dot(lo)


def _level_reference(b, b_ref, head, n, row):
    if n == 2:
        return jnp.where(jnp.bitwise_and(row, 1) == 0, b, pltpu.roll(b, 1, 0))
    if n == 4:
        m4 = jnp.bitwise_and(row, 3)
        return jnp.where(m4 == 0, pltpu.roll(b, CHUNK - 1, 0),
                         jnp.where(m4 == 1, b, jnp.where(m4 == 2, pltpu.roll(b, 1, 0), pltpu.roll(b, 2, 0))))
    pieces = [jnp.broadcast_to(b_ref[head, pl.ds(m * n + n // 2 - 1, 1), :], (n, HG_D)) for m in range(CHUNK // n)]
    return pieces[0] if len(pieces) == 1 else jnp.concatenate(pieces, axis=0)


def _hgrn_prompt_kernel(layer, hq_ref, hf_ref, hi_ref, hg_ref, logits_ref, normw_ref, o_ref, st_ref, b_ref):
    @pl.when(pl.program_id(1) == 0)
    def _():
        st_ref[...] = jnp.zeros_like(st_ref)

    lb = _lower_bound(logits_ref, layer)
    normw = normw_ref[...]
    row = lax.broadcasted_iota(jnp.int32, (CHUNK, HG_D), 0)
    col = lax.broadcasted_iota(jnp.int32, (CHUNK, CHUNK), 1)
    tri = jnp.where(row >= col, 1.0, 0.0).astype(BF16)
    lvl = jnp.where(row >= col, 31 - lax.clz(jnp.bitwise_xor(row, col)), -2)

    for h in range(HG_HEADS):
        sl = slice(h * HG_D, (h + 1) * HG_D)
        q, k, logf = _hgrn_gates(hq_ref[:, sl], hf_ref[:, sl], lb[:, sl])
        v = hi_ref[:, sl]
        vb = v.astype(BF16)
        b = _cumsum_rows(logf, tri)
        b_ref[h] = b
        att = jnp.where(lvl == -1, jnp.sum(q * k, axis=-1, keepdims=True), 0.0)
        for level in range(7):
            r = _level_reference(b, b_ref, h, 2 << level, row)
            e = jnp.exp(-jnp.abs(b - r))
            p = lax.dot_general((q * e).astype(BF16), (k * e).astype(BF16), (((1,), (1,)), ((), ())),
                                preferred_element_type=F32)
            att = jnp.where(lvl == level, p, att)
        s0 = st_ref[0, h]
        o = jnp.dot((q * jnp.exp(b)).astype(BF16), s0.astype(BF16), preferred_element_type=F32)
        o = o + jnp.dot(att.astype(BF16), vb, preferred_element_type=F32)
        o_ref[:, sl] = _hgrn_out(o, hg_ref[:, sl], normw).astype(o_ref.dtype)
        b_end = b_ref[h, pl.ds(CHUNK - 1, 1), :]
        kd_t = jnp.transpose(k * jnp.exp(b_end - b))
        decay = jnp.transpose(jnp.broadcast_to(jnp.exp(b_end), (HG_D, HG_D)))
        st_ref[0, h] = decay * s0 + jnp.dot(kd_t.astype(BF16), vb, preferred_element_type=F32)


def _hgrn_prompt(z, logits, normw, layer, batch, seq):
    nc = seq // CHUNK
    zcol = lambda off: pl.BlockSpec((CHUNK, HG_W), lambda b, i: (b * nc + i, off // HG_W))
    return pl.pallas_call(
        functools.partial(_hgrn_prompt_kernel, layer),
        grid=(batch, nc),
        in_specs=[zcol(HQ_OFF), zcol(HF_OFF), zcol(HI_OFF), zcol(HG_OFF),
                  pl.BlockSpec(logits.shape, lambda b, i: (0, 0)),
                  pl.BlockSpec((1, HG_D), lambda b, i: (0, 0))],
        out_specs=[pl.BlockSpec((CHUNK, HG_W), lambda b, i: (b * nc + i, 0)),
                   pl.BlockSpec((1, HG_HEADS, HG_D, HG_D), lambda b, i: (b, 0, 0, 0))],
        out_shape=[jax.ShapeDtypeStruct((batch * seq, HG_W), BF16),
                   jax.ShapeDtypeStruct((batch, HG_HEADS, HG_D, HG_D), F32)],
        scratch_shapes=[pltpu.VMEM((HG_HEADS, CHUNK, HG_D), F32)],
        compiler_params=_cparams("parallel", "arbitrary"),
        name="hgrn_prompt",
    )(z, z, z, z, logits, normw)


def _hgrn_sample_kernel(layer, hq_ref, hf_ref, hi_ref, hg_ref, logits_ref, normw_ref, st_ref, o_ref, ns_ref):
    lb = _lower_bound(logits_ref, layer)
    q, k, logf = _hgrn_gates(hq_ref[...], hf_ref[...], lb)
    v = hi_ref[...]
    row = lax.broadcasted_iota(jnp.int32, (ROWS_PER_STEP, HG_W), 0)
    tok = jnp.bitwise_and(row, DEC_SEQ - 1)
    seq_of_row = jnp.right_shift(row, 2)
    down = lambda t, d: pltpu.roll(t, d, 0)
    up = lambda t, d: pltpu.roll(t, ROWS_PER_STEP - d, 0)
    b = logf
    for d in range(1, DEC_SEQ):
        b = b + jnp.where(tok >= d, down(logf, d), 0.0)
    b_end = jnp.where(tok == 3, b, jnp.where(tok == 2, up(b, 1), jnp.where(tok == 1, up(b, 2), up(b, 3))))
    qe = (q * jnp.exp(b)).astype(BF16)
    kd = k * jnp.exp(b_end - b)
    e_end = jnp.exp(b_end)
    intra = [jnp.zeros((ROWS_PER_STEP, HG_D), F32) for _ in range(HG_HEADS)]
    for d in range(DEC_SEQ):
        kd_, bd_, vd_ = (k, b, v) if d == 0 else (down(k, d), down(b, d), down(v, d))
        w = jnp.where(tok >= d, q * kd_ * jnp.exp(jnp.minimum(b - bd_, 0.0)), 0.0)
        for h in range(HG_HEADS):
            sl = slice(h * HG_D, (h + 1) * HG_D)
            intra[h] = intra[h] + jnp.sum(w[:, sl], axis=-1, keepdims=True) * vd_[:, sl]
    pad = jnp.zeros((HG_D - 2 * ROWS_PER_STEP, HG_D), F32)
    rowh = row[:, :HG_D]
    colh = lax.broadcasted_iota(jnp.int32, (HG_D, HG_D), 1)
    normw = normw_ref[...]
    for h in range(HG_HEADS):
        sl = slice(h * HG_D, (h + 1) * HG_D)
        vb = jnp.concatenate([v[:, sl], jnp.zeros((HG_D - ROWS_PER_STEP, HG_D), F32)], axis=0).astype(BF16)
        tt = jnp.transpose(jnp.concatenate([kd[:, sl], e_end[:, sl], pad], axis=0))
        inter = jnp.zeros((ROWS_PER_STEP, HG_D), F32)
        for j in range(SEQ_PER_STEP):
            s0 = st_ref[j, h]
            oi = jnp.dot(qe[:, sl], s0.astype(BF16), preferred_element_type=F32)
            inter = jnp.where(seq_of_row[:, :HG_D] == j, oi, inter)
            kd_t = jnp.where(jnp.right_shift(colh, 2) == j, tt, 0.0).astype(BF16)
            decay = jnp.broadcast_to(tt[:, ROWS_PER_STEP + DEC_SEQ * j:ROWS_PER_STEP + DEC_SEQ * j + 1], (HG_D, HG_D))
            ns_ref[j, h] = decay * s0 + jnp.dot(kd_t, vb, preferred_element_type=F32)
        o_ref[:, sl] = _hgrn_out(inter + intra[h], hg_ref[:, sl], normw).astype(o_ref.dtype)
    del rowh


def _hgrn_sample(z, logits, normw, state, layer):
    db = state.shape[1]
    steps = db // SEQ_PER_STEP
    zcol = lambda off: pl.BlockSpec((ROWS_PER_STEP, HG_W), lambda s: (s, off // HG_W))
    return pl.pallas_call(
        functools.partial(_hgrn_sample_kernel, layer),
        grid=(steps,),
        in_specs=[zcol(HQ_OFF), zcol(HF_OFF), zcol(HI_OFF), zcol(HG_OFF),
                  pl.BlockSpec(logits.shape, lambda s: (0, 0)),
                  pl.BlockSpec((1, HG_D), lambda s: (0, 0)),
                  pl.BlockSpec((None, SEQ_PER_STEP, HG_HEADS, HG_D, HG_D), lambda s: (layer, s, 0, 0, 0))],
        out_specs=[pl.BlockSpec((ROWS_PER_STEP, HG_W), lambda s: (s, 0)),
                   pl.BlockSpec((SEQ_PER_STEP, HG_HEADS, HG_D, HG_D), lambda s: (s, 0, 0, 0))],
        out_shape=[jax.ShapeDtypeStruct((db * DEC_SEQ, HG_W), BF16),
                   jax.ShapeDtypeStruct((db, HG_HEADS, HG_D, HG_D), F32)],
        compiler_params=_cparams("parallel"),
        name="hgrn_sample",
    )(z, z, z, z, logits, normw, state)


def _layer_norm(y, g, b):
    mu = jnp.mean(y, axis=-1, keepdims=True)
    yc = y - mu
    var = jnp.mean(yc * yc, axis=-1, keepdims=True)
    return yc * lax.rsqrt(var + LN_EPS) * g + b


def _post_kernel(alpha, a_ref, h_ref, g_ref, x_ref, bg_ref, wua_ref, wuh_ref, wo_ref, l1g_ref, l1b_ref,
                 wf1_ref, wf2_ref, l2g_ref, l2b_ref, o_ref):
    gate = jax.nn.sigmoid(g_ref[...] + bg_ref[...])
    ua = jnp.dot(a_ref[...], wua_ref[...], preferred_element_type=F32)
    uh = jnp.dot(h_ref[...], wuh_ref[...], preferred_element_type=F32)
    merged = gate[:, :D_MODEL] * ua + gate[:, D_MODEL:] * uh
    m = jnp.dot(merged.astype(BF16), wo_ref[...], preferred_element_type=F32)
    x1 = _layer_norm(alpha * x_ref[...] + m, l1g_ref[...], l1b_ref[...])
    hid = jnp.dot(x1.astype(BF16), wf1_ref[...], preferred_element_type=F32)
    hid = jnp.square(jnp.maximum(hid, 0.0))
    ff = jnp.dot(hid.astype(BF16), wf2_ref[...], preferred_element_type=F32)
    o_ref[...] = _layer_norm(alpha * x1 + ff, l2g_ref[...], l2b_ref[...])


def _post(a, h, z, x, p, layer, alpha):
    t = x.shape[0]
    tm = 256
    assert t % tm == 0
    rows = lambda w: pl.BlockSpec((tm, w), lambda i: (i, 0))
    vec = lambda w: _resident((None, 1, w), lambda i: (layer, 0, 0))
    mat = lambda r, c: _resident((None, r, c), lambda i: (layer, 0, 0))
    return pl.pallas_call(
        functools.partial(_post_kernel, alpha),
        grid=(t // tm,),
        in_specs=[rows(ATT_W), rows(HG_W), pl.BlockSpec((tm, GATE_W), lambda i: (i, G_OFF // GATE_W)), rows(D_MODEL),
                  vec(GATE_W), mat(ATT_W, D_MODEL), mat(HG_W, D_MODEL), mat(D_MODEL, D_MODEL),
                  vec(D_MODEL), vec(D_MODEL), mat(D_MODEL, D_FF), mat(D_FF, D_MODEL), vec(D_MODEL), vec(D_MODEL)],
        out_specs=rows(D_MODEL),
        out_shape=jax.ShapeDtypeStruct((t, D_MODEL), F32),
        compiler_params=_cparams("parallel"),
        name="post",
    )(a, h, z, x, p["b_gate"], p["w_up_attn"], p["w_up_hgrn"], p["w_out"], p["ln1_g"], p["ln1_b"],
      p["w_ff1"], p["w_ff2"], p["ln2_g"], p["ln2_b"])


def kernel(x_prompt, x_sample, cache_k, cache_v, state_hgrn, w_in, b_gate, attn_sink, hgrn_lb_logits, hgrn_norm_w,
           w_up_attn, w_up_hgrn, w_out, ln1_g, ln1_b, w_ff1, w_ff2, ln2_g, ln2_b):
    depth = w_in.shape[0]
    batch, seq, _ = x_prompt.shape
    db, dec_seq, _ = x_sample.shape
    assert dec_seq == DEC_SEQ and seq % CHUNK == 0 and db % SEQ_PER_STEP == 0
    assert cache_k.shape[2:] == (WINDOW, KV_HEADS, HEAD_DIM)
    alpha = (2 * depth) ** 0.25

    order = ("gates", "aq", "hq", "hf", "hi", "hg", "ak", "av")
    w_in_b = jnp.concatenate([w_in[:, :, _SRC[n][0]:_SRC[n][1]] for n in order], axis=2).astype(BF16)
    vec = lambda t: t.reshape(depth, 1, t.shape[-1])
    p = dict(b_gate=vec(b_gate), w_up_attn=w_up_attn.astype(BF16), w_up_hgrn=w_up_hgrn.astype(BF16),
             w_out=w_out.astype(BF16), ln1_g=vec(ln1_g), ln1_b=vec(ln1_b), w_ff1=w_ff1.astype(BF16),
             w_ff2=w_ff2.astype(BF16), ln2_g=vec(ln2_g), ln2_b=vec(ln2_b))
    ck = cache_k.reshape(depth, db, WINDOW, KV_W)
    cv = cache_v.reshape(depth, db, WINDOW, KV_W)

    xp = x_prompt.reshape(batch * seq, D_MODEL)
    xs = x_sample.reshape(db * DEC_SEQ, D_MODEL)
    pk, pv, ps, sk, sv, ss = [], [], [], [], [], []
    for l in range(depth):
        sink = attn_sink[l].reshape(1, N_HEADS)
        normw = hgrn_norm_w[l].reshape(1, HG_D)
        zp = _in_proj(xp, w_in_b, l)
        ap = _attn_prompt(zp, sink, batch, seq)
        hp, st_p = _hgrn_prompt(zp, hgrn_lb_logits, normw, l, batch, seq)
        xp = _post(ap, hp, zp, xp, p, l, alpha)
        z3 = zp.reshape(batch, seq, IN_COLS)
        pk.append(z3[:, seq - WINDOW:, AK_OFF:AK_OFF + KV_W].reshape(batch, WINDOW, KV_HEADS, HEAD_DIM))
        pv.append(z3[:, seq - WINDOW:, AV_OFF:AV_OFF + KV_W].reshape(batch, WINDOW, KV_HEADS, HEAD_DIM))
        ps.append(st_p)
        zs = _in_proj(xs, w_in_b, l)
        a_s, nk, nv = _attn_sample(zs, sink, ck, cv, l)
        hs, st_s = _hgrn_sample(zs, hgrn_lb_logits, normw, state_hgrn, l)
        xs = _post(a_s, hs, zs, xs, p, l, alpha)
        sk.append(nk.reshape(db, WINDOW, KV_HEADS, HEAD_DIM))
        sv.append(nv.reshape(db, WINDOW, KV_HEADS, HEAD_DIM))
        ss.append(st_s)
    return (xp.reshape(batch, seq, D_MODEL), xs.reshape(db, DEC_SEQ, D_MODEL), jnp.stack(pk), jnp.stack(pv),
            jnp.stack(ps), jnp.stack(sk), jnp.stack(sv), jnp.stack(ss))
```

```python
import functools

import jax
import jax.numpy as jnp
from jax import lax
from jax.experimental import pallas as pl
from jax.experimental.pallas import tpu as pltpu

F32 = jnp.float32
BF16 = jnp.bfloat16

D_MODEL = 1024
WINDOW = 128
N_HEADS = 8
KV_HEADS = 2
HEAD_DIM = 64
ATT_W = N_HEADS * HEAD_DIM
KV_W = KV_HEADS * HEAD_DIM
HG_HEADS = 4
HG_D = 128
HG_W = HG_HEADS * HG_D
GATE_W = 2 * D_MODEL
D_FF = 4 * D_MODEL
DEC_SEQ = 4
LN_EPS = 1e-5
RMS_EPS = 1e-6
NEG_BIG = -1e30
LB_FLOOR = 1e-30

AQ0, AK0, AV0, HQ0, HF0, HI0, HG0, GT0, IN_COLS = 0, 512, 640, 768, 1280, 1792, 2304, 2816, 4864

CHUNK = 128
SEQ_PER_STEP = 4
ROWS_PER_STEP = SEQ_PER_STEP * DEC_SEQ
IN_PROJ_ROWS = 256
POST_ROWS = 256
V7X_VMEM_LIMIT = 56 * 1024 * 1024


def _cparams(*sem):
    return pltpu.CompilerParams(dimension_semantics=sem, vmem_limit_bytes=V7X_VMEM_LIMIT)


def _resident(shape, index_map):
    return pl.BlockSpec(shape, index_map, pipeline_mode=pl.Buffered(1))


def _stacked_out(call, layer, stacked, n_in):
    extra, aliases = [], {}
    if layer > 0:
        for out_idx, prev in stacked.items():
            aliases[n_in + len(extra)] = out_idx
            extra.append(prev)
    return call(extra, aliases)


def _lower_bound(logits_ref, layer):
    lg = logits_ref[...]
    e = jnp.exp(lg - jnp.max(lg, axis=0, keepdims=True))
    p = e / jnp.sum(e, axis=0, keepdims=True)
    lb = jnp.zeros((1, HG_W), F32)
    for i in range(1, layer + 1):
        lb = lb + p[i:i + 1]
    return lb


def _swish(z):
    return z * jax.nn.sigmoid(z)


def _forget_and_key(zf, lb):
    e = jnp.exp(-jnp.abs(zf))
    r = 1.0 / (1.0 + e)
    pos = zf >= 0
    sig_p = jnp.where(pos, r, e * r)
    sig_n = jnp.where(pos, e * r, r)
    logf = jnp.log(jnp.maximum(lb, LB_FLOOR) + (1.0 - lb) * sig_p)
    return logf, (1.0 - lb) * sig_n


def _in_proj_kernel(layer, x_ref, w_ref, bg_ref, logits_ref,
                    q_ref, kv_ref, hq_ref, hk_ref, lf_ref, hv_ref, hg_ref, g_ref, wb_ref):
    @pl.when(pl.program_id(0) == 0)
    def _():
        wb_ref[...] = w_ref[...].astype(BF16)

    xb = x_ref[...].astype(BF16)
    seg = lambda lo, hi: jnp.dot(xb, wb_ref[:, lo:hi], preferred_element_type=F32)
    q_ref[...] = (seg(AQ0, AK0) * (HEAD_DIM ** -0.5)).astype(BF16)
    kv_ref[...] = seg(AK0, HQ0)
    hq_ref[...] = _swish(seg(HQ0, HF0)).astype(BF16)
    logf, key = _forget_and_key(seg(HF0, HI0), _lower_bound(logits_ref, layer))
    lf_ref[...] = logf
    hk_ref[...] = key.astype(BF16)
    hv_ref[...] = seg(HI0, HG0).astype(BF16)
    hg_ref[...] = _swish(seg(HG0, GT0))
    for lo in range(0, GATE_W, 512):
        g_ref[:, lo:lo + 512] = jax.nn.sigmoid(seg(GT0 + lo, GT0 + lo + 512) + bg_ref[:, lo:lo + 512])


def _in_proj(x, w_in, b_gate, logits, layer):
    t = x.shape[0]
    tm = min(IN_PROJ_ROWS, t)
    assert t % tm == 0
    rows = lambda w: pl.BlockSpec((tm, w), lambda i: (i, 0))
    out = lambda w, dt: jax.ShapeDtypeStruct((t, w), dt)
    return pl.pallas_call(
        functools.partial(_in_proj_kernel, layer),
        grid=(t // tm,),
        in_specs=[rows(D_MODEL),
                  _resident((None, D_MODEL, IN_COLS), lambda i: (layer, 0, 0)),
                  _resident((None, 1, GATE_W), lambda i: (layer, 0, 0)),
                  _resident(logits.shape, lambda i: (0, 0))],
        out_specs=[rows(ATT_W), rows(2 * KV_W), rows(HG_W), rows(HG_W), rows(HG_W), rows(HG_W), rows(HG_W),
                   rows(GATE_W)],
        out_shape=[out(ATT_W, BF16), out(2 * KV_W, F32), out(HG_W, BF16), out(HG_W, BF16), out(HG_W, F32),
                   out(HG_W, BF16), out(HG_W, F32), out(GATE_W, F32)],
        scratch_shapes=[pltpu.VMEM((D_MODEL, IN_COLS), BF16)],
        compiler_params=_cparams("arbitrary"),
        name="in_proj",
    )(x, w_in, b_gate, logits)


def _split_kv(t):
    low = lax.broadcasted_iota(jnp.int32, t.shape, 1) < HEAD_DIM
    h0_lo = jnp.where(low, t, 0.0)
    h1_hi = jnp.where(low, 0.0, t)
    h0_hi = pltpu.roll(h0_lo, HEAD_DIM, 1)
    h1_lo = pltpu.roll(h1_hi, HEAD_DIM, 1)
    return ((h0_lo.astype(BF16), h0_hi.astype(BF16)), (h1_lo.astype(BF16), h1_hi.astype(BF16)))


def _window_attention(qb, k, v, mask, sink_ref):
    r = qb.shape[0]
    ks = _split_kv(k)
    vs = _split_kv(v)
    mask2 = jnp.concatenate([mask, mask], axis=0)
    top = lax.broadcasted_iota(jnp.int32, (2 * r, 1), 0) < r
    cols = [None] * 4
    for g in range(KV_HEADS):
        c0, c1 = 2 * g, 2 * g + 1
        qg = jnp.concatenate([qb[:, c0 * 128:(c0 + 1) * 128], qb[:, c1 * 128:(c1 + 1) * 128]], axis=0)
        acc = None
        for half in range(2):
            sink = jnp.where(top, sink_ref[0, 2 * c0 + half], sink_ref[0, 2 * c1 + half])
            s = lax.dot_general(qg, ks[g][half], (((1,), (1,)), ((), ())), preferred_element_type=F32)
            s = jnp.where(mask2, s, NEG_BIG)
            m = jnp.maximum(jnp.max(s, axis=-1, keepdims=True), sink)
            p = jnp.exp(s - m)
            den = jnp.sum(p, axis=-1, keepdims=True) + jnp.exp(sink - m)
            o = jnp.dot(p.astype(BF16), vs[g][half], preferred_element_type=F32) / den
            acc = o if acc is None else acc + o
        cols[c0] = acc[:r]
        cols[c1] = acc[r:]
    return jnp.concatenate(cols, axis=1)


def _attn_prompt_kernel(sink_ref, q_ref, kp_ref, kc_ref, vp_ref, vc_ref, o_ref):
    i = pl.program_id(1)
    k = jnp.concatenate([kp_ref[...], kc_ref[...]], axis=0)
    v = jnp.concatenate([vp_ref[...], vc_ref[...]], axis=0)
    qi = lax.broadcasted_iota(jnp.int32, (WINDOW, 2 * WINDOW), 0)
    kj = lax.broadcasted_iota(jnp.int32, (WINDOW, 2 * WINDOW), 1)
    mask = jnp.logical_and(jnp.logical_and(kj >= qi, kj <= qi + WINDOW), jnp.logical_or(kj >= WINDOW, i > 0))
    o_ref[...] = _window_attention(q_ref[...], k, v, mask, sink_ref).astype(o_ref.dtype)


def _attn_prompt(q, kv, sink, batch, seq):
    nb = seq // WINDOW
    cur = lambda col: (lambda b, i: (b * nb + i, col))
    prev = lambda col: (lambda b, i: (b * nb + jnp.maximum(i - 1, 0), col))
    return pl.pallas_call(
        _attn_prompt_kernel,
        grid=(batch, nb),
        in_specs=[pl.BlockSpec(memory_space=pltpu.SMEM),
                  pl.BlockSpec((WINDOW, ATT_W), cur(0)),
                  pl.BlockSpec((WINDOW, KV_W), prev(0)), pl.BlockSpec((WINDOW, KV_W), cur(0)),
                  pl.BlockSpec((WINDOW, KV_W), prev(1)), pl.BlockSpec((WINDOW, KV_W), cur(1))],
        out_specs=pl.BlockSpec((WINDOW, ATT_W), cur(0)),
        out_shape=jax.ShapeDtypeStruct((batch * seq, ATT_W), BF16),
        compiler_params=_cparams("parallel", "parallel"),
        name="attn_prompt",
    )(sink, q, kv, kv, kv, kv)


def _attn_sample_kernel(sink_ref, q_ref, kvn_ref, ck_ref, cv_ref, *rest):
    o_ref, nk_ref, nv_ref = rest[-3:]
    lane = lax.broadcasted_iota(jnp.int32, (ROWS_PER_STEP, KV_W), 1)
    q16 = q_ref[...].astype(F32)
    blocks = []
    for h in range(N_HEADS):
        c, half, g = h // 2, h % 2, h // (N_HEADS // KV_HEADS)
        own = lane >= HEAD_DIM if half else lane < HEAD_DIM
        qh = jnp.where(own, q16[:, c * 128:(c + 1) * 128], 0.0)
        blocks.append(qh if half == g else pltpu.roll(qh, HEAD_DIM, 1))
    wt = jnp.concatenate(blocks, axis=0).astype(BF16)

    row = lax.broadcasted_iota(jnp.int32, (WINDOW, WINDOW), 0)
    col = lax.broadcasted_iota(jnp.int32, (WINDOW, WINDOW), 1)
    tok = jnp.bitwise_and(row, DEC_SEQ - 1)
    row_seq = jnp.bitwise_and(jnp.right_shift(row, 2), SEQ_PER_STEP - 1)
    pad = jnp.zeros((WINDOW - ROWS_PER_STEP, KV_W), F32)
    kn = jnp.concatenate([kvn_ref[:, :KV_W], pad], axis=0)
    vn = jnp.concatenate([kvn_ref[:, KV_W:], pad], axis=0)
    kn_t = jnp.transpose(kn)
    vn_t = jnp.transpose(vn)

    s_c = jnp.zeros((WINDOW, WINDOW), F32)
    for j in range(SEQ_PER_STEP):
        sj = jnp.dot(wt, ck_ref[j].astype(BF16), preferred_element_type=F32)
        s_c = jnp.where(row_seq == j, sj, s_c)
    s_n = jnp.dot(wt, kn_t.astype(BF16), preferred_element_type=F32)
    s_c = jnp.where(col >= tok, s_c, NEG_BIG)
    new_ok = jnp.logical_and(jnp.right_shift(col, 2) == row_seq, jnp.bitwise_and(col, DEC_SEQ - 1) <= tok)
    s_n = jnp.where(new_ok, s_n, NEG_BIG)
    head = jnp.right_shift(lax.broadcasted_iota(jnp.int32, (WINDOW, 1), 0), 4)
    sink = jnp.zeros((WINDOW, 1), F32)
    for h in range(N_HEADS):
        sink = jnp.where(head == h, sink_ref[0, h], sink)
    m = jnp.maximum(jnp.maximum(jnp.max(s_c, axis=-1, keepdims=True), jnp.max(s_n, axis=-1, keepdims=True)), sink)
    p_c = jnp.exp(s_c - m)
    p_n = jnp.exp(s_n - m)
    den = jnp.sum(p_c, axis=-1, keepdims=True) + jnp.sum(p_n, axis=-1, keepdims=True) + jnp.exp(sink - m)
    p_cb = p_c.astype(BF16)
    o = jnp.dot(p_n.astype(BF16), vn.astype(BF16), preferred_element_type=F32)
    for j in range(SEQ_PER_STEP):
        oj = lax.dot_general(p_cb, cv_ref[j].astype(BF16), (((1,), (1,)), ((), ())), preferred_element_type=F32)
        o = o + jnp.where(row_seq == j, oj, 0.0)
    o = o / den
    outs = []
    for c in range(4):
        lo, hi = o[32 * c:32 * c + 16], o[32 * c + 16:32 * c + 32]
        if c < 2:
            outs.append(jnp.where(lane < HEAD_DIM, lo, pltpu.roll(hi, HEAD_DIM, 1)))
        else:
            outs.append(jnp.where(lane < HEAD_DIM, pltpu.roll(lo, HEAD_DIM, 1), hi))
    o_ref[...] = jnp.concatenate(outs, axis=1).astype(o_ref.dtype)
    for j in range(SEQ_PER_STEP):
        shift = (WINDOW - DEC_SEQ - DEC_SEQ * j) % WINDOW
        for new_t, cache_ref, dst in ((kn_t, ck_ref, nk_ref), (vn_t, cv_ref, nv_ref)):
            dst[j] = jnp.where(col >= WINDOW - DEC_SEQ, pltpu.roll(new_t, shift, 1),
                               pltpu.roll(cache_ref[j], WINDOW - DEC_SEQ, 1))


def _attn_sample(q, kvn, sink, cache_kt, cache_vt, layer, prev_k, prev_v):
    depth, db = cache_kt.shape[:2]
    cache_spec = pl.BlockSpec((None, SEQ_PER_STEP, KV_W, WINDOW), lambda s: (layer, s, 0, 0))
    rows = lambda w: pl.BlockSpec((ROWS_PER_STEP, w), lambda s: (s, 0))
    n_in = 5

    def call(extra, aliases):
        return pl.pallas_call(
            _attn_sample_kernel,
            grid=(db // SEQ_PER_STEP,),
            in_specs=[pl.BlockSpec(memory_space=pltpu.SMEM), rows(ATT_W), rows(2 * KV_W), cache_spec, cache_spec]
            + [pl.BlockSpec(memory_space=pl.ANY)] * len(extra),
            out_specs=[rows(ATT_W), cache_spec, cache_spec],
            out_shape=[jax.ShapeDtypeStruct((db * DEC_SEQ, ATT_W), BF16),
                       jax.ShapeDtypeStruct((depth, db, KV_W, WINDOW), F32),
                       jax.ShapeDtypeStruct((depth, db, KV_W, WINDOW), F32)],
            input_output_aliases=aliases,
            compiler_params=_cparams("parallel"),
            name="attn_sample",
        )(sink, q, kvn, cache_kt, cache_vt, *extra)

    return _stacked_out(call, layer, {1: prev_k, 2: prev_v}, n_in)


def _hgrn_out(o, g_act, normw):
    ms = jnp.mean(o * o, axis=-1, keepdims=True)
    return o * lax.rsqrt(ms + RMS_EPS) * normw * g_act


def _cumsum_rows(x, tri):
    hi = x.astype(BF16)
    r1 = x - hi.astype(F32)
    mid = r1.astype(BF16)
    lo = (r1 - mid.astype(F32)).astype(BF16)
    dot = lambda t: jnp.dot(tri, t, preferred_element_type=F32)
    return dot(hi) + dot(mid) + dot(lo)


def _level_reference(b, b_ref, head, n, row):
    if n == 2:
        return jnp.where(jnp.bitwise_and(row, 1) == 0, b, pltpu.roll(b, 1, 0))
    if n == 4:
        m4 = jnp.bitwise_and(row, 3)
        return jnp.where(m4 == 0, pltpu.roll(b, CHUNK - 1, 0),
                         jnp.where(m4 == 1, b, jnp.where(m4 == 2, pltpu.roll(b, 1, 0), pltpu.roll(b, 2, 0))))
    pieces = [jnp.broadcast_to(b_ref[head, pl.ds(m * n + n // 2 - 1, 1), :], (n, HG_D)) for m in range(CHUNK // n)]
    return pieces[0] if len(pieces) == 1 else jnp.concatenate(pieces, axis=0)


def _hgrn_prompt_kernel(hq_ref, hk_ref, lf_ref, hv_ref, hg_ref, normw_ref, o_ref, st_ref, b_ref):
    @pl.when(pl.program_id(1) == 0)
    def _():
        st_ref[...] = jnp.zeros_like(st_ref)

    normw = normw_ref[...]
    row = lax.broadcasted_iota(jnp.int32, (CHUNK, HG_D), 0)
    col = lax.broadcasted_iota(jnp.int32, (CHUNK, CHUNK), 1)
    tri = jnp.where(row >= col, 1.0, 0.0).astype(BF16)
    lvl = jnp.where(row >= col, 31 - lax.clz(jnp.bitwise_xor(row, col)), -2)

    for h in range(HG_HEADS):
        sl = slice(h * HG_D, (h + 1) * HG_D)
        q, k, v = hq_ref[:, sl], hk_ref[:, sl], hv_ref[:, sl]
        b = _cumsum_rows(lf_ref[:, sl], tri)
        b_ref[h] = b
        att = jnp.where(lvl == -1, jnp.sum((q * k).astype(F32), axis=-1, keepdims=True), 0.0)
        for level in range(7):
            r = _level_reference(b, b_ref, h, 2 << level, row)
            e = jnp.exp(-jnp.abs(b - r)).astype(BF16)
            p = lax.dot_general(q * e, k * e, (((1,), (1,)), ((), ())), preferred_element_type=F32)
            att = jnp.where(lvl == level, p, att)
        s0 = st_ref[0, h]
        o = jnp.dot(q * jnp.exp(b).astype(BF16), s0.astype(BF16), preferred_element_type=F32)
        o = o + jnp.dot(att.astype(BF16), v, preferred_element_type=F32)
        o_ref[:, sl] = _hgrn_out(o, hg_ref[:, sl], normw).astype(o_ref.dtype)
        b_end = b_ref[h, pl.ds(CHUNK - 1, 1), :]
        kd_t = jnp.transpose(k.astype(F32) * jnp.exp(b_end - b))
        decay = jnp.transpose(jnp.broadcast_to(jnp.exp(b_end), (HG_D, HG_D)))
        st_ref[0, h] = decay * s0 + jnp.dot(kd_t.astype(BF16), v, preferred_element_type=F32)


def _hgrn_prompt(hq, hk, lf, hv, hg, normw, batch, seq):
    nc = seq // CHUNK
    rows = pl.BlockSpec((CHUNK, HG_W), lambda b, i: (b * nc + i, 0))
    return pl.pallas_call(
        _hgrn_prompt_kernel,
        grid=(batch, nc),
        in_specs=[rows, rows, rows, rows, rows, pl.BlockSpec((1, HG_D), lambda b, i: (0, 0))],
        out_specs=[rows, pl.BlockSpec((1, HG_HEADS, HG_D, HG_D), lambda b, i: (b, 0, 0, 0))],
        out_shape=[jax.ShapeDtypeStruct((batch * seq, HG_W), BF16),
                   jax.ShapeDtypeStruct((batch, HG_HEADS, HG_D, HG_D), F32)],
        scratch_shapes=[pltpu.VMEM((HG_HEADS, CHUNK, HG_D), F32)],
        compiler_params=_cparams("parallel", "arbitrary"),
        name="hgrn_prompt",
    )(hq, hk, lf, hv, hg, normw)


def _hgrn_sample_kernel(hq_ref, hk_ref, lf_ref, hv_ref, hg_ref, normw_ref, st_ref, *rest):
    o_ref, ns_ref = rest[-2:]
    q, k, v, logf = hq_ref[...].astype(F32), hk_ref[...].astype(F32), hv_ref[...].astype(F32), lf_ref[...]
    row = lax.broadcasted_iota(jnp.int32, (ROWS_PER_STEP, HG_W), 0)
    tok = jnp.bitwise_and(row, DEC_SEQ - 1)
    seq_of_row = jnp.right_shift(row[:, :HG_D], 2)
    down = lambda t, d: pltpu.roll(t, d, 0)
    up = lambda t, d: pltpu.roll(t, ROWS_PER_STEP - d, 0)
    b = logf
    for d in range(1, DEC_SEQ):
        b = b + jnp.where(tok >= d, down(logf, d), 0.0)
    b_end = jnp.where(tok == 3, b, jnp.where(tok == 2, up(b, 1), jnp.where(tok == 1, up(b, 2), up(b, 3))))
    qe = (q * jnp.exp(b)).astype(BF16)
    kd = k * jnp.exp(b_end - b)
    e_end = jnp.exp(b_end)
    intra = [jnp.zeros((ROWS_PER_STEP, HG_D), F32) for _ in range(HG_HEADS)]
    for d in range(DEC_SEQ):
        kd_, bd_, vd_ = (k, b, v) if d == 0 else (down(k, d), down(b, d), down(v, d))
        w = jnp.where(tok >= d, q * kd_ * jnp.exp(jnp.minimum(b - bd_, 0.0)), 0.0)
        for h in range(HG_HEADS):
            sl = slice(h * HG_D, (h + 1) * HG_D)
            intra[h] = intra[h] + jnp.sum(w[:, sl], axis=-1, keepdims=True) * vd_[:, sl]
    pad = jnp.zeros((HG_D - 2 * ROWS_PER_STEP, HG_D), F32)
    colh = lax.broadcasted_iota(jnp.int32, (HG_D, HG_D), 1)
    normw = normw_ref[...]
    for h in range(HG_HEADS):
        sl = slice(h * HG_D, (h + 1) * HG_D)
        vb = jnp.concatenate([v[:, sl], jnp.zeros((HG_D - ROWS_PER_STEP, HG_D), F32)], axis=0).astype(BF16)
        tt = jnp.transpose(jnp.concatenate([kd[:, sl], e_end[:, sl], pad], axis=0))
        inter = jnp.zeros((ROWS_PER_STEP, HG_D), F32)
        for j in range(SEQ_PER_STEP):
            s0 = st_ref[j, h]
            oi = jnp.dot(qe[:, sl], s0.astype(BF16), preferred_element_type=F32)
            inter = jnp.where(seq_of_row == j, oi, inter)
            kd_t = jnp.where(jnp.right_shift(colh, 2) == j, tt, 0.0).astype(BF16)
            e_col = ROWS_PER_STEP + DEC_SEQ * j
            decay = jnp.broadcast_to(tt[:, e_col:e_col + 1], (HG_D, HG_D))
            ns_ref[j, h] = decay * s0 + jnp.dot(kd_t, vb, preferred_element_type=F32)
        o_ref[:, sl] = _hgrn_out(inter + intra[h], hg_ref[:, sl], normw).astype(o_ref.dtype)


def _hgrn_sample(hq, hk, lf, hv, hg, normw, state, layer, prev_state):
    depth, db = state.shape[:2]
    rows = pl.BlockSpec((ROWS_PER_STEP, HG_W), lambda s: (s, 0))
    st_spec = pl.BlockSpec((None, SEQ_PER_STEP, HG_HEADS, HG_D, HG_D), lambda s: (layer, s, 0, 0, 0))
    n_in = 7

    def call(extra, aliases):
        return pl.pallas_call(
            _hgrn_sample_kernel,
            grid=(db // SEQ_PER_STEP,),
            in_specs=[rows, rows, rows, rows, rows, pl.BlockSpec((1, HG_D), lambda s: (0, 0)), st_spec]
            + [pl.BlockSpec(memory_space=pl.ANY)] * len(extra),
            out_specs=[rows, st_spec],
            out_shape=[jax.ShapeDtypeStruct((db * DEC_SEQ, HG_W), BF16), jax.ShapeDtypeStruct(state.shape, F32)],
            input_output_aliases=aliases,
            compiler_params=_cparams("parallel"),
            name="hgrn_sample",
        )(hq, hk, lf, hv, hg, normw, state, *extra)

    return _stacked_out(call, layer, {1: prev_state}, n_in)


def _layer_norm(y, g, b):
    mu = jnp.mean(y, axis=-1, keepdims=True)
    yc = y - mu
    var = jnp.mean(yc * yc, axis=-1, keepdims=True)
    return yc * lax.rsqrt(var + LN_EPS) * g + b


def _post_kernel(alpha, a_ref, h_ref, g_ref, x_ref, wua_ref, wuh_ref, wo_ref, l1g_ref, l1b_ref,
                 wf1_ref, wf2_ref, l2g_ref, l2b_ref, o_ref):
    ua = jnp.dot(a_ref[...], wua_ref[...], preferred_element_type=F32)
    uh = jnp.dot(h_ref[...], wuh_ref[...], preferred_element_type=F32)
    merged = g_ref[:, :D_MODEL] * ua + g_ref[:, D_MODEL:] * uh
    m = jnp.dot(merged.astype(BF16), wo_ref[...], preferred_element_type=F32)
    x1 = _layer_norm(alpha * x_ref[...] + m, l1g_ref[...], l1b_ref[...])
    hid = jnp.dot(x1.astype(BF16), wf1_ref[...], preferred_element_type=F32)
    hid = jnp.square(jnp.maximum(hid, 0.0))
    ff = jnp.dot(hid.astype(BF16), wf2_ref[...], preferred_element_type=F32)
    o_ref[...] = _layer_norm(alpha * x1 + ff, l2g_ref[...], l2b_ref[...])


def _post(a, h, g, x, p, layer, alpha):
    t = x.shape[0]
    tm = min(POST_ROWS, t)
    assert t % tm == 0
    rows = lambda w: pl.BlockSpec((tm, w), lambda i: (i, 0))
    vec = lambda w: _resident((None, 1, w), lambda i: (layer, 0, 0))
    mat = lambda r, c: _resident((None, r, c), lambda i: (layer, 0, 0))
    return pl.pallas_call(
        functools.partial(_post_kernel, alpha),
        grid=(t // tm,),
        in_specs=[rows(ATT_W), rows(HG_W), rows(GATE_W), rows(D_MODEL),
                  mat(ATT_W, D_MODEL), mat(HG_W, D_MODEL), mat(D_MODEL, D_MODEL),
                  vec(D_MODEL), vec(D_MODEL), mat(D_MODEL, D_FF), mat(D_FF, D_MODEL), vec(D_MODEL), vec(D_MODEL)],
        out_specs=rows(D_MODEL),
        out_shape=jax.ShapeDtypeStruct((t, D_MODEL), F32),
        compiler_params=_cparams("parallel"),
        name="post",
    )(a, h, g, x, p["w_up_attn"], p["w_up_hgrn"], p["w_out"], p["ln1_g"], p["ln1_b"],
      p["w_ff1"], p["w_ff2"], p["ln2_g"], p["ln2_b"])


def kernel(x_prompt, x_sample, cache_k, cache_v, state_hgrn, w_in, b_gate, attn_sink, hgrn_lb_logits, hgrn_norm_w,
           w_up_attn, w_up_hgrn, w_out, ln1_g, ln1_b, w_ff1, w_ff2, ln2_g, ln2_b):
    depth = w_in.shape[0]
    batch, seq, _ = x_prompt.shape
    db, dec_seq, _ = x_sample.shape
    assert dec_seq == DEC_SEQ and seq % CHUNK == 0 and db % SEQ_PER_STEP == 0
    assert cache_k.shape[2:] == (WINDOW, KV_HEADS, HEAD_DIM)
    alpha = (2 * depth) ** 0.25

    vec = lambda t: t.reshape(depth, 1, t.shape[-1])
    b_gate3 = vec(b_gate)
    p = dict(w_up_attn=w_up_attn.astype(BF16), w_up_hgrn=w_up_hgrn.astype(BF16), w_out=w_out.astype(BF16),
             ln1_g=vec(ln1_g), ln1_b=vec(ln1_b), w_ff1=w_ff1.astype(BF16), w_ff2=w_ff2.astype(BF16),
             ln2_g=vec(ln2_g), ln2_b=vec(ln2_b))
    to_t = lambda c: jnp.transpose(c, (0, 1, 3, 4, 2)).reshape(depth, db, KV_W, WINDOW)
    from_t = lambda c: jnp.transpose(c.reshape(depth, db, KV_HEADS, HEAD_DIM, WINDOW), (0, 1, 4, 2, 3))
    ck_t, cv_t = to_t(cache_k), to_t(cache_v)

    xp = x_prompt.reshape(batch * seq, D_MODEL)
    xs = x_sample.reshape(db * DEC_SEQ, D_MODEL)
    pk, pv, ps = [], [], []
    sk = sv = ss = None
    for l in range(depth):
        sink = attn_sink[l].reshape(1, N_HEADS)
        normw = hgrn_norm_w[l].reshape(1, HG_D)
        q, kv, hq, hk, lf, hv, hg, g = _in_proj(xp, w_in, b_gate3, hgrn_lb_logits, l)
        a = _attn_prompt(q, kv, sink, batch, seq)
        h, st = _hgrn_prompt(hq, hk, lf, hv, hg, normw, batch, seq)
        xp = _post(a, h, g, xp, p, l, alpha)
        win = kv.reshape(batch, seq, 2, KV_HEADS, HEAD_DIM)[:, seq - WINDOW:]
        pk.append(win[:, :, 0])
        pv.append(win[:, :, 1])
        ps.append(st)
        q, kvn, hq, hk, lf, hv, hg, g = _in_proj(xs, w_in, b_gate3, hgrn_lb_logits, l)
        a, sk, sv = _attn_sample(q, kvn, sink, ck_t, cv_t, l, sk, sv)
        h, ss = _hgrn_sample(hq, hk, lf, hv, hg, normw, state_hgrn, l, ss)
        xs = _post(a, h, g, xs, p, l, alpha)
    return (xp.reshape(batch, seq, D_MODEL), xs.reshape(db, DEC_SEQ, D_MODEL), jnp.stack(pk), jnp.stack(pv),
            jnp.stack(ps), from_t(sk), from_t(sv), ss)
```

```python
import functools

import jax
import jax.numpy as jnp
from jax import lax
from jax.experimental import pallas as pl
from jax.experimental.pallas import tpu as pltpu

F32 = jnp.float32
BF16 = jnp.bfloat16

D_MODEL = 1024
WINDOW = 128
N_HEADS = 8
KV_HEADS = 2
HEAD_DIM = 64
ATT_W = N_HEADS * HEAD_DIM
KV_W = KV_HEADS * HEAD_DIM
HG_HEADS = 4
HG_D = 128
HG_W = HG_HEADS * HG_D
GATE_W = 2 * D_MODEL
D_FF = 4 * D_MODEL
DEC_SEQ = 4
LN_EPS = 1e-5
RMS_EPS = 1e-6
NEG_BIG = -1e30
LB_FLOOR = 1e-30

AQ0, AK0, AV0, HQ0, HF0, HI0, HG0, GT0, IN_COLS = 0, 512, 640, 768, 1280, 1792, 2304, 2816, 4864

CHUNK = 128
HALF = CHUNK // 2
FAST_DECAY_LIMIT = 120.0
FAST_Q_LIMIT = 1e9
SEQ_PER_STEP = 4
ROWS_PER_STEP = SEQ_PER_STEP * DEC_SEQ
IN_PROJ_ROWS = 512
POST_ROWS = 256
V7X_VMEM_LIMIT = 56 * 1024 * 1024


def _cparams(*sem):
    return pltpu.CompilerParams(dimension_semantics=sem, vmem_limit_bytes=V7X_VMEM_LIMIT)


def _resident(shape, index_map):
    return pl.BlockSpec(shape, index_map, pipeline_mode=pl.Buffered(1))


def _stacked_out(call, layer, stacked, n_in):
    extra, aliases = [], {}
    if layer > 0:
        for out_idx, prev in stacked.items():
            aliases[n_in + len(extra)] = out_idx
            extra.append(prev)
    return call(extra, aliases)


def _lower_bound(logits_ref, layer):
    lg = logits_ref[...]
    e = jnp.exp(lg - jnp.max(lg, axis=0, keepdims=True))
    p = e / jnp.sum(e, axis=0, keepdims=True)
    lb = jnp.zeros((1, HG_W), F32)
    for i in range(1, layer + 1):
        lb = lb + p[i:i + 1]
    return lb


def _swish(z):
    return z * jax.nn.sigmoid(z)


def _forget_and_key(zf, lb):
    e = jnp.exp(-jnp.abs(zf))
    r = 1.0 / (1.0 + e)
    pos = zf >= 0
    sig_p = jnp.where(pos, r, e * r)
    sig_n = jnp.where(pos, e * r, r)
    logf = jnp.log(jnp.maximum(lb, LB_FLOOR) + (1.0 - lb) * sig_p)
    return logf, (1.0 - lb) * sig_n


def _in_proj_kernel(layer, x_ref, w_ref, bg_ref, logits_ref,
                    q_ref, kv_ref, hq_ref, hk_ref, lf_ref, hv_ref, hg_ref, g_ref, ok_ref):
    xb = x_ref[...].astype(BF16)
    seg = lambda lo, hi: jnp.dot(xb, w_ref[:, lo:hi], preferred_element_type=F32)
    q_ref[...] = (seg(AQ0, AK0) * (HEAD_DIM ** -0.5)).astype(BF16)
    kv_ref[...] = seg(AK0, HQ0)
    hq = _swish(seg(HQ0, HF0))
    hq_ref[...] = hq.astype(BF16)
    logf, key = _forget_and_key(seg(HF0, HI0), _lower_bound(logits_ref, layer))
    lf_ref[...] = logf
    hk_ref[...] = key.astype(BF16)
    hv_ref[...] = seg(HI0, HG0).astype(BF16)
    hg_ref[...] = _swish(seg(HG0, GT0))
    for lo in range(0, GATE_W, 512):
        g_ref[:, lo:lo + 512] = jax.nn.sigmoid(seg(GT0 + lo, GT0 + lo + 512) + bg_ref[:, lo:lo + 512])
    for c in range(x_ref.shape[0] // CHUNK):
        ok = jnp.max(jnp.abs(hq[c * CHUNK:(c + 1) * CHUNK]), keepdims=True) <= FAST_Q_LIMIT
        for lo in range(c * CHUNK, (c + 1) * CHUNK, HALF):
            total = jnp.sum(logf[lo:lo + HALF], axis=0, keepdims=True)
            ok = jnp.logical_and(ok, jnp.min(total, keepdims=True) >= -FAST_DECAY_LIMIT)
        ok_ref[c] = jnp.broadcast_to(jnp.where(ok, 1, 0), (1, 128)).astype(jnp.int32)


def _in_proj(x, w_in, b_gate, logits, layer):
    t = x.shape[0]
    tm = min(IN_PROJ_ROWS, t)
    assert t % tm == 0 and tm % CHUNK == 0
    rows = lambda w: pl.BlockSpec((tm, w), lambda i: (i, 0))
    out = lambda w, dt: jax.ShapeDtypeStruct((t, w), dt)
    return pl.pallas_call(
        functools.partial(_in_proj_kernel, layer),
        grid=(t // tm,),
        in_specs=[rows(D_MODEL),
                  _resident((None, D_MODEL, IN_COLS), lambda i: (layer, 0, 0)),
                  _resident((None, 1, GATE_W), lambda i: (layer, 0, 0)),
                  _resident(logits.shape, lambda i: (0, 0))],
        out_specs=[rows(ATT_W), rows(2 * KV_W), rows(HG_W), rows(HG_W), rows(HG_W), rows(HG_W), rows(HG_W),
                   rows(GATE_W), pl.BlockSpec((tm // CHUNK, 1, 128), lambda i: (i, 0, 0))],
        out_shape=[out(ATT_W, BF16), out(2 * KV_W, F32), out(HG_W, BF16), out(HG_W, BF16), out(HG_W, F32),
                   out(HG_W, BF16), out(HG_W, F32), out(GATE_W, F32),
                   jax.ShapeDtypeStruct((t // CHUNK, 1, 128), jnp.int32)],
        compiler_params=_cparams("parallel"),
        name="in_proj",
    )(x, w_in, b_gate, logits)


def _split_kv(t):
    low = lax.broadcasted_iota(jnp.int32, t.shape, 1) < HEAD_DIM
    h0_lo = jnp.where(low, t, 0.0)
    h1_hi = jnp.where(low, 0.0, t)
    h0_hi = pltpu.roll(h0_lo, HEAD_DIM, 1)
    h1_lo = pltpu.roll(h1_hi, HEAD_DIM, 1)
    return ((h0_lo.astype(BF16), h0_hi.astype(BF16)), (h1_lo.astype(BF16), h1_hi.astype(BF16)))


def _window_attention(qb, k, v, mask, sink_ref):
    r = qb.shape[0]
    ks = _split_kv(k)
    vs = _split_kv(v)
    mask2 = jnp.concatenate([mask, mask], axis=0)
    top = lax.broadcasted_iota(jnp.int32, (2 * r, 1), 0) < r
    cols = [None] * 4
    for g in range(KV_HEADS):
        c0, c1 = 2 * g, 2 * g + 1
        qg = jnp.concatenate([qb[:, c0 * 128:(c0 + 1) * 128], qb[:, c1 * 128:(c1 + 1) * 128]], axis=0)
        acc = None
        for half in range(2):
            sink = jnp.where(top, sink_ref[0, 2 * c0 + half], sink_ref[0, 2 * c1 + half])
            s = lax.dot_general(qg, ks[g][half], (((1,), (1,)), ((), ())), preferred_element_type=F32)
            s = jnp.where(mask2, s, NEG_BIG)
            m = jnp.maximum(jnp.max(s, axis=-1, keepdims=True), sink)
            p = jnp.exp(s - m)
            den = jnp.sum(p, axis=-1, keepdims=True) + jnp.exp(sink - m)
            o = jnp.dot(p.astype(BF16), vs[g][half], preferred_element_type=F32) / den
            acc = o if acc is None else acc + o
        cols[c0] = acc[:r]
        cols[c1] = acc[r:]
    return jnp.concatenate(cols, axis=1)


def _attn_prompt_kernel(sink_ref, q_ref, kp_ref, kc_ref, vp_ref, vc_ref, o_ref):
    i = pl.program_id(1)
    k = jnp.concatenate([kp_ref[...], kc_ref[...]], axis=0)
    v = jnp.concatenate([vp_ref[...], vc_ref[...]], axis=0)
    qi = lax.broadcasted_iota(jnp.int32, (WINDOW, 2 * WINDOW), 0)
    kj = lax.broadcasted_iota(jnp.int32, (WINDOW, 2 * WINDOW), 1)
    mask = jnp.logical_and(jnp.logical_and(kj >= qi, kj <= qi + WINDOW), jnp.logical_or(kj >= WINDOW, i > 0))
    o_ref[...] = _window_attention(q_ref[...], k, v, mask, sink_ref).astype(o_ref.dtype)


def _attn_prompt(q, kv, sink, batch, seq):
    nb = seq // WINDOW
    cur = lambda col: (lambda b, i: (b * nb + i, col))
    prev = lambda col: (lambda b, i: (b * nb + jnp.maximum(i - 1, 0), col))
    return pl.pallas_call(
        _attn_prompt_kernel,
        grid=(batch, nb),
        in_specs=[pl.BlockSpec(memory_space=pltpu.SMEM),
                  pl.BlockSpec((WINDOW, ATT_W), cur(0)),
                  pl.BlockSpec((WINDOW, KV_W), prev(0)), pl.BlockSpec((WINDOW, KV_W), cur(0)),
                  pl.BlockSpec((WINDOW, KV_W), prev(1)), pl.BlockSpec((WINDOW, KV_W), cur(1))],
        out_specs=pl.BlockSpec((WINDOW, ATT_W), cur(0)),
        out_shape=jax.ShapeDtypeStruct((batch * seq, ATT_W), BF16),
        compiler_params=_cparams("parallel", "parallel"),
        name="attn_prompt",
    )(sink, q, kv, kv, kv, kv)


def _attn_sample_kernel(sink_ref, q_ref, kvn_ref, ck_ref, cv_ref, *rest):
    o_ref, nk_ref, nv_ref = rest[-3:]
    lane = lax.broadcasted_iota(jnp.int32, (ROWS_PER_STEP, KV_W), 1)
    q16 = q_ref[...].astype(F32)
    blocks = []
    for h in range(N_HEADS):
        c, half, g = h // 2, h % 2, h // (N_HEADS // KV_HEADS)
        own = lane >= HEAD_DIM if half else lane < HEAD_DIM
        qh = jnp.where(own, q16[:, c * 128:(c + 1) * 128], 0.0)
        blocks.append(qh if half == g else pltpu.roll(qh, HEAD_DIM, 1))
    wt = jnp.concatenate(blocks, axis=0).astype(BF16)

    row = lax.broadcasted_iota(jnp.int32, (WINDOW, WINDOW), 0)
    col = lax.broadcasted_iota(jnp.int32, (WINDOW, WINDOW), 1)
    tok = jnp.bitwise_and(row, DEC_SEQ - 1)
    row_seq = jnp.bitwise_and(jnp.right_shift(row, 2), SEQ_PER_STEP - 1)
    pad = jnp.zeros((WINDOW - ROWS_PER_STEP, KV_W), F32)
    kn = jnp.concatenate([kvn_ref[:, :KV_W], pad], axis=0)
    vn = jnp.concatenate([kvn_ref[:, KV_W:], pad], axis=0)
    kn_t = jnp.transpose(kn)
    vn_t = jnp.transpose(vn)

    s_c = jnp.zeros((WINDOW, WINDOW), F32)
    for j in range(SEQ_PER_STEP):
        sj = jnp.dot(wt, ck_ref[j].astype(BF16), preferred_element_type=F32)
        s_c = jnp.where(row_seq == j, sj, s_c)
    s_n = jnp.dot(wt, kn_t.astype(BF16), preferred_element_type=F32)
    s_c = jnp.where(col >= tok, s_c, NEG_BIG)
    new_ok = jnp.logical_and(jnp.right_shift(col, 2) == row_seq, jnp.bitwise_and(col, DEC_SEQ - 1) <= tok)
    s_n = jnp.where(new_ok, s_n, NEG_BIG)
    head = jnp.right_shift(lax.broadcasted_iota(jnp.int32, (WINDOW, 1), 0), 4)
    sink = jnp.zeros((WINDOW, 1), F32)
    for h in range(N_HEADS):
        sink = jnp.where(head == h, sink_ref[0, h], sink)
    m = jnp.maximum(jnp.maximum(jnp.max(s_c, axis=-1, keepdims=True), jnp.max(s_n, axis=-1, keepdims=True)), sink)
    p_c = jnp.exp(s_c - m)
    p_n = jnp.exp(s_n - m)
    den = jnp.sum(p_c, axis=-1, keepdims=True) + jnp.sum(p_n, axis=-1, keepdims=True) + jnp.exp(sink - m)
    p_cb = p_c.astype(BF16)
    o = jnp.dot(p_n.astype(BF16), vn.astype(BF16), preferred_element_type=F32)
    for j in range(SEQ_PER_STEP):
        oj = lax.dot_general(p_cb, cv_ref[j].astype(BF16), (((1,), (1,)), ((), ())), preferred_element_type=F32)
        o = o + jnp.where(row_seq == j, oj, 0.0)
    o = o / den
    outs = []
    for c in range(4):
        lo, hi = o[32 * c:32 * c + 16], o[32 * c + 16:32 * c + 32]
        if c < 2:
            outs.append(jnp.where(lane < HEAD_DIM, lo, pltpu.roll(hi, HEAD_DIM, 1)))
        else:
            outs.append(jnp.where(lane < HEAD_DIM, pltpu.roll(lo, HEAD_DIM, 1), hi))
    o_ref[...] = jnp.concatenate(outs, axis=1).astype(o_ref.dtype)
    for j in range(SEQ_PER_STEP):
        shift = (WINDOW - DEC_SEQ - DEC_SEQ * j) % WINDOW
        for new_t, cache_ref, dst in ((kn_t, ck_ref, nk_ref), (vn_t, cv_ref, nv_ref)):
            dst[j] = jnp.where(col >= WINDOW - DEC_SEQ, pltpu.roll(new_t, shift, 1),
                               pltpu.roll(cache_ref[j], WINDOW - DEC_SEQ, 1))


def _attn_sample(q, kvn, sink, cache_kt, cache_vt, layer, prev_k, prev_v):
    depth, db = cache_kt.shape[:2]
    cache_spec = pl.BlockSpec((None, SEQ_PER_STEP, KV_W, WINDOW), lambda s: (layer, s, 0, 0))
    rows = lambda w: pl.BlockSpec((ROWS_PER_STEP, w), lambda s: (s, 0))
    n_in = 5

    def call(extra, aliases):
        return pl.pallas_call(
            _attn_sample_kernel,
            grid=(db // SEQ_PER_STEP,),
            in_specs=[pl.BlockSpec(memory_space=pltpu.SMEM), rows(ATT_W), rows(2 * KV_W), cache_spec, cache_spec]
            + [pl.BlockSpec(memory_space=pl.ANY)] * len(extra),
            out_specs=[rows(ATT_W), cache_spec, cache_spec],
            out_shape=[jax.ShapeDtypeStruct((db * DEC_SEQ, ATT_W), BF16),
                       jax.ShapeDtypeStruct((depth, db, KV_W, WINDOW), F32),
                       jax.ShapeDtypeStruct((depth, db, KV_W, WINDOW), F32)],
            input_output_aliases=aliases,
            compiler_params=_cparams("parallel"),
            name="attn_sample",
        )(sink, q, kvn, cache_kt, cache_vt, *extra)

    return _stacked_out(call, layer, {1: prev_k, 2: prev_v}, n_in)


def _hgrn_out(o, g_act, normw):
    ms = jnp.mean(o * o, axis=-1, keepdims=True)
    return o * lax.rsqrt(ms + RMS_EPS) * normw * g_act


def _cumsum_rows(x, tri):
    hi = x.astype(BF16)
    r1 = x - hi.astype(F32)
    mid = r1.astype(BF16)
    lo = (r1 - mid.astype(F32)).astype(BF16)
    dot = lambda t: jnp.dot(tri, t, preferred_element_type=F32)
    return dot(hi) + dot(mid) + dot(lo)


def _level_reference(b, b_ref, head, n, row):
    if n == 2:
        return jnp.where(jnp.bitwise_and(row, 1) == 0, b, pltpu.roll(b, 1, 0))
    if n == 4:
        m4 = jnp.bitwise_and(row, 3)
        return jnp.where(m4 == 0, pltpu.roll(b, CHUNK - 1, 0),
                         jnp.where(m4 == 1, b, jnp.where(m4 == 2, pltpu.roll(b, 1, 0), pltpu.roll(b, 2, 0))))
    pieces = [jnp.broadcast_to(b_ref[head, pl.ds(m * n + n // 2 - 1, 1), :], (n, HG_D)) for m in range(CHUNK // n)]
    return pieces[0] if len(pieces) == 1 else jnp.concatenate(pieces, axis=0)


def _scores(q, k, e_q, e_k):
    return lax.dot_general(q * e_q.astype(BF16), k * e_k.astype(BF16), (((1,), (1,)), ((), ())),
                           preferred_element_type=F32)


def _intra_chunk_scores(q, k, b, b_ref, head, row, lvl, fast):
    if fast:
        b_mid = b_ref[head, pl.ds(HALF - 1, 1), :]
        b_end = b_ref[head, pl.ds(CHUNK - 1, 1), :]
        r = jnp.concatenate([jnp.broadcast_to(0.5 * b_mid, (HALF, HG_D)),
                             jnp.broadcast_to(0.5 * (b_mid + b_end), (HALF, HG_D))], axis=0)
        e = jnp.exp(b - r)
        within = _scores(q, k, e, 1.0 / e)
        e6 = jnp.exp(-jnp.abs(b - b_mid))
        return jnp.where(lvl == 6, _scores(q, k, e6, e6), jnp.where(lvl >= -1, within, 0.0))
    att = jnp.where(lvl == -1, jnp.sum((q * k).astype(F32), axis=-1, keepdims=True), 0.0)
    for level in range(7):
        e = jnp.exp(-jnp.abs(b - _level_reference(b, b_ref, head, 2 << level, row)))
        att = jnp.where(lvl == level, _scores(q, k, e, e), att)
    return att


def _hgrn_chunk(fast, hq_ref, hk_ref, lf_ref, hv_ref, hg_ref, normw_ref, o_ref, st_ref, b_ref):
    normw = normw_ref[...]
    row = lax.broadcasted_iota(jnp.int32, (CHUNK, HG_D), 0)
    col = lax.broadcasted_iota(jnp.int32, (CHUNK, CHUNK), 1)
    tri = jnp.where(row >= col, 1.0, 0.0).astype(BF16)
    lvl = jnp.where(row >= col, 31 - lax.clz(jnp.bitwise_xor(row, col)), -2)
    for h in range(HG_HEADS):
        sl = slice(h * HG_D, (h + 1) * HG_D)
        q, k, v = hq_ref[:, sl], hk_ref[:, sl], hv_ref[:, sl]
        b = _cumsum_rows(lf_ref[:, sl], tri)
        b_ref[h] = b
        att = _intra_chunk_scores(q, k, b, b_ref, h, row, lvl, fast)
        s0 = st_ref[0, h]
        o = jnp.dot(q * jnp.exp(b).astype(BF16), s0.astype(BF16), preferred_element_type=F32)
        o = o + jnp.dot(att.astype(BF16), v, preferred_element_type=F32)
        o_ref[:, sl] = _hgrn_out(o, hg_ref[:, sl], normw).astype(o_ref.dtype)
        b_end = b_ref[h, pl.ds(CHUNK - 1, 1), :]
        kd_t = jnp.transpose(k.astype(F32) * jnp.exp(b_end - b))
        decay = jnp.transpose(jnp.broadcast_to(jnp.exp(b_end), (HG_D, HG_D)))
        st_ref[0, h] = decay * s0 + jnp.dot(kd_t.astype(BF16), v, preferred_element_type=F32)


def _hgrn_prompt_kernel(ok_ref, *refs):
    st_ref = refs[-2]

    @pl.when(pl.program_id(1) == 0)
    def _():
        st_ref[...] = jnp.zeros_like(st_ref)

    ok = ok_ref[pl.program_id(0) * pl.num_programs(1) + pl.program_id(1)] != 0

    @pl.when(ok)
    def _():
        _hgrn_chunk(True, *refs)

    @pl.when(jnp.logical_not(ok))
    def _():
        _hgrn_chunk(False, *refs)


def _hgrn_prompt(chunk_ok, hq, hk, lf, hv, hg, normw, batch, seq):
    nc = seq // CHUNK
    rows = pl.BlockSpec((CHUNK, HG_W), lambda b, i, ok: (b * nc + i, 0))
    return pl.pallas_call(
        _hgrn_prompt_kernel,
        grid_spec=pltpu.PrefetchScalarGridSpec(
            num_scalar_prefetch=1,
            grid=(batch, nc),
            in_specs=[rows, rows, rows, rows, rows, pl.BlockSpec((1, HG_D), lambda b, i, ok: (0, 0))],
            out_specs=[rows, pl.BlockSpec((1, HG_HEADS, HG_D, HG_D), lambda b, i, ok: (b, 0, 0, 0))],
            scratch_shapes=[pltpu.VMEM((HG_HEADS, CHUNK, HG_D), F32)]),
        out_shape=[jax.ShapeDtypeStruct((batch * seq, HG_W), BF16),
                   jax.ShapeDtypeStruct((batch, HG_HEADS, HG_D, HG_D), F32)],
        compiler_params=_cparams("parallel", "arbitrary"),
        name="hgrn_prompt",
    )(chunk_ok, hq, hk, lf, hv, hg, normw)


def _hgrn_sample_kernel(hq_ref, hk_ref, lf_ref, hv_ref, hg_ref, normw_ref, st_ref, *rest):
    o_ref, ns_ref = rest[-2:]
    q, k, v, logf = hq_ref[...].astype(F32), hk_ref[...].astype(F32), hv_ref[...].astype(F32), lf_ref[...]
    row = lax.broadcasted_iota(jnp.int32, (ROWS_PER_STEP, HG_W), 0)
    tok = jnp.bitwise_and(row, DEC_SEQ - 1)
    seq_of_row = jnp.right_shift(row[:, :HG_D], 2)
    down = lambda t, d: pltpu.roll(t, d, 0)
    up = lambda t, d: pltpu.roll(t, ROWS_PER_STEP - d, 0)
    b = logf
    for d in range(1, DEC_SEQ):
        b = b + jnp.where(tok >= d, down(logf, d), 0.0)
    b_end = jnp.where(tok == 3, b, jnp.where(tok == 2, up(b, 1), jnp.where(tok == 1, up(b, 2), up(b, 3))))
    qe = (q * jnp.exp(b)).astype(BF16)
    kd = k * jnp.exp(b_end - b)
    e_end = jnp.exp(b_end)
    intra = [jnp.zeros((ROWS_PER_STEP, HG_D), F32) for _ in range(HG_HEADS)]
    for d in range(DEC_SEQ):
        kd_, bd_, vd_ = (k, b, v) if d == 0 else (down(k, d), down(b, d), down(v, d))
        w = jnp.where(tok >= d, q * kd_ * jnp.exp(jnp.minimum(b - bd_, 0.0)), 0.0)
        for h in range(HG_HEADS):
            sl = slice(h * HG_D, (h + 1) * HG_D)
            intra[h] = intra[h] + jnp.sum(w[:, sl], axis=-1, keepdims=True) * vd_[:, sl]
    pad = jnp.zeros((HG_D - 2 * ROWS_PER_STEP, HG_D), F32)
    colh = lax.broadcasted_iota(jnp.int32, (HG_D, HG_D), 1)
    normw = normw_ref[...]
    for h in range(HG_HEADS):
        sl = slice(h * HG_D, (h + 1) * HG_D)
        vb = jnp.concatenate([v[:, sl], jnp.zeros((HG_D - ROWS_PER_STEP, HG_D), F32)], axis=0).astype(BF16)
        tt = jnp.transpose(jnp.concatenate([kd[:, sl], e_end[:, sl], pad], axis=0))
        inter = jnp.zeros((ROWS_PER_STEP, HG_D), F32)
        for j in range(SEQ_PER_STEP):
            s0 = st_ref[j, h]
            oi = jnp.dot(qe[:, sl], s0.astype(BF16), preferred_element_type=F32)
            inter = jnp.where(seq_of_row == j, oi, inter)
            kd_t = jnp.where(jnp.right_shift(colh, 2) == j, tt, 0.0).astype(BF16)
            e_col = ROWS_PER_STEP + DEC_SEQ * j
            decay = jnp.broadcast_to(tt[:, e_col:e_col + 1], (HG_D, HG_D))
            ns_ref[j, h] = decay * s0 + jnp.dot(kd_t, vb, preferred_element_type=F32)
        o_ref[:, sl] = _hgrn_out(inter + intra[h], hg_ref[:, sl], normw).astype(o_ref.dtype)


def _hgrn_sample(hq, hk, lf, hv, hg, normw, state, layer, prev_state):
    depth, db = state.shape[:2]
    rows = pl.BlockSpec((ROWS_PER_STEP, HG_W), lambda s: (s, 0))
    st_spec = pl.BlockSpec((None, SEQ_PER_STEP, HG_HEADS, HG_D, HG_D), lambda s: (layer, s, 0, 0, 0))
    n_in = 7

    def call(extra, aliases):
        return pl.pallas_call(
            _hgrn_sample_kernel,
            grid=(db // SEQ_PER_STEP,),
            in_specs=[rows, rows, rows, rows, rows, pl.BlockSpec((1, HG_D), lambda s: (0, 0)), st_spec]
            + [pl.BlockSpec(memory_space=pl.ANY)] * len(extra),
            out_specs=[rows, st_spec],
            out_shape=[jax.ShapeDtypeStruct((db * DEC_SEQ, HG_W), BF16), jax.ShapeDtypeStruct(state.shape, F32)],
            input_output_aliases=aliases,
            compiler_params=_cparams("parallel"),
            name="hgrn_sample",
        )(hq, hk, lf, hv, hg, normw, state, *extra)

    return _stacked_out(call, layer, {1: prev_state}, n_in)


def _layer_norm(y, g, b):
    mu = jnp.mean(y, axis=-1, keepdims=True)
    yc = y - mu
    var = jnp.mean(yc * yc, axis=-1, keepdims=True)
    return yc * lax.rsqrt(var + LN_EPS) * g + b


def _post_kernel(alpha, a_ref, h_ref, g_ref, x_ref, wua_ref, wuh_ref, wo_ref, l1g_ref, l1b_ref,
                 wf1_ref, wf2_ref, l2g_ref, l2b_ref, o_ref):
    ua = jnp.dot(a_ref[...], wua_ref[...], preferred_element_type=F32)
    uh = jnp.dot(h_ref[...], wuh_ref[...], preferred_element_type=F32)
    merged = g_ref[:, :D_MODEL] * ua + g_ref[:, D_MODEL:] * uh
    m = jnp.dot(merged.astype(BF16), wo_ref[...], preferred_element_type=F32)
    x1 = _layer_norm(alpha * x_ref[...] + m, l1g_ref[...], l1b_ref[...])
    hid = jnp.dot(x1.astype(BF16), wf1_ref[...], preferred_element_type=F32)
    hid = jnp.square(jnp.maximum(hid, 0.0))
    ff = jnp.dot(hid.astype(BF16), wf2_ref[...], preferred_element_type=F32)
    o_ref[...] = _layer_norm(alpha * x1 + ff, l2g_ref[...], l2b_ref[...])


def _post(a, h, g, x, p, layer, alpha):
    t = x.shape[0]
    tm = min(POST_ROWS, t)
    assert t % tm == 0
    rows = lambda w: pl.BlockSpec((tm, w), lambda i: (i, 0))
    vec = lambda w: _resident((None, 1, w), lambda i: (layer, 0, 0))
    mat = lambda r, c: _resident((None, r, c), lambda i: (layer, 0, 0))
    return pl.pallas_call(
        functools.partial(_post_kernel, alpha),
        grid=(t // tm,),
        in_specs=[rows(ATT_W), rows(HG_W), rows(GATE_W), rows(D_MODEL),
                  mat(ATT_W, D_MODEL), mat(HG_W, D_MODEL), mat(D_MODEL, D_MODEL),
                  vec(D_MODEL), vec(D_MODEL), mat(D_MODEL, D_FF), mat(D_FF, D_MODEL), vec(D_MODEL), vec(D_MODEL)],
        out_specs=rows(D_MODEL),
        out_shape=jax.ShapeDtypeStruct((t, D_MODEL), F32),
        compiler_params=_cparams("parallel"),
        name="post",
    )(a, h, g, x, p["w_up_attn"], p["w_up_hgrn"], p["w_out"], p["ln1_g"], p["ln1_b"],
      p["w_ff1"], p["w_ff2"], p["ln2_g"], p["ln2_b"])


def kernel(x_prompt, x_sample, cache_k, cache_v, state_hgrn, w_in, b_gate, attn_sink, hgrn_lb_logits, hgrn_norm_w,
           w_up_attn, w_up_hgrn, w_out, ln1_g, ln1_b, w_ff1, w_ff2, ln2_g, ln2_b):
    depth = w_in.shape[0]
    batch, seq, _ = x_prompt.shape
    db, dec_seq, _ = x_sample.shape
    assert dec_seq == DEC_SEQ and seq % CHUNK == 0 and db % SEQ_PER_STEP == 0
    assert cache_k.shape[2:] == (WINDOW, KV_HEADS, HEAD_DIM)
    alpha = (2 * depth) ** 0.25

    vec = lambda t: t.reshape(depth, 1, t.shape[-1])
    b_gate3 = vec(b_gate)
    w_in_b = w_in.astype(BF16)
    p = dict(w_up_attn=w_up_attn.astype(BF16), w_up_hgrn=w_up_hgrn.astype(BF16), w_out=w_out.astype(BF16),
             ln1_g=vec(ln1_g), ln1_b=vec(ln1_b), w_ff1=w_ff1.astype(BF16), w_ff2=w_ff2.astype(BF16),
             ln2_g=vec(ln2_g), ln2_b=vec(ln2_b))
    to_t = lambda c: jnp.transpose(c, (0, 1, 3, 4, 2)).reshape(depth, db, KV_W, WINDOW)
    from_t = lambda c: jnp.transpose(c.reshape(depth, db, KV_HEADS, HEAD_DIM, WINDOW), (0, 1, 4, 2, 3))
    ck_t, cv_t = to_t(cache_k), to_t(cache_v)

    xp = x_prompt.reshape(batch * seq, D_MODEL)
    xs = x_sample.reshape(db * DEC_SEQ, D_MODEL)
    pk, pv, ps = [], [], []
    sk = sv = ss = None
    for l in range(depth):
        sink = attn_sink[l].reshape(1, N_HEADS)
        normw = hgrn_norm_w[l].reshape(1, HG_D)
        q, kv, hq, hk, lf, hv, hg, g, chunk_ok = _in_proj(xp, w_in_b, b_gate3, hgrn_lb_logits, l)
        a = _attn_prompt(q, kv, sink, batch, seq)
        h, st = _hgrn_prompt(chunk_ok[:, 0, 0], hq, hk, lf, hv, hg, normw, batch, seq)
        xp = _post(a, h, g, xp, p, l, alpha)
        win = kv.reshape(batch, seq, 2 * KV_W)[:, seq - WINDOW:].reshape(batch, WINDOW, 2, KV_HEADS, HEAD_DIM)
        pk.append(win[:, :, 0])
        pv.append(win[:, :, 1])
        ps.append(st)
        q, kvn, hq, hk, lf, hv, hg, g, _ = _in_proj(xs, w_in_b, b_gate3, hgrn_lb_logits, l)
        a, sk, sv = _attn_sample(q, kvn, sink, ck_t, cv_t, l, sk, sv)
        h, ss = _hgrn_sample(hq, hk, lf, hv, hg, normw, state_hgrn, l, ss)
        xs = _post(a, h, g, xs, p, l, alpha)
    return (xp.reshape(batch, seq, D_MODEL), xs.reshape(db, DEC_SEQ, D_MODEL), jnp.stack(pk), jnp.stack(pv),
            jnp.stack(ps), from_t(sk), from_t(sv), ss)
```

```python
import functools

import jax
import jax.numpy as jnp
from jax import lax
from jax.experimental import pallas as pl
from jax.experimental.pallas import tpu as pltpu

F32 = jnp.float32
BF16 = jnp.bfloat16

D_MODEL = 1024
WINDOW = 128
N_HEADS = 8
KV_HEADS = 2
HEAD_DIM = 64
ATT_W = N_HEADS * HEAD_DIM
KV_W = KV_HEADS * HEAD_DIM
HG_HEADS = 4
HG_D = 128
HG_W = HG_HEADS * HG_D
GATE_W = 2 * D_MODEL
D_FF = 4 * D_MODEL
DEC_SEQ = 4
LN_EPS = 1e-5
RMS_EPS = 1e-6
NEG_BIG = -1e30
LB_FLOOR = 1e-30

AQ0, AK0, AV0, HQ0, HF0, HI0, HG0, GT0, IN_COLS = 0, 512, 640, 768, 1280, 1792, 2304, 2816, 4864

CHUNK = 128
HALF = CHUNK // 2
FAST_DECAY_LIMIT = 120.0
FAST_Q_LIMIT = 1e9
SEQ_PER_STEP = 4
ROWS_PER_STEP = SEQ_PER_STEP * DEC_SEQ
IN_PROJ_ROWS = 512
POST_ROWS = 512
POST_GROUP = 256
V7X_VMEM_LIMIT = 56 * 1024 * 1024


def _cparams(*sem):
    return pltpu.CompilerParams(dimension_semantics=sem, vmem_limit_bytes=V7X_VMEM_LIMIT)


def _resident(shape, index_map):
    return pl.BlockSpec(shape, index_map, pipeline_mode=pl.Buffered(1))


def _stacked_out(call, layer, stacked, n_in):
    extra, aliases = [], {}
    if layer > 0:
        for out_idx, prev in stacked.items():
            aliases[n_in + len(extra)] = out_idx
            extra.append(prev)
    return call(extra, aliases)


def _lower_bound(logits_ref, layer):
    lg = logits_ref[...]
    e = jnp.exp(lg - jnp.max(lg, axis=0, keepdims=True))
    p = e / jnp.sum(e, axis=0, keepdims=True)
    lb = jnp.zeros((1, HG_W), F32)
    for i in range(1, layer + 1):
        lb = lb + p[i:i + 1]
    return lb


def _swish(z):
    return z * jax.nn.sigmoid(z)


def _forget_and_key(zf, lb):
    e = jnp.exp(-jnp.abs(zf))
    r = 1.0 / (1.0 + e)
    pos = zf >= 0
    sig_p = jnp.where(pos, r, e * r)
    sig_n = jnp.where(pos, e * r, r)
    logf = jnp.log(jnp.maximum(lb, LB_FLOOR) + (1.0 - lb) * sig_p)
    return logf, (1.0 - lb) * sig_n


def _in_proj_kernel(layer, x_ref, w_ref, bg_ref, logits_ref,
                    q_ref, kv_ref, hq_ref, hk_ref, lf_ref, hv_ref, hg_ref, g_ref, ok_ref, xb_ref):
    xb = x_ref[...].astype(BF16)
    lb = _lower_bound(logits_ref, layer)
    chunks = x_ref.shape[0] // CHUNK
    q_ok = []
    decay_ok = []

    def put_q(z):
        q_ref[...] = (z * (HEAD_DIM ** -0.5)).astype(BF16)

    def put_kv(z):
        kv_ref[...] = z

    def put_hq(z):
        hq = _swish(z)
        hq_ref[...] = hq.astype(BF16)
        for c in range(chunks):
            q_ok.append(jnp.max(jnp.abs(hq[c * CHUNK:(c + 1) * CHUNK]), keepdims=True) <= FAST_Q_LIMIT)

    def put_hf(z):
        logf, key = _forget_and_key(z, lb)
        lf_ref[...] = logf
        hk_ref[...] = key.astype(BF16)
        for lo in range(0, chunks * CHUNK, HALF):
            total = jnp.sum(logf[lo:lo + HALF], axis=0, keepdims=True)
            decay_ok.append(jnp.min(total, keepdims=True) >= -FAST_DECAY_LIMIT)

    def put_hv(z):
        hv_ref[...] = z.astype(BF16)

    def put_hg(z):
        hg_ref[...] = _swish(z)

    def put_gates(lo):
        def put(z):
            g_ref[:, lo:lo + 512] = jax.nn.sigmoid(z + bg_ref[:, lo:lo + 512])
        return put

    stages = [(HF0, HI0, put_hf), (HQ0, HF0, put_hq), (HG0, GT0, put_hg)]
    stages += [(GT0 + lo, GT0 + lo + 512, put_gates(lo)) for lo in range(0, GATE_W, 512)]
    stages += [(AQ0, AK0, put_q), (HI0, HG0, put_hv), (AK0, HQ0, put_kv)]
    pending = None
    xb_ref[...] = xb
    for lo, hi, put in stages:
        z = jnp.dot(xb_ref[...], w_ref[:, lo:hi], preferred_element_type=F32)
        if pending is not None:
            pending[0](pending[1])
        pending = (put, z)
    pending[0](pending[1])
    for c in range(chunks):
        ok = jnp.logical_and(q_ok[c], jnp.logical_and(decay_ok[2 * c], decay_ok[2 * c + 1]))
        ok_ref[c] = jnp.broadcast_to(jnp.where(ok, 1, 0), (1, 128)).astype(jnp.int32)


def _in_proj(x, w_in, b_gate, logits, layer):
    t = x.shape[0]
    tm = min(IN_PROJ_ROWS, t)
    assert t % tm == 0 and tm % CHUNK == 0
    rows = lambda w: pl.BlockSpec((tm, w), lambda i: (i, 0))
    out = lambda w, dt: jax.ShapeDtypeStruct((t, w), dt)
    return pl.pallas_call(
        functools.partial(_in_proj_kernel, layer),
        grid=(t // tm,),
        in_specs=[rows(D_MODEL),
                  _resident((None, D_MODEL, IN_COLS), lambda i: (layer, 0, 0)),
                  _resident((None, 1, GATE_W), lambda i: (layer, 0, 0)),
                  _resident(logits.shape, lambda i: (0, 0))],
        out_specs=[rows(ATT_W), rows(2 * KV_W), rows(HG_W), rows(HG_W), rows(HG_W), rows(HG_W), rows(HG_W),
                   rows(GATE_W), pl.BlockSpec((tm // CHUNK, 1, 128), lambda i: (i, 0, 0))],
        out_shape=[out(ATT_W, BF16), out(2 * KV_W, F32), out(HG_W, BF16), out(HG_W, BF16), out(HG_W, F32),
                   out(HG_W, BF16), out(HG_W, F32), out(GATE_W, F32),
                   jax.ShapeDtypeStruct((t // CHUNK, 1, 128), jnp.int32)],
        scratch_shapes=[pltpu.VMEM((tm, D_MODEL), BF16)],
        compiler_params=_cparams("parallel"),
        name="in_proj",
    )(x, w_in, b_gate, logits)


def _split_kv(t):
    low = lax.broadcasted_iota(jnp.int32, t.shape, 1) < HEAD_DIM
    h0_lo = jnp.where(low, t, 0.0)
    h1_hi = jnp.where(low, 0.0, t)
    h0_hi = pltpu.roll(h0_lo, HEAD_DIM, 1)
    h1_lo = pltpu.roll(h1_hi, HEAD_DIM, 1)
    return ((h0_lo.astype(BF16), h0_hi.astype(BF16)), (h1_lo.astype(BF16), h1_hi.astype(BF16)))


def _window_attention(qb, k, v, mask, sink_ref):
    r = qb.shape[0]
    ks = _split_kv(k)
    vs = _split_kv(v)
    mask2 = jnp.concatenate([mask, mask], axis=0)
    top = lax.broadcasted_iota(jnp.int32, (2 * r, 1), 0) < r
    cols = [None] * 4
    for g in range(KV_HEADS):
        c0, c1 = 2 * g, 2 * g + 1
        qg = jnp.concatenate([qb[:, c0 * 128:(c0 + 1) * 128], qb[:, c1 * 128:(c1 + 1) * 128]], axis=0)
        acc = None
        for half in range(2):
            sink = jnp.where(top, sink_ref[0, 2 * c0 + half], sink_ref[0, 2 * c1 + half])
            s = lax.dot_general(qg, ks[g][half], (((1,), (1,)), ((), ())), preferred_element_type=F32)
            s = jnp.where(mask2, s, NEG_BIG)
            m = jnp.maximum(jnp.max(s, axis=-1, keepdims=True), sink)
            p = jnp.exp(s - m)
            den = jnp.sum(p, axis=-1, keepdims=True) + jnp.exp(sink - m)
            o = jnp.dot(p.astype(BF16), vs[g][half], preferred_element_type=F32) / den
            acc = o if acc is None else acc + o
        cols[c0] = acc[:r]
        cols[c1] = acc[r:]
    return jnp.concatenate(cols, axis=1)


def _attn_prompt_kernel(sink_ref, q_ref, kp_ref, kc_ref, vp_ref, vc_ref, o_ref):
    i = pl.program_id(1)
    k = jnp.concatenate([kp_ref[...], kc_ref[...]], axis=0)
    v = jnp.concatenate([vp_ref[...], vc_ref[...]], axis=0)
    qi = lax.broadcasted_iota(jnp.int32, (WINDOW, 2 * WINDOW), 0)
    kj = lax.broadcasted_iota(jnp.int32, (WINDOW, 2 * WINDOW), 1)
    mask = jnp.logical_and(jnp.logical_and(kj >= qi, kj <= qi + WINDOW), jnp.logical_or(kj >= WINDOW, i > 0))
    o_ref[...] = _window_attention(q_ref[...], k, v, mask, sink_ref).astype(o_ref.dtype)


def _attn_prompt(q, kv, sink, batch, seq):
    nb = seq // WINDOW
    cur = lambda col: (lambda b, i: (b * nb + i, col))
    prev = lambda col: (lambda b, i: (b * nb + jnp.maximum(i - 1, 0), col))
    return pl.pallas_call(
        _attn_prompt_kernel,
        grid=(batch, nb),
        in_specs=[pl.BlockSpec(memory_space=pltpu.SMEM),
                  pl.BlockSpec((WINDOW, ATT_W), cur(0)),
                  pl.BlockSpec((WINDOW, KV_W), prev(0)), pl.BlockSpec((WINDOW, KV_W), cur(0)),
                  pl.BlockSpec((WINDOW, KV_W), prev(1)), pl.BlockSpec((WINDOW, KV_W), cur(1))],
        out_specs=pl.BlockSpec((WINDOW, ATT_W), cur(0)),
        out_shape=jax.ShapeDtypeStruct((batch * seq, ATT_W), BF16),
        compiler_params=_cparams("parallel", "parallel"),
        name="attn_prompt",
    )(sink, q, kv, kv, kv, kv)


def _attn_sample_kernel(sink_ref, q_ref, kvn_ref, ck_ref, cv_ref, *rest):
    o_ref, nk_ref, nv_ref = rest[-3:]
    lane = lax.broadcasted_iota(jnp.int32, (ROWS_PER_STEP, KV_W), 1)
    q16 = q_ref[...].astype(F32)
    blocks = []
    for h in range(N_HEADS):
        c, half, g = h // 2, h % 2, h // (N_HEADS // KV_HEADS)
        own = lane >= HEAD_DIM if half else lane < HEAD_DIM
        qh = jnp.where(own, q16[:, c * 128:(c + 1) * 128], 0.0)
        blocks.append(qh if half == g else pltpu.roll(qh, HEAD_DIM, 1))
    wt = jnp.concatenate(blocks, axis=0).astype(BF16)

    row = lax.broadcasted_iota(jnp.int32, (WINDOW, WINDOW), 0)
    col = lax.broadcasted_iota(jnp.int32, (WINDOW, WINDOW), 1)
    tok = jnp.bitwise_and(row, DEC_SEQ - 1)
    row_seq = jnp.bitwise_and(jnp.right_shift(row, 2), SEQ_PER_STEP - 1)
    pad = jnp.zeros((WINDOW - ROWS_PER_STEP, KV_W), F32)
    kn = jnp.concatenate([kvn_ref[:, :KV_W], pad], axis=0)
    vn = jnp.concatenate([kvn_ref[:, KV_W:], pad], axis=0)
    kn_t = jnp.transpose(kn)
    vn_t = jnp.transpose(vn)

    s_c = jnp.zeros((WINDOW, WINDOW), F32)
    for j in range(SEQ_PER_STEP):
        sj = jnp.dot(wt, ck_ref[j].astype(BF16), preferred_element_type=F32)
        s_c = jnp.where(row_seq == j, sj, s_c)
    s_n = jnp.dot(wt, kn_t.astype(BF16), preferred_element_type=F32)
    s_c = jnp.where(col >= tok, s_c, NEG_BIG)
    new_ok = jnp.logical_and(jnp.right_shift(col, 2) == row_seq, jnp.bitwise_and(col, DEC_SEQ - 1) <= tok)
    s_n = jnp.where(new_ok, s_n, NEG_BIG)
    head = jnp.right_shift(lax.broadcasted_iota(jnp.int32, (WINDOW, 1), 0), 4)
    sink = jnp.zeros((WINDOW, 1), F32)
    for h in range(N_HEADS):
        sink = jnp.where(head == h, sink_ref[0, h], sink)
    m = jnp.maximum(jnp.maximum(jnp.max(s_c, axis=-1, keepdims=True), jnp.max(s_n, axis=-1, keepdims=True)), sink)
    p_c = jnp.exp(s_c - m)
    p_n = jnp.exp(s_n - m)
    den = jnp.sum(p_c, axis=-1, keepdims=True) + jnp.sum(p_n, axis=-1, keepdims=True) + jnp.exp(sink - m)
    p_cb = p_c.astype(BF16)
    o = jnp.dot(p_n.astype(BF16), vn.astype(BF16), preferred_element_type=F32)
    for j in range(SEQ_PER_STEP):
        oj = lax.dot_general(p_cb, cv_ref[j].astype(BF16), (((1,), (1,)), ((), ())), preferred_element_type=F32)
        o = o + jnp.where(row_seq == j, oj, 0.0)
    o = o / den
    outs = []
    for c in range(4):
        lo, hi = o[32 * c:32 * c + 16], o[32 * c + 16:32 * c + 32]
        if c < 2:
            outs.append(jnp.where(lane < HEAD_DIM, lo, pltpu.roll(hi, HEAD_DIM, 1)))
        else:
            outs.append(jnp.where(lane < HEAD_DIM, pltpu.roll(lo, HEAD_DIM, 1), hi))
    o_ref[...] = jnp.concatenate(outs, axis=1).astype(o_ref.dtype)
    for j in range(SEQ_PER_STEP):
        shift = (WINDOW - DEC_SEQ - DEC_SEQ * j) % WINDOW
        for new_t, cache_ref, dst in ((kn_t, ck_ref, nk_ref), (vn_t, cv_ref, nv_ref)):
            dst[j] = jnp.where(col >= WINDOW - DEC_SEQ, pltpu.roll(new_t, shift, 1),
                               pltpu.roll(cache_ref[j], WINDOW - DEC_SEQ, 1))


def _attn_sample(q, kvn, sink, cache_kt, cache_vt, layer, prev_k, prev_v):
    depth, db = cache_kt.shape[:2]
    cache_spec = pl.BlockSpec((None, SEQ_PER_STEP, KV_W, WINDOW), lambda s: (layer, s, 0, 0))
    rows = lambda w: pl.BlockSpec((ROWS_PER_STEP, w), lambda s: (s, 0))
    n_in = 5

    def call(extra, aliases):
        return pl.pallas_call(
            _attn_sample_kernel,
            grid=(db // SEQ_PER_STEP,),
            in_specs=[pl.BlockSpec(memory_space=pltpu.SMEM), rows(ATT_W), rows(2 * KV_W), cache_spec, cache_spec]
            + [pl.BlockSpec(memory_space=pl.ANY)] * len(extra),
            out_specs=[rows(ATT_W), cache_spec, cache_spec],
            out_shape=[jax.ShapeDtypeStruct((db * DEC_SEQ, ATT_W), BF16),
                       jax.ShapeDtypeStruct((depth, db, KV_W, WINDOW), F32),
                       jax.ShapeDtypeStruct((depth, db, KV_W, WINDOW), F32)],
            input_output_aliases=aliases,
            compiler_params=_cparams("parallel"),
            name="attn_sample",
        )(sink, q, kvn, cache_kt, cache_vt, *extra)

    return _stacked_out(call, layer, {1: prev_k, 2: prev_v}, n_in)


def _hgrn_out(o, g_act, normw):
    ms = jnp.mean(o * o, axis=-1, keepdims=True)
    return o * lax.rsqrt(ms + RMS_EPS) * normw * g_act


def _cumsum_rows(x, tri):
    hi = x.astype(BF16)
    r1 = x - hi.astype(F32)
    mid = r1.astype(BF16)
    lo = (r1 - mid.astype(F32)).astype(BF16)
    dot = lambda t: jnp.dot(tri, t, preferred_element_type=F32)
    return dot(hi) + dot(mid) + dot(lo)


def _heads(t):
    return [t[:, h * HG_D:(h + 1) * HG_D] for h in range(HG_HEADS)]


def _level_reference(b, n, row):
    if n == 2:
        return jnp.where(jnp.bitwise_and(row, 1) == 0, b, pltpu.roll(b, 1, 0))
    if n == 4:
        m4 = jnp.bitwise_and(row, 3)
        return jnp.where(m4 == 0, pltpu.roll(b, CHUNK - 1, 0),
                         jnp.where(m4 == 1, b, jnp.where(m4 == 2, pltpu.roll(b, 1, 0), pltpu.roll(b, 2, 0))))
    bounds = [m * n + n // 2 - 1 for m in range(CHUNK // n)]
    pieces = [jnp.broadcast_to(b[t:t + 1], (n, b.shape[1])) for t in bounds]
    return pieces[0] if len(pieces) == 1 else jnp.concatenate(pieces, axis=0)


def _scores(q, k, e_q, e_k):
    qs, ks = _heads(q * e_q.astype(BF16)), _heads(k * e_k.astype(BF16))
    return [lax.dot_general(a, c, (((1,), (1,)), ((), ())), preferred_element_type=F32) for a, c in zip(qs, ks)]


def _intra_chunk_scores(q, k, b, lvl, fast):
    if fast:
        b_mid, b_end = b[HALF - 1:HALF], b[CHUNK - 1:CHUNK]
        r = jnp.concatenate([jnp.broadcast_to(0.5 * b_mid, (HALF, HG_W)),
                             jnp.broadcast_to(0.5 * (b_mid + b_end), (HALF, HG_W))], axis=0)
        e = jnp.exp(b - r)
        e6 = jnp.exp(-jnp.abs(b - b_mid))
        return [jnp.where(lvl == 6, across, jnp.where(lvl >= -1, within, 0.0))
                for within, across in zip(_scores(q, k, e, 1.0 / e), _scores(q, k, e6, e6))]
    row = lax.broadcasted_iota(jnp.int32, (CHUNK, HG_W), 0)
    atts = [jnp.where(lvl == -1, jnp.sum(qk, axis=-1, keepdims=True), 0.0) for qk in _heads((q * k).astype(F32))]
    for level in range(7):
        e = jnp.exp(-jnp.abs(b - _level_reference(b, 2 << level, row)))
        atts = [jnp.where(lvl == level, p, att) for p, att in zip(_scores(q, k, e, e), atts)]
    return atts


def _hgrn_chunk(fast, hq_ref, hk_ref, lf_ref, hv_ref, hg_ref, normw_ref, o_ref, st_ref):
    row = lax.broadcasted_iota(jnp.int32, (CHUNK, CHUNK), 0)
    col = lax.broadcasted_iota(jnp.int32, (CHUNK, CHUNK), 1)
    tri = jnp.where(row >= col, 1.0, 0.0).astype(BF16)
    lvl = jnp.where(row >= col, 31 - lax.clz(jnp.bitwise_xor(row, col)), -2)
    q, k, v = hq_ref[...], hk_ref[...], hv_ref[...]
    b = _cumsum_rows(lf_ref[...], tri)
    b_end = b[CHUNK - 1:CHUNK]
    atts = _intra_chunk_scores(q, k, b, lvl, fast)
    q_in = _heads(q * jnp.exp(b).astype(BF16))
    k_out = _heads(k.astype(F32) * jnp.exp(b_end - b))
    decay = _heads(jnp.exp(b_end))
    vs, gs = _heads(v), _heads(hg_ref[...])
    s0 = [st_ref[0, h] for h in range(HG_HEADS)]
    outs = []
    for h in range(HG_HEADS):
        o = jnp.dot(q_in[h], s0[h].astype(BF16), preferred_element_type=F32)
        o = o + jnp.dot(atts[h].astype(BF16), vs[h], preferred_element_type=F32)
        outs.append(_hgrn_out(o, gs[h], normw_ref[...]))
    o_ref[...] = jnp.concatenate(outs, axis=1).astype(o_ref.dtype)
    for h in range(HG_HEADS):
        k_t = jnp.transpose(k_out[h]).astype(BF16)
        decay_col = jnp.transpose(jnp.broadcast_to(decay[h], (HG_D, HG_D)))
        st_ref[0, h] = decay_col * s0[h] + jnp.dot(k_t, vs[h], preferred_element_type=F32)


def _hgrn_prompt_kernel(ok_ref, *refs):
    st_ref = refs[-1]

    @pl.when(pl.program_id(1) == 0)
    def _():
        st_ref[...] = jnp.zeros_like(st_ref)

    ok = ok_ref[pl.program_id(0) * pl.num_programs(1) + pl.program_id(1)] != 0

    @pl.when(ok)
    def _():
        _hgrn_chunk(True, *refs)

    @pl.when(jnp.logical_not(ok))
    def _():
        _hgrn_chunk(False, *refs)


def _hgrn_prompt(chunk_ok, hq, hk, lf, hv, hg, normw, batch, seq):
    nc = seq // CHUNK
    rows = pl.BlockSpec((CHUNK, HG_W), lambda b, i, ok: (b * nc + i, 0))
    return pl.pallas_call(
        _hgrn_prompt_kernel,
        grid_spec=pltpu.PrefetchScalarGridSpec(
            num_scalar_prefetch=1,
            grid=(batch, nc),
            in_specs=[rows, rows, rows, rows, rows, pl.BlockSpec((1, HG_D), lambda b, i, ok: (0, 0))],
            out_specs=[rows, pl.BlockSpec((1, HG_HEADS, HG_D, HG_D), lambda b, i, ok: (b, 0, 0, 0))]),
        out_shape=[jax.ShapeDtypeStruct((batch * seq, HG_W), BF16),
                   jax.ShapeDtypeStruct((batch, HG_HEADS, HG_D, HG_D), F32)],
        compiler_params=_cparams("parallel", "arbitrary"),
        name="hgrn_prompt",
    )(chunk_ok, hq, hk, lf, hv, hg, normw)


def _hgrn_sample_kernel(hq_ref, hk_ref, lf_ref, hv_ref, hg_ref, normw_ref, st_ref, *rest):
    o_ref, ns_ref = rest[-2:]
    q, k, v, logf = hq_ref[...].astype(F32), hk_ref[...].astype(F32), hv_ref[...].astype(F32), lf_ref[...]
    row = lax.broadcasted_iota(jnp.int32, (ROWS_PER_STEP, HG_W), 0)
    tok = jnp.bitwise_and(row, DEC_SEQ - 1)
    seq_of_row = jnp.right_shift(row[:, :HG_D], 2)
    down = lambda t, d: pltpu.roll(t, d, 0)
    up = lambda t, d: pltpu.roll(t, ROWS_PER_STEP - d, 0)
    b = logf
    for d in range(1, DEC_SEQ):
        b = b + jnp.where(tok >= d, down(logf, d), 0.0)
    b_end = jnp.where(tok == 3, b, jnp.where(tok == 2, up(b, 1), jnp.where(tok == 1, up(b, 2), up(b, 3))))
    qe = (q * jnp.exp(b)).astype(BF16)
    kd = k * jnp.exp(b_end - b)
    e_end = jnp.exp(b_end)
    intra = [jnp.zeros((ROWS_PER_STEP, HG_D), F32) for _ in range(HG_HEADS)]
    for d in range(DEC_SEQ):
        kd_, bd_, vd_ = (k, b, v) if d == 0 else (down(k, d), down(b, d), down(v, d))
        w = jnp.where(tok >= d, q * kd_ * jnp.exp(jnp.minimum(b - bd_, 0.0)), 0.0)
        for h in range(HG_HEADS):
            sl = slice(h * HG_D, (h + 1) * HG_D)
            intra[h] = intra[h] + jnp.sum(w[:, sl], axis=-1, keepdims=True) * vd_[:, sl]
    pad = jnp.zeros((HG_D - 2 * ROWS_PER_STEP, HG_D), F32)
    colh = lax.broadcasted_iota(jnp.int32, (HG_D, HG_D), 1)
    normw = normw_ref[...]
    for h in range(HG_HEADS):
        sl = slice(h * HG_D, (h + 1) * HG_D)
        vb = jnp.concatenate([v[:, sl], jnp.zeros((HG_D - ROWS_PER_STEP, HG_D), F32)], axis=0).astype(BF16)
        tt = jnp.transpose(jnp.concatenate([kd[:, sl], e_end[:, sl], pad], axis=0))
        inter = jnp.zeros((ROWS_PER_STEP, HG_D), F32)
        for j in range(SEQ_PER_STEP):
            s0 = st_ref[j, h]
            oi = jnp.dot(qe[:, sl], s0.astype(BF16), preferred_element_type=F32)
            inter = jnp.where(seq_of_row == j, oi, inter)
            kd_t = jnp.where(jnp.right_shift(colh, 2) == j, tt, 0.0).astype(BF16)
            e_col = ROWS_PER_STEP + DEC_SEQ * j
            decay = jnp.broadcast_to(tt[:, e_col:e_col + 1], (HG_D, HG_D))
            ns_ref[j, h] = decay * s0 + jnp.dot(kd_t, vb, preferred_element_type=F32)
        o_ref[:, sl] = _hgrn_out(inter + intra[h], hg_ref[:, sl], normw).astype(o_ref.dtype)


def _hgrn_sample(hq, hk, lf, hv, hg, normw, state, layer, prev_state):
    depth, db = state.shape[:2]
    rows = pl.BlockSpec((ROWS_PER_STEP, HG_W), lambda s: (s, 0))
    st_spec = pl.BlockSpec((None, SEQ_PER_STEP, HG_HEADS, HG_D, HG_D), lambda s: (layer, s, 0, 0, 0))
    n_in = 7

    def call(extra, aliases):
        return pl.pallas_call(
            _hgrn_sample_kernel,
            grid=(db // SEQ_PER_STEP,),
            in_specs=[rows, rows, rows, rows, rows, pl.BlockSpec((1, HG_D), lambda s: (0, 0)), st_spec]
            + [pl.BlockSpec(memory_space=pl.ANY)] * len(extra),
            out_specs=[rows, st_spec],
            out_shape=[jax.ShapeDtypeStruct((db * DEC_SEQ, HG_W), BF16), jax.ShapeDtypeStruct(state.shape, F32)],
            input_output_aliases=aliases,
            compiler_params=_cparams("parallel"),
            name="hgrn_sample",
        )(hq, hk, lf, hv, hg, normw, state, *extra)

    return _stacked_out(call, layer, {1: prev_state}, n_in)


def _layer_norm(y, g, b):
    mu = jnp.mean(y, axis=-1, keepdims=True)
    yc = y - mu
    var = jnp.mean(yc * yc, axis=-1, keepdims=True)
    return yc * lax.rsqrt(var + LN_EPS) * g + b


def _post_kernel(alpha, a_ref, h_ref, g_ref, x_ref, wua_ref, wuh_ref, wo_ref, l1g_ref, l1b_ref,
                 wf1_ref, wf2_ref, l2g_ref, l2b_ref, o_ref):
    groups = [slice(r0, r0 + POST_GROUP) for r0 in range(0, x_ref.shape[0], POST_GROUP)]
    dot = lambda lhs, w_ref: jnp.dot(lhs, w_ref[...], preferred_element_type=F32)
    ua = [dot(a_ref[rs], wua_ref) for rs in groups]
    uh = [dot(h_ref[rs], wuh_ref) for rs in groups]
    merged = [(g_ref[rs, :D_MODEL] * a + g_ref[rs, D_MODEL:] * h).astype(BF16) for rs, a, h in zip(groups, ua, uh)]
    m = [dot(t, wo_ref) for t in merged]
    x1 = [_layer_norm(alpha * x_ref[rs] + t, l1g_ref[...], l1b_ref[...]) for rs, t in zip(groups, m)]
    hid = [jnp.square(jnp.maximum(dot(t.astype(BF16), wf1_ref), 0.0)).astype(BF16) for t in x1]
    ff = [dot(t, wf2_ref) for t in hid]
    for rs, t, f in zip(groups, x1, ff):
        o_ref[rs] = _layer_norm(alpha * t + f, l2g_ref[...], l2b_ref[...])


def _post(a, h, g, x, p, layer, alpha):
    t = x.shape[0]
    tm = min(POST_ROWS, t)
    assert t % tm == 0
    rows = lambda w: pl.BlockSpec((tm, w), lambda i: (i, 0))
    vec = lambda w: _resident((None, 1, w), lambda i: (layer, 0, 0))
    mat = lambda r, c: _resident((None, r, c), lambda i: (layer, 0, 0))
    return pl.pallas_call(
        functools.partial(_post_kernel, alpha),
        grid=(t // tm,),
        in_specs=[rows(ATT_W), rows(HG_W), rows(GATE_W), rows(D_MODEL),
                  mat(ATT_W, D_MODEL), mat(HG_W, D_MODEL), mat(D_MODEL, D_MODEL),
                  vec(D_MODEL), vec(D_MODEL), mat(D_MODEL, D_FF), mat(D_FF, D_MODEL), vec(D_MODEL), vec(D_MODEL)],
        out_specs=rows(D_MODEL),
        out_shape=jax.ShapeDtypeStruct((t, D_MODEL), F32),
        compiler_params=_cparams("parallel"),
        name="post",
    )(a, h, g, x, p["w_up_attn"], p["w_up_hgrn"], p["w_out"], p["ln1_g"], p["ln1_b"],
      p["w_ff1"], p["w_ff2"], p["ln2_g"], p["ln2_b"])


def kernel(x_prompt, x_sample, cache_k, cache_v, state_hgrn, w_in, b_gate, attn_sink, hgrn_lb_logits, hgrn_norm_w,
           w_up_attn, w_up_hgrn, w_out, ln1_g, ln1_b, w_ff1, w_ff2, ln2_g, ln2_b):
    depth = w_in.shape[0]
    batch, seq, _ = x_prompt.shape
    db, dec_seq, _ = x_sample.shape
    assert dec_seq == DEC_SEQ and seq % CHUNK == 0 and db % SEQ_PER_STEP == 0
    assert cache_k.shape[2:] == (WINDOW, KV_HEADS, HEAD_DIM)
    alpha = (2 * depth) ** 0.25

    vec = lambda t: t.reshape(depth, 1, t.shape[-1])
    b_gate3 = vec(b_gate)
    w_in_b = w_in.astype(BF16)
    p = dict(w_up_attn=w_up_attn.astype(BF16), w_up_hgrn=w_up_hgrn.astype(BF16), w_out=w_out.astype(BF16),
             ln1_g=vec(ln1_g), ln1_b=vec(ln1_b), w_ff1=w_ff1.astype(BF16), w_ff2=w_ff2.astype(BF16),
             ln2_g=vec(ln2_g), ln2_b=vec(ln2_b))
    to_t = lambda c: jnp.transpose(c, (0, 1, 3, 4, 2)).reshape(depth, db, KV_W, WINDOW)
    from_t = lambda c: jnp.transpose(c.reshape(depth, db, KV_HEADS, HEAD_DIM, WINDOW), (0, 1, 4, 2, 3))
    ck_t, cv_t = to_t(cache_k), to_t(cache_v)

    xp = x_prompt.reshape(batch * seq, D_MODEL)
    xs = x_sample.reshape(db * DEC_SEQ, D_MODEL)
    pk, pv, ps = [], [], []
    sk = sv = ss = None
    for l in range(depth):
        sink = attn_sink[l].reshape(1, N_HEADS)
        normw = hgrn_norm_w[l].reshape(1, HG_D)
        q, kv, hq, hk, lf, hv, hg, g, chunk_ok = _in_proj(xp, w_in_b, b_gate3, hgrn_lb_logits, l)
        a = _attn_prompt(q, kv, sink, batch, seq)
        h, st = _hgrn_prompt(chunk_ok[:, 0, 0], hq, hk, lf, hv, hg, normw, batch, seq)
        xp = _post(a, h, g, xp, p, l, alpha)
        win = kv.reshape(batch, seq, 2 * KV_W)[:, seq - WINDOW:].reshape(batch, WINDOW, 2, KV_HEADS, HEAD_DIM)
        pk.append(win[:, :, 0])
        pv.append(win[:, :, 1])
        ps.append(st)
        q, kvn, hq, hk, lf, hv, hg, g, _ = _in_proj(xs, w_in_b, b_gate3, hgrn_lb_logits, l)
        a, sk, sv = _attn_sample(q, kvn, sink, ck_t, cv_t, l, sk, sv)
        h, ss = _hgrn_sample(hq, hk, lf, hv, hg, normw, state_hgrn, l, ss)
        xs = _post(a, h, g, xs, p, l, alpha)
    return (xp.reshape(batch, seq, D_MODEL), xs.reshape(db, DEC_SEQ, D_MODEL), jnp.stack(pk), jnp.stack(pv),
            jnp.stack(ps), from_t(sk), from_t(sv), ss)
```

```python
import functools

import jax
import jax.numpy as jnp
from jax import lax
from jax.experimental import pallas as pl
from jax.experimental.pallas import tpu as pltpu

F32 = jnp.float32
BF16 = jnp.bfloat16

D_MODEL = 1024
WINDOW = 128
N_HEADS = 8
KV_HEADS = 2
HEAD_DIM = 64
ATT_W = N_HEADS * HEAD_DIM
KV_W = KV_HEADS * HEAD_DIM
HG_HEADS = 4
HG_D = 128
HG_W = HG_HEADS * HG_D
GATE_W = 2 * D_MODEL
D_FF = 4 * D_MODEL
DEC_SEQ = 4
LN_EPS = 1e-5
RMS_EPS = 1e-6
NEG_BIG = -1e30
LB_FLOOR = 1e-30
LOG2E = 1.4426950408889634
QK_SCALE = HEAD_DIM ** -0.5 * LOG2E

AQ0, AK0, AV0, HQ0, HF0, HI0, HG0, GT0, IN_COLS = 0, 512, 640, 768, 1280, 1792, 2304, 2816, 4864

ATT_BLOCKS = 2
CHUNK = 128
HGRN_CHUNKS = 2
HALF = CHUNK // 2
FAST_DECAY_LIMIT = 120.0
FAST_Q_LIMIT = 1e9
SEQ_PER_STEP = 4
ROWS_PER_STEP = SEQ_PER_STEP * DEC_SEQ
IN_PROJ_ROWS = 512
POST_ROWS = 512
POST_GROUP = 256
V7X_VMEM_LIMIT = 56 * 1024 * 1024


def _cparams(*sem):
    return pltpu.CompilerParams(dimension_semantics=sem, vmem_limit_bytes=V7X_VMEM_LIMIT)


def _resident(shape, index_map):
    return pl.BlockSpec(shape, index_map, pipeline_mode=pl.Buffered(1))


def _stacked_out(call, layer, stacked, n_in):
    extra, aliases = [], {}
    if layer > 0:
        for out_idx, prev in stacked.items():
            aliases[n_in + len(extra)] = out_idx
            extra.append(prev)
    return call(extra, aliases)


def _lower_bound(logits_ref, layer):
    lg = logits_ref[...]
    e = jnp.exp(lg - jnp.max(lg, axis=0, keepdims=True))
    p = e / jnp.sum(e, axis=0, keepdims=True)
    lb = jnp.zeros((1, HG_W), F32)
    for i in range(1, layer + 1):
        lb = lb + p[i:i + 1]
    return lb


def _swish(z):
    return z * jax.nn.sigmoid(z)


def _forget_and_key(zf, lb):
    e = jnp.exp(-jnp.abs(zf))
    r = 1.0 / (1.0 + e)
    pos = zf >= 0
    sig_p = jnp.where(pos, r, e * r)
    sig_n = jnp.where(pos, e * r, r)
    logf = jnp.log(jnp.maximum(lb, LB_FLOOR) + (1.0 - lb) * sig_p)
    return logf, (1.0 - lb) * sig_n


def _in_proj_kernel(layer, x_ref, w_ref, bg_ref, logits_ref,
                    q_ref, kv_ref, ks_ref, vs_ref, hq_ref, hk_ref, lf_ref, hv_ref, hg_ref, g_ref, ok_ref, xb_ref):
    xb = x_ref[...].astype(BF16)
    lb = _lower_bound(logits_ref, layer)
    chunks = x_ref.shape[0] // CHUNK
    q_ok = []
    decay_ok = []

    def put_q(z):
        q_ref[...] = (z * QK_SCALE).astype(BF16)

    def put_kv(z):
        kv_ref[...] = z
        ks_ref[...] = _split_heads(z[:, :KV_W])
        vs_ref[...] = _split_heads(z[:, KV_W:])

    def put_hq(z):
        hq = _swish(z)
        hq_ref[...] = hq.astype(BF16)
        for c in range(chunks):
            q_ok.append(jnp.max(jnp.abs(hq[c * CHUNK:(c + 1) * CHUNK]), keepdims=True) <= FAST_Q_LIMIT)

    def put_hf(z):
        logf, key = _forget_and_key(z, lb)
        lf_ref[...] = logf
        hk_ref[...] = key.astype(BF16)
        for lo in range(0, chunks * CHUNK, HALF):
            total = jnp.sum(logf[lo:lo + HALF], axis=0, keepdims=True)
            decay_ok.append(jnp.min(total, keepdims=True) >= -FAST_DECAY_LIMIT)

    def put_hv(z):
        hv_ref[...] = z.astype(BF16)

    def put_hg(z):
        hg_ref[...] = _swish(z)

    def put_gates(lo):
        def put(z):
            g_ref[:, lo:lo + 512] = jax.nn.sigmoid(z + bg_ref[:, lo:lo + 512])
        return put

    stages = [(HF0, HI0, put_hf), (HQ0, HF0, put_hq), (HG0, GT0, put_hg)]
    stages += [(GT0 + lo, GT0 + lo + 512, put_gates(lo)) for lo in range(0, GATE_W, 512)]
    stages += [(AQ0, AK0, put_q), (HI0, HG0, put_hv), (AK0, HQ0, put_kv)]
    pending = None
    xb_ref[...] = xb
    for lo, hi, put in stages:
        z = jnp.dot(xb_ref[...], w_ref[:, lo:hi], preferred_element_type=F32)
        if pending is not None:
            pending[0](pending[1])
        pending = (put, z)
    pending[0](pending[1])
    for c in range(chunks):
        ok = jnp.logical_and(q_ok[c], jnp.logical_and(decay_ok[2 * c], decay_ok[2 * c + 1]))
        ok_ref[c] = jnp.broadcast_to(jnp.where(ok, 1, 0), (1, 128)).astype(jnp.int32)


def _split_heads(t):
    low = lax.broadcasted_iota(jnp.int32, t.shape, 1) < HEAD_DIM
    h0_lo = jnp.where(low, t, 0.0)
    h1_hi = jnp.where(low, 0.0, t)
    blocks = [h0_lo, pltpu.roll(h0_lo, HEAD_DIM, 1), pltpu.roll(h1_hi, HEAD_DIM, 1), h1_hi]
    return jnp.concatenate(blocks, axis=1).astype(BF16)


def _in_proj(x, w_in, b_gate, logits, layer):
    t = x.shape[0]
    tm = min(IN_PROJ_ROWS, t)
    assert t % tm == 0 and tm % CHUNK == 0
    rows = lambda w: pl.BlockSpec((tm, w), lambda i: (i, 0))
    out = lambda w, dt: jax.ShapeDtypeStruct((t, w), dt)
    return pl.pallas_call(
        functools.partial(_in_proj_kernel, layer),
        grid=(t // tm,),
        in_specs=[rows(D_MODEL),
                  _resident((None, D_MODEL, IN_COLS), lambda i: (layer, 0, 0)),
                  _resident((None, 1, GATE_W), lambda i: (layer, 0, 0)),
                  _resident(logits.shape, lambda i: (0, 0))],
        out_specs=[rows(ATT_W), rows(2 * KV_W), rows(4 * KV_W), rows(4 * KV_W),
                   rows(HG_W), rows(HG_W), rows(HG_W), rows(HG_W), rows(HG_W),
                   rows(GATE_W), pl.BlockSpec((tm // CHUNK, 1, 128), lambda i: (i, 0, 0))],
        out_shape=[out(ATT_W, BF16), out(2 * KV_W, F32), out(4 * KV_W, BF16), out(4 * KV_W, BF16),
                   out(HG_W, BF16), out(HG_W, BF16), out(HG_W, F32), out(HG_W, BF16), out(HG_W, F32),
                   out(GATE_W, F32),
                   jax.ShapeDtypeStruct((t // CHUNK, 1, 128), jnp.int32)],
        scratch_shapes=[pltpu.VMEM((tm, D_MODEL), BF16)],
        compiler_params=_cparams("parallel"),
        name="in_proj",
    )(x, w_in, b_gate, logits)


def _attn_prompt_kernel(sink_ref, bias_ref, q_ref, kp_ref, kc_ref, vp_ref, vc_ref, o_ref):
    r = WINDOW
    first_bias = bias_ref[jnp.minimum(pl.program_id(1), 1)]
    top = lax.broadcasted_iota(jnp.int32, (2 * r, 1), 0) < r
    combos = [(n, g, half) for n in range(ATT_BLOCKS) for g in range(KV_HEADS) for half in range(2)]
    lanes = lambda g, half: slice((2 * g + half) * KV_W, (2 * g + half + 1) * KV_W)

    def kv_rows(prev_ref, cur_ref, n, g, half):
        before = prev_ref[:, lanes(g, half)] if n == 0 else cur_ref[(n - 1) * r:n * r, lanes(g, half)]
        return jnp.concatenate([before, cur_ref[n * r:(n + 1) * r, lanes(g, half)]], axis=0)

    def q_rows(n, g):
        return jnp.concatenate([q_ref[n * r:(n + 1) * r, 2 * g * 128:(2 * g + 1) * 128],
                                q_ref[n * r:(n + 1) * r, (2 * g + 1) * 128:(2 * g + 2) * 128]], axis=0)

    bias = [jnp.concatenate([b, b], axis=0) for b in [first_bias] + [bias_ref[1]] * (ATT_BLOCKS - 1)]
    keys = [kv_rows(kp_ref, kc_ref, *c) for c in combos]
    vals = [kv_rows(vp_ref, vc_ref, *c) for c in combos]
    sink = [jnp.where(top, sink_ref[0, 4 * g + half], sink_ref[0, 4 * g + 2 + half]) * LOG2E for _, g, half in combos]
    s = [lax.dot_general(q_rows(n, g), k, (((1,), (1,)), ((), ())), preferred_element_type=F32) + bias[n]
         for (n, g, _), k in zip(combos, keys)]
    m = [jnp.maximum(jnp.max(t, axis=-1, keepdims=True), sk) for t, sk in zip(s, sink)]
    p = [jnp.exp2(t - mx) for t, mx in zip(s, m)]
    den = [jnp.sum(t, axis=-1, keepdims=True) + jnp.exp2(sk - mx) for t, sk, mx in zip(p, sink, m)]
    o = [jnp.dot(t.astype(BF16), v, preferred_element_type=F32) / d for t, v, d in zip(p, vals, den)]
    for n in range(ATT_BLOCKS):
        cols = []
        for g in range(KV_HEADS):
            acc = o[4 * n + 2 * g] + o[4 * n + 2 * g + 1]
            cols += [acc[:r], acc[r:]]
        o_ref[n * r:(n + 1) * r, :] = jnp.concatenate(cols, axis=1).astype(o_ref.dtype)


def _band_bias():
    qi = lax.broadcasted_iota(jnp.int32, (WINDOW, 2 * WINDOW), 0)
    kj = lax.broadcasted_iota(jnp.int32, (WINDOW, 2 * WINDOW), 1)
    band = jnp.logical_and(kj >= qi, kj <= qi + WINDOW)
    first = jnp.logical_and(band, kj >= WINDOW)
    return jnp.where(jnp.stack([first, band]), 0.0, NEG_BIG).astype(F32)


def _attn_prompt(q, ks, vs, sink, batch, seq):
    steps = seq // (ATT_BLOCKS * WINDOW)
    assert seq % (ATT_BLOCKS * WINDOW) == 0
    cur = lambda b, i: (b * steps + i, 0)
    prev = lambda b, i: ((b * steps + i) * ATT_BLOCKS - jnp.minimum(i, 1), 0)
    prev_spec = pl.BlockSpec((WINDOW, 4 * KV_W), prev)
    cur_spec = pl.BlockSpec((ATT_BLOCKS * WINDOW, 4 * KV_W), cur)
    return pl.pallas_call(
        _attn_prompt_kernel,
        grid=(batch, steps),
        in_specs=[pl.BlockSpec(memory_space=pltpu.SMEM),
                  _resident((2, WINDOW, 2 * WINDOW), lambda b, i: (0, 0, 0)),
                  pl.BlockSpec((ATT_BLOCKS * WINDOW, ATT_W), cur),
                  prev_spec, cur_spec, prev_spec, cur_spec],
        out_specs=pl.BlockSpec((ATT_BLOCKS * WINDOW, ATT_W), cur),
        out_shape=jax.ShapeDtypeStruct((batch * seq, ATT_W), BF16),
        compiler_params=_cparams("parallel", "parallel"),
        name="attn_prompt",
    )(sink, _band_bias(), q, ks, ks, vs, vs)


def _attn_sample_kernel(sink_ref, q_ref, kvn_ref, ck_ref, cv_ref, *rest):
    o_ref, nk_ref, nv_ref = rest[-3:]
    lane = lax.broadcasted_iota(jnp.int32, (ROWS_PER_STEP, KV_W), 1)
    q16 = q_ref[...].astype(F32)
    blocks = []
    for h in range(N_HEADS):
        c, half, g = h // 2, h % 2, h // (N_HEADS // KV_HEADS)
        own = lane >= HEAD_DIM if half else lane < HEAD_DIM
        qh = jnp.where(own, q16[:, c * 128:(c + 1) * 128], 0.0)
        blocks.append(qh if half == g else pltpu.roll(qh, HEAD_DIM, 1))
    wt = jnp.concatenate(blocks, axis=0).astype(BF16)

    row = lax.broadcasted_iota(jnp.int32, (WINDOW, WINDOW), 0)
    col = lax.broadcasted_iota(jnp.int32, (WINDOW, WINDOW), 1)
    tok = jnp.bitwise_and(row, DEC_SEQ - 1)
    row_seq = jnp.bitwise_and(jnp.right_shift(row, 2), SEQ_PER_STEP - 1)
    pad = jnp.zeros((WINDOW - ROWS_PER_STEP, KV_W), F32)
    kn = jnp.concatenate([kvn_ref[:, :KV_W], pad], axis=0)
    vn = jnp.concatenate([kvn_ref[:, KV_W:], pad], axis=0)
    kn_t = jnp.transpose(kn)
    vn_t = jnp.transpose(vn)

    s_c = jnp.zeros((WINDOW, WINDOW), F32)
    for j in range(SEQ_PER_STEP):
        sj = jnp.dot(wt, ck_ref[j].astype(BF16), preferred_element_type=F32)
        s_c = jnp.where(row_seq == j, sj, s_c)
    s_n = jnp.dot(wt, kn_t.astype(BF16), preferred_element_type=F32)
    s_c = jnp.where(col >= tok, s_c, NEG_BIG)
    new_ok = jnp.logical_and(jnp.right_shift(col, 2) == row_seq, jnp.bitwise_and(col, DEC_SEQ - 1) <= tok)
    s_n = jnp.where(new_ok, s_n, NEG_BIG)
    head = jnp.right_shift(lax.broadcasted_iota(jnp.int32, (WINDOW, 1), 0), 4)
    sink = jnp.zeros((WINDOW, 1), F32)
    for h in range(N_HEADS):
        sink = jnp.where(head == h, sink_ref[0, h] * LOG2E, sink)
    m = jnp.maximum(jnp.maximum(jnp.max(s_c, axis=-1, keepdims=True), jnp.max(s_n, axis=-1, keepdims=True)), sink)
    p_c = jnp.exp2(s_c - m)
    p_n = jnp.exp2(s_n - m)
    den = jnp.sum(p_c, axis=-1, keepdims=True) + jnp.sum(p_n, axis=-1, keepdims=True) + jnp.exp2(sink - m)
    p_cb = p_c.astype(BF16)
    o = jnp.dot(p_n.astype(BF16), vn.astype(BF16), preferred_element_type=F32)
    for j in range(SEQ_PER_STEP):
        oj = lax.dot_general(p_cb, cv_ref[j].astype(BF16), (((1,), (1,)), ((), ())), preferred_element_type=F32)
        o = o + jnp.where(row_seq == j, oj, 0.0)
    o = o / den
    outs = []
    for c in range(4):
        lo, hi = o[32 * c:32 * c + 16], o[32 * c + 16:32 * c + 32]
        if c < 2:
            outs.append(jnp.where(lane < HEAD_DIM, lo, pltpu.roll(hi, HEAD_DIM, 1)))
        else:
            outs.append(jnp.where(lane < HEAD_DIM, pltpu.roll(lo, HEAD_DIM, 1), hi))
    o_ref[...] = jnp.concatenate(outs, axis=1).astype(o_ref.dtype)
    for j in range(SEQ_PER_STEP):
        shift = (WINDOW - DEC_SEQ - DEC_SEQ * j) % WINDOW
        for new_t, cache_ref, dst in ((kn_t, ck_ref, nk_ref), (vn_t, cv_ref, nv_ref)):
            dst[j] = jnp.where(col >= WINDOW - DEC_SEQ, pltpu.roll(new_t, shift, 1),
                               pltpu.roll(cache_ref[j], WINDOW - DEC_SEQ, 1))


def _attn_sample(q, kvn, sink, cache_kt, cache_vt, layer, prev_k, prev_v):
    depth, db = cache_kt.shape[:2]
    cache_spec = pl.BlockSpec((None, SEQ_PER_STEP, KV_W, WINDOW), lambda s: (layer, s, 0, 0))
    rows = lambda w: pl.BlockSpec((ROWS_PER_STEP, w), lambda s: (s, 0))
    n_in = 5

    def call(extra, aliases):
        return pl.pallas_call(
            _attn_sample_kernel,
            grid=(db // SEQ_PER_STEP,),
            in_specs=[pl.BlockSpec(memory_space=pltpu.SMEM), rows(ATT_W), rows(2 * KV_W), cache_spec, cache_spec]
            + [pl.BlockSpec(memory_space=pl.ANY)] * len(extra),
            out_specs=[rows(ATT_W), cache_spec, cache_spec],
            out_shape=[jax.ShapeDtypeStruct((db * DEC_SEQ, ATT_W), BF16),
                       jax.ShapeDtypeStruct((depth, db, KV_W, WINDOW), F32),
                       jax.ShapeDtypeStruct((depth, db, KV_W, WINDOW), F32)],
            input_output_aliases=aliases,
            compiler_params=_cparams("parallel"),
            name="attn_sample",
        )(sink, q, kvn, cache_kt, cache_vt, *extra)

    return _stacked_out(call, layer, {1: prev_k, 2: prev_v}, n_in)


def _hgrn_out(o, g_act, normw):
    ms = jnp.mean(o * o, axis=-1, keepdims=True)
    return o * lax.rsqrt(ms + RMS_EPS) * normw * g_act


def _cumsum_rows(x, tri):
    hi = x.astype(BF16)
    r1 = x - hi.astype(F32)
    mid = r1.astype(BF16)
    lo = (r1 - mid.astype(F32)).astype(BF16)
    dot = lambda t: jnp.dot(tri, t, preferred_element_type=F32)
    return dot(hi) + dot(mid) + dot(lo)


def _heads(t):
    return [t[:, h * HG_D:(h + 1) * HG_D] for h in range(HG_HEADS)]


def _level_reference(b, n, row):
    if n == 2:
        return jnp.where(jnp.bitwise_and(row, 1) == 0, b, pltpu.roll(b, 1, 0))
    if n == 4:
        m4 = jnp.bitwise_and(row, 3)
        return jnp.where(m4 == 0, pltpu.roll(b, CHUNK - 1, 0),
                         jnp.where(m4 == 1, b, jnp.where(m4 == 2, pltpu.roll(b, 1, 0), pltpu.roll(b, 2, 0))))
    bounds = [m * n + n // 2 - 1 for m in range(CHUNK // n)]
    pieces = [jnp.broadcast_to(b[t:t + 1], (n, b.shape[1])) for t in bounds]
    return pieces[0] if len(pieces) == 1 else jnp.concatenate(pieces, axis=0)


def _scores(q, k, e_q, e_k):
    qs, ks = _heads(q * e_q.astype(BF16)), _heads(k * e_k.astype(BF16))
    return [lax.dot_general(a, c, (((1,), (1,)), ((), ())), preferred_element_type=F32) for a, c in zip(qs, ks)]


def _intra_chunk_scores(q, k, b, lvl, fast):
    if fast:
        b_mid, b_end = b[HALF - 1:HALF], b[CHUNK - 1:CHUNK]
        r = jnp.concatenate([jnp.broadcast_to(0.5 * b_mid, (HALF, HG_W)),
                             jnp.broadcast_to(0.5 * (b_mid + b_end), (HALF, HG_W))], axis=0)
        e = jnp.exp(b - r)
        e6 = jnp.exp(-jnp.abs(b - b_mid))
        return [jnp.where(lvl == 6, across, jnp.where(lvl >= -1, within, 0.0))
                for within, across in zip(_scores(q, k, e, 1.0 / e), _scores(q, k, e6, e6))]
    row = lax.broadcasted_iota(jnp.int32, (CHUNK, HG_W), 0)
    atts = [jnp.where(lvl == -1, jnp.sum(qk, axis=-1, keepdims=True), 0.0) for qk in _heads((q * k).astype(F32))]
    for level in range(7):
        e = jnp.exp(-jnp.abs(b - _level_reference(b, 2 << level, row)))
        atts = [jnp.where(lvl == level, p, att) for p, att in zip(_scores(q, k, e, e), atts)]
    return atts


def _hgrn_chunks(fast, hq_ref, hk_ref, lf_ref, hv_ref, hg_ref, normw_ref, o_ref, st_ref):
    row = lax.broadcasted_iota(jnp.int32, (CHUNK, CHUNK), 0)
    col = lax.broadcasted_iota(jnp.int32, (CHUNK, CHUNK), 1)
    tri = jnp.where(row >= col, 1.0, 0.0).astype(BF16)
    lvl = jnp.where(row >= col, 31 - lax.clz(jnp.bitwise_xor(row, col)), -2)
    subs = [slice(c * CHUNK, (c + 1) * CHUNK) for c in range(hq_ref.shape[0] // CHUNK)]
    q, k, v = ([ref[rs] for rs in subs] for ref in (hq_ref, hk_ref, hv_ref))
    b = [_cumsum_rows(lf_ref[rs], tri) for rs in subs]
    b_end = [t[CHUNK - 1:CHUNK] for t in b]
    atts = [_intra_chunk_scores(qc, kc, bc, lvl, fast) for qc, kc, bc in zip(q, k, b)]
    q_in = [_heads(qc * jnp.exp(bc).astype(BF16)) for qc, bc in zip(q, b)]
    k_out = [_heads(kc.astype(F32) * jnp.exp(be - bc)) for kc, bc, be in zip(k, b, b_end)]
    k_t = [[jnp.transpose(t).astype(BF16) for t in per_head] for per_head in k_out]
    decay = [[jnp.transpose(jnp.broadcast_to(t, (HG_D, HG_D))) for t in _heads(jnp.exp(be))] for be in b_end]
    vs = [_heads(vc) for vc in v]
    local = [[jnp.dot(a.astype(BF16), vh, preferred_element_type=F32) for a, vh in zip(ac, vc)]
             for ac, vc in zip(atts, vs)]
    grow = [[jnp.dot(kt, vh, preferred_element_type=F32) for kt, vh in zip(kc, vc)] for kc, vc in zip(k_t, vs)]
    state = [st_ref[0, h] for h in range(HG_HEADS)]
    for c, rs in enumerate(subs):
        gs = _heads(hg_ref[rs])
        outs = [_hgrn_out(jnp.dot(q_in[c][h], state[h].astype(BF16), preferred_element_type=F32) + local[c][h],
                          gs[h], normw_ref[...]) for h in range(HG_HEADS)]
        o_ref[rs] = jnp.concatenate(outs, axis=1).astype(o_ref.dtype)
        state = [decay[c][h] * state[h] + grow[c][h] for h in range(HG_HEADS)]
    for h in range(HG_HEADS):
        st_ref[0, h] = state[h]


def _hgrn_prompt_kernel(ok_ref, *refs):
    st_ref = refs[-1]

    @pl.when(pl.program_id(1) == 0)
    def _():
        st_ref[...] = jnp.zeros_like(st_ref)

    first = (pl.program_id(0) * pl.num_programs(1) + pl.program_id(1)) * HGRN_CHUNKS
    ok = ok_ref[first] != 0
    for c in range(1, HGRN_CHUNKS):
        ok = jnp.logical_and(ok, ok_ref[first + c] != 0)

    @pl.when(ok)
    def _():
        _hgrn_chunks(True, *refs)

    @pl.when(jnp.logical_not(ok))
    def _():
        _hgrn_chunks(False, *refs)


def _hgrn_prompt(chunk_ok, hq, hk, lf, hv, hg, normw, batch, seq):
    nc = seq // (HGRN_CHUNKS * CHUNK)
    assert seq % (HGRN_CHUNKS * CHUNK) == 0
    rows = pl.BlockSpec((HGRN_CHUNKS * CHUNK, HG_W), lambda b, i, ok: (b * nc + i, 0))
    return pl.pallas_call(
        _hgrn_prompt_kernel,
        grid_spec=pltpu.PrefetchScalarGridSpec(
            num_scalar_prefetch=1,
            grid=(batch, nc),
            in_specs=[rows, rows, rows, rows, rows, pl.BlockSpec((1, HG_D), lambda b, i, ok: (0, 0))],
            out_specs=[rows, pl.BlockSpec((1, HG_HEADS, HG_D, HG_D), lambda b, i, ok: (b, 0, 0, 0))]),
        out_shape=[jax.ShapeDtypeStruct((batch * seq, HG_W), BF16),
                   jax.ShapeDtypeStruct((batch, HG_HEADS, HG_D, HG_D), F32)],
        compiler_params=_cparams("parallel", "arbitrary"),
        name="hgrn_prompt",
    )(chunk_ok, hq, hk, lf, hv, hg, normw)


def _hgrn_sample_kernel(hq_ref, hk_ref, lf_ref, hv_ref, hg_ref, normw_ref, st_ref, *rest):
    o_ref, ns_ref = rest[-2:]
    q, k, v, logf = hq_ref[...].astype(F32), hk_ref[...].astype(F32), hv_ref[...].astype(F32), lf_ref[...]
    row = lax.broadcasted_iota(jnp.int32, (ROWS_PER_STEP, HG_W), 0)
    tok = jnp.bitwise_and(row, DEC_SEQ - 1)
    seq_of_row = jnp.right_shift(row[:, :HG_D], 2)
    down = lambda t, d: pltpu.roll(t, d, 0)
    up = lambda t, d: pltpu.roll(t, ROWS_PER_STEP - d, 0)
    b = logf
    for d in range(1, DEC_SEQ):
        b = b + jnp.where(tok >= d, down(logf, d), 0.0)
    b_end = jnp.where(tok == 3, b, jnp.where(tok == 2, up(b, 1), jnp.where(tok == 1, up(b, 2), up(b, 3))))
    qe = (q * jnp.exp(b)).astype(BF16)
    kd = k * jnp.exp(b_end - b)
    e_end = jnp.exp(b_end)
    intra = [jnp.zeros((ROWS_PER_STEP, HG_D), F32) for _ in range(HG_HEADS)]
    for d in range(DEC_SEQ):
        kd_, bd_, vd_ = (k, b, v) if d == 0 else (down(k, d), down(b, d), down(v, d))
        w = jnp.where(tok >= d, q * kd_ * jnp.exp(jnp.minimum(b - bd_, 0.0)), 0.0)
        for h in range(HG_HEADS):
            sl = slice(h * HG_D, (h + 1) * HG_D)
            intra[h] = intra[h] + jnp.sum(w[:, sl], axis=-1, keepdims=True) * vd_[:, sl]
    pad = jnp.zeros((HG_D - 2 * ROWS_PER_STEP, HG_D), F32)
    colh = lax.broadcasted_iota(jnp.int32, (HG_D, HG_D), 1)
    normw = normw_ref[...]
    for h in range(HG_HEADS):
        sl = slice(h * HG_D, (h + 1) * HG_D)
        vb = jnp.concatenate([v[:, sl], jnp.zeros((HG_D - ROWS_PER_STEP, HG_D), F32)], axis=0).astype(BF16)
        tt = jnp.transpose(jnp.concatenate([kd[:, sl], e_end[:, sl], pad], axis=0))
        inter = jnp.zeros((ROWS_PER_STEP, HG_D), F32)
        for j in range(SEQ_PER_STEP):
            s0 = st_ref[j, h]
            oi = jnp.dot(qe[:, sl], s0.astype(BF16), preferred_element_type=F32)
            inter = jnp.where(seq_of_row == j, oi, inter)
            kd_t = jnp.where(jnp.right_shift(colh, 2) == j, tt, 0.0).astype(BF16)
            e_col = ROWS_PER_STEP + DEC_SEQ * j
            decay = jnp.broadcast_to(tt[:, e_col:e_col + 1], (HG_D, HG_D))
            ns_ref[j, h] = decay * s0 + jnp.dot(kd_t, vb, preferred_element_type=F32)
        o_ref[:, sl] = _hgrn_out(inter + intra[h], hg_ref[:, sl], normw).astype(o_ref.dtype)


def _hgrn_sample(hq, hk, lf, hv, hg, normw, state, layer, prev_state):
    depth, db = state.shape[:2]
    rows = pl.BlockSpec((ROWS_PER_STEP, HG_W), lambda s: (s, 0))
    st_spec = pl.BlockSpec((None, SEQ_PER_STEP, HG_HEADS, HG_D, HG_D), lambda s: (layer, s, 0, 0, 0))
    n_in = 7

    def call(extra, aliases):
        return pl.pallas_call(
            _hgrn_sample_kernel,
            grid=(db // SEQ_PER_STEP,),
            in_specs=[rows, rows, rows, rows, rows, pl.BlockSpec((1, HG_D), lambda s: (0, 0)), st_spec]
            + [pl.BlockSpec(memory_space=pl.ANY)] * len(extra),
            out_specs=[rows, st_spec],
            out_shape=[jax.ShapeDtypeStruct((db * DEC_SEQ, HG_W), BF16), jax.ShapeDtypeStruct(state.shape, F32)],
            input_output_aliases=aliases,
            compiler_params=_cparams("parallel"),
            name="hgrn_sample",
        )(hq, hk, lf, hv, hg, normw, state, *extra)

    return _stacked_out(call, layer, {1: prev_state}, n_in)


def _layer_norm(y, g, b):
    mu = jnp.mean(y, axis=-1, keepdims=True)
    yc = y - mu
    var = jnp.mean(yc * yc, axis=-1, keepdims=True)
    return yc * lax.rsqrt(var + LN_EPS) * g + b


def _post_kernel(alpha, a_ref, h_ref, g_ref, x_ref, wua_ref, wuh_ref, wo_ref, l1g_ref, l1b_ref,
                 wf1_ref, wf2_ref, l2g_ref, l2b_ref, o_ref):
    groups = [slice(r0, r0 + POST_GROUP) for r0 in range(0, x_ref.shape[0], POST_GROUP)]
    dot = lambda lhs, w_ref: jnp.dot(lhs, w_ref[...], preferred_element_type=F32)
    ua = [dot(a_ref[rs], wua_ref) for rs in groups]
    uh = [dot(h_ref[rs], wuh_ref) for rs in groups]
    merged = [(g_ref[rs, :D_MODEL] * a + g_ref[rs, D_MODEL:] * h).astype(BF16) for rs, a, h in zip(groups, ua, uh)]
    m = [dot(t, wo_ref) for t in merged]
    x1 = [_layer_norm(alpha * x_ref[rs] + t, l1g_ref[...], l1b_ref[...]) for rs, t in zip(groups, m)]
    hid = [jnp.square(jnp.maximum(dot(t.astype(BF16), wf1_ref), 0.0)).astype(BF16) for t in x1]
    ff = [dot(t, wf2_ref) for t in hid]
    for rs, t, f in zip(groups, x1, ff):
        o_ref[rs] = _layer_norm(alpha * t + f, l2g_ref[...], l2b_ref[...])


def _post(a, h, g, x, p, layer, alpha):
    t = x.shape[0]
    tm = min(POST_ROWS, t)
    assert t % tm == 0
    rows = lambda w: pl.BlockSpec((tm, w), lambda i: (i, 0))
    vec = lambda w: _resident((None, 1, w), lambda i: (layer, 0, 0))
    mat = lambda r, c: _resident((None, r, c), lambda i: (layer, 0, 0))
    return pl.pallas_call(
        functools.partial(_post_kernel, alpha),
        grid=(t // tm,),
        in_specs=[rows(ATT_W), rows(HG_W), rows(GATE_W), rows(D_MODEL),
                  mat(ATT_W, D_MODEL), mat(HG_W, D_MODEL), mat(D_MODEL, D_MODEL),
                  vec(D_MODEL), vec(D_MODEL), mat(D_MODEL, D_FF), mat(D_FF, D_MODEL), vec(D_MODEL), vec(D_MODEL)],
        out_specs=rows(D_MODEL),
        out_shape=jax.ShapeDtypeStruct((t, D_MODEL), F32),
        compiler_params=_cparams("parallel"),
        name="post",
    )(a, h, g, x, p["w_up_attn"], p["w_up_hgrn"], p["w_out"], p["ln1_g"], p["ln1_b"],
      p["w_ff1"], p["w_ff2"], p["ln2_g"], p["ln2_b"])


def kernel(x_prompt, x_sample, cache_k, cache_v, state_hgrn, w_in, b_gate, attn_sink, hgrn_lb_logits, hgrn_norm_w,
           w_up_attn, w_up_hgrn, w_out, ln1_g, ln1_b, w_ff1, w_ff2, ln2_g, ln2_b):
    depth = w_in.shape[0]
    batch, seq, _ = x_prompt.shape
    db, dec_seq, _ = x_sample.shape
    assert dec_seq == DEC_SEQ and seq % CHUNK == 0 and db % SEQ_PER_STEP == 0
    assert cache_k.shape[2:] == (WINDOW, KV_HEADS, HEAD_DIM)
    alpha = (2 * depth) ** 0.25

    vec = lambda t: t.reshape(depth, 1, t.shape[-1])
    b_gate3 = vec(b_gate)
    w_in_b = w_in.astype(BF16)
    p = dict(w_up_attn=w_up_attn.astype(BF16), w_up_hgrn=w_up_hgrn.astype(BF16), w_out=w_out.astype(BF16),
             ln1_g=vec(ln1_g), ln1_b=vec(ln1_b), w_ff1=w_ff1.astype(BF16), w_ff2=w_ff2.astype(BF16),
             ln2_g=vec(ln2_g), ln2_b=vec(ln2_b))
    to_t = lambda c: jnp.transpose(c, (0, 1, 3, 4, 2)).reshape(depth, db, KV_W, WINDOW)
    from_t = lambda c: jnp.transpose(c.reshape(depth, db, KV_HEADS, HEAD_DIM, WINDOW), (0, 1, 4, 2, 3))
    ck_t, cv_t = to_t(cache_k), to_t(cache_v)

    xp = x_prompt.reshape(batch * seq, D_MODEL)
    xs = x_sample.reshape(db * DEC_SEQ, D_MODEL)
    pk, pv, ps = [], [], []
    sk = sv = ss = None
    for l in range(depth):
        sink = attn_sink[l].reshape(1, N_HEADS)
        normw = hgrn_norm_w[l].reshape(1, HG_D)
        q, kv, ks, vs, hq, hk, lf, hv, hg, g, chunk_ok = _in_proj(xp, w_in_b, b_gate3, hgrn_lb_logits, l)
        a = _attn_prompt(q, ks, vs, sink, batch, seq)
        h, st = _hgrn_prompt(chunk_ok[:, 0, 0], hq, hk, lf, hv, hg, normw, batch, seq)
        xp = _post(a, h, g, xp, p, l, alpha)
        win = kv.reshape(batch, seq, 2 * KV_W)[:, seq - WINDOW:].reshape(batch, WINDOW, 2, KV_HEADS, HEAD_DIM)
        pk.append(win[:, :, 0])
        pv.append(win[:, :, 1])
        ps.append(st)
        q, kvn, _, _, hq, hk, lf, hv, hg, g, _ = _in_proj(xs, w_in_b, b_gate3, hgrn_lb_logits, l)
        a, sk, sv = _attn_sample(q, kvn, sink, ck_t, cv_t, l, sk, sv)
        h, ss = _hgrn_sample(hq, hk, lf, hv, hg, normw, state_hgrn, l, ss)
        xs = _post(a, h, g, xs, p, l, alpha)
    return (xp.reshape(batch, seq, D_MODEL), xs.reshape(db, DEC_SEQ, D_MODEL), jnp.stack(pk), jnp.stack(pv),
            jnp.stack(ps), from_t(sk), from_t(sv), ss)
```

```python
import functools

import jax
import jax.numpy as jnp
from jax import lax
from jax.experimental import pallas as pl
from jax.experimental.pallas import tpu as pltpu

F32 = jnp.float32
BF16 = jnp.bfloat16

D_MODEL = 1024
WINDOW = 128
N_HEADS = 8
KV_HEADS = 2
HEAD_DIM = 64
ATT_W = N_HEADS * HEAD_DIM
KV_W = KV_HEADS * HEAD_DIM
HG_HEADS = 4
HG_D = 128
HG_W = HG_HEADS * HG_D
GATE_W = 2 * D_MODEL
D_FF = 4 * D_MODEL
DEC_SEQ = 4
LN_EPS = 1e-5
RMS_EPS = 1e-6
NEG_BIG = -1e30
LB_FLOOR = 1e-30
LOG2E = 1.4426950408889634
QK_SCALE = HEAD_DIM ** -0.5 * LOG2E

AQ0, AK0, AV0, HQ0, HF0, HI0, HG0, GT0, IN_COLS = 0, 512, 640, 768, 1280, 1792, 2304, 2816, 4864

ATT_BLOCKS = 4
CHUNK = 128
HGRN_CHUNKS = 4
HALF = CHUNK // 2
FAST_DECAY_LIMIT = 120.0
FAST_Q_LIMIT = 1e9
SEQ_PER_STEP = 4
ROWS_PER_STEP = SEQ_PER_STEP * DEC_SEQ
IN_PROJ_ROWS = 512
POST_ROWS = 512
POST_GROUP = 256
V7X_VMEM_LIMIT = 56 * 1024 * 1024


def _cparams(*sem):
    return pltpu.CompilerParams(dimension_semantics=sem, vmem_limit_bytes=V7X_VMEM_LIMIT)


def _resident(shape, index_map):
    return pl.BlockSpec(shape, index_map, pipeline_mode=pl.Buffered(1))


def _stacked_out(call, layer, stacked, n_in):
    extra, aliases = [], {}
    if layer > 0:
        for out_idx, prev in stacked.items():
            aliases[n_in + len(extra)] = out_idx
            extra.append(prev)
    return call(extra, aliases)


def _lower_bound(logits_ref, layer):
    lg = logits_ref[...]
    e = jnp.exp(lg - jnp.max(lg, axis=0, keepdims=True))
    p = e / jnp.sum(e, axis=0, keepdims=True)
    lb = jnp.zeros((1, HG_W), F32)
    for i in range(1, layer + 1):
        lb = lb + p[i:i + 1]
    return lb


def _swish(z):
    return z * jax.nn.sigmoid(z)


def _forget_and_key(zf, lb):
    e = jnp.exp(-jnp.abs(zf))
    r = 1.0 / (1.0 + e)
    pos = zf >= 0
    sig_p = jnp.where(pos, r, e * r)
    sig_n = jnp.where(pos, e * r, r)
    logf = jnp.log(jnp.maximum(lb, LB_FLOOR) + (1.0 - lb) * sig_p)
    return logf, (1.0 - lb) * sig_n


def _cast_rows(src_refs, dst_refs):
    for src, dst in zip(src_refs, dst_refs):
        dst[...] = src[...].astype(BF16)


def _in_proj_kernel(layer, n_cast, x_ref, w_ref, bg_ref, logits_ref, *refs):
    cast_src, refs = refs[:n_cast], refs[n_cast:]
    (q_ref, kv_ref, ks_ref, vs_ref, hq_ref, hk_ref, lf_ref, hv_ref, hg_ref, g_ref, ok_ref), refs = refs[:11], refs[11:]
    cast_dst, xb_ref = refs[:n_cast], refs[n_cast]
    _cast_rows(cast_src, cast_dst)
    xb = x_ref[...].astype(BF16)
    lb = _lower_bound(logits_ref, layer)
    chunks = x_ref.shape[0] // CHUNK
    q_ok = []
    decay_ok = []

    def put_q(z):
        q_ref[...] = (z * QK_SCALE).astype(BF16)

    def put_kv(z):
        kv_ref[...] = z
        ks_ref[...] = _split_heads(z[:, :KV_W])
        vs_ref[...] = _split_heads(z[:, KV_W:])

    def put_hq(z):
        hq = _swish(z)
        hq_ref[...] = hq.astype(BF16)
        for c in range(chunks):
            q_ok.append(jnp.max(jnp.abs(hq[c * CHUNK:(c + 1) * CHUNK]), keepdims=True) <= FAST_Q_LIMIT)

    def put_hf(z):
        logf, key = _forget_and_key(z, lb)
        lf_ref[...] = logf
        hk_ref[...] = key.astype(BF16)
        for lo in range(0, chunks * CHUNK, HALF):
            total = jnp.sum(logf[lo:lo + HALF], axis=0, keepdims=True)
            decay_ok.append(jnp.min(total, keepdims=True) >= -FAST_DECAY_LIMIT)

    def put_hv(z):
        hv_ref[...] = z.astype(BF16)

    def put_hg(z):
        hg_ref[...] = _swish(z)

    def put_gates(lo):
        def put(z):
            g_ref[:, lo:lo + 512] = jax.nn.sigmoid(z + bg_ref[:, lo:lo + 512])
        return put

    gates = [(GT0 + lo, GT0 + lo + 512, put_gates(lo)) for lo in range(0, GATE_W, 512)]
    stages = [(HF0, HI0, put_hf), (AQ0, AK0, put_q), (HQ0, HF0, put_hq), gates[0], (HG0, GT0, put_hg), gates[1],
              gates[2], gates[3], (HI0, HG0, put_hv), (AK0, HQ0, put_kv)]
    pending = None
    xb_ref[...] = xb
    for lo, hi, put in stages:
        z = jnp.dot(xb_ref[...], w_ref[:, lo:hi], preferred_element_type=F32)
        if pending is not None:
            pending[0](pending[1])
        pending = (put, z)
    pending[0](pending[1])
    for c in range(chunks):
        ok = jnp.logical_and(q_ok[c], jnp.logical_and(decay_ok[2 * c], decay_ok[2 * c + 1]))
        ok_ref[c] = jnp.broadcast_to(jnp.where(ok, 1, 0), (1, 128)).astype(jnp.int32)


def _split_heads(t):
    low = lax.broadcasted_iota(jnp.int32, t.shape, 1) < HEAD_DIM
    h0_lo = jnp.where(low, t, 0.0)
    h1_hi = jnp.where(low, 0.0, t)
    blocks = [h0_lo, pltpu.roll(h0_lo, HEAD_DIM, 1), pltpu.roll(h1_hi, HEAD_DIM, 1), h1_hi]
    return jnp.concatenate(blocks, axis=1).astype(BF16)


def _cast_specs(weights, layer, steps):
    src, dst, shapes = [], [], []
    for w in weights:
        _, r, c = w.shape
        slab = r // steps
        assert r % steps == 0 and (slab % 16 == 0 or steps == 1)
        src.append(pl.BlockSpec((None, slab, c), lambda i, layer=layer: (layer, i, 0)))
        dst.append(pl.BlockSpec((slab, c), lambda i: (i, 0)))
        shapes.append(jax.ShapeDtypeStruct((r, c), BF16))
    return src, dst, shapes


def _in_proj(x, w_b, b_gate, logits, layer, cast=()):
    t = x.shape[0]
    tm = min(IN_PROJ_ROWS, t)
    assert t % tm == 0 and tm % CHUNK == 0
    rows = lambda w: pl.BlockSpec((tm, w), lambda i: (i, 0))
    out = lambda w, dt: jax.ShapeDtypeStruct((t, w), dt)
    cast_src, cast_dst, cast_shapes = _cast_specs(cast, layer, t // tm)
    return pl.pallas_call(
        functools.partial(_in_proj_kernel, layer, len(cast)),
        grid=(t // tm,),
        in_specs=[rows(D_MODEL),
                  _resident((D_MODEL, IN_COLS), lambda i: (0, 0)),
                  _resident((None, 1, GATE_W), lambda i: (layer, 0, 0)),
                  _resident(logits.shape, lambda i: (0, 0))] + cast_src,
        out_specs=[rows(ATT_W), rows(2 * KV_W), rows(4 * KV_W), rows(4 * KV_W),
                   rows(HG_W), rows(HG_W), rows(HG_W), rows(HG_W), rows(HG_W),
                   rows(GATE_W), pl.BlockSpec((tm // CHUNK, 1, 128), lambda i: (i, 0, 0))] + cast_dst,
        out_shape=[out(ATT_W, BF16), out(2 * KV_W, F32), out(4 * KV_W, BF16), out(4 * KV_W, BF16),
                   out(HG_W, BF16), out(HG_W, BF16), out(HG_W, F32), out(HG_W, BF16), out(HG_W, F32),
                   out(GATE_W, F32),
                   jax.ShapeDtypeStruct((t // CHUNK, 1, 128), jnp.int32)] + cast_shapes,
        scratch_shapes=[pltpu.VMEM((tm, D_MODEL), BF16)],
        compiler_params=_cparams("parallel"),
        name="in_proj",
    )(x, w_b, b_gate, logits, *cast)


def _attn_prompt_kernel(sink_ref, bias_ref, q_ref, kp_ref, kc_ref, vp_ref, vc_ref, o_ref):
    r = WINDOW
    first_bias = bias_ref[jnp.minimum(pl.program_id(1), 1)]
    top = lax.broadcasted_iota(jnp.int32, (2 * r, 1), 0) < r
    combos = [(n, g, half) for n in range(ATT_BLOCKS) for g in range(KV_HEADS) for half in range(2)]
    lanes = lambda g, half: slice((2 * g + half) * KV_W, (2 * g + half + 1) * KV_W)

    def kv_rows(prev_ref, cur_ref, n, g, half):
        before = prev_ref[:, lanes(g, half)] if n == 0 else cur_ref[(n - 1) * r:n * r, lanes(g, half)]
        return jnp.concatenate([before, cur_ref[n * r:(n + 1) * r, lanes(g, half)]], axis=0)

    def q_rows(n, g):
        return jnp.concatenate([q_ref[n * r:(n + 1) * r, 2 * g * 128:(2 * g + 1) * 128],
                                q_ref[n * r:(n + 1) * r, (2 * g + 1) * 128:(2 * g + 2) * 128]], axis=0)

    bias = [jnp.concatenate([b, b], axis=0) for b in [first_bias] + [bias_ref[1]] * (ATT_BLOCKS - 1)]
    keys = [kv_rows(kp_ref, kc_ref, *c) for c in combos]
    vals = [kv_rows(vp_ref, vc_ref, *c) for c in combos]
    sink = [jnp.where(top, sink_ref[0, 4 * g + half], sink_ref[0, 4 * g + 2 + half]) * LOG2E for _, g, half in combos]
    s = [lax.dot_general(q_rows(n, g), k, (((1,), (1,)), ((), ())), preferred_element_type=F32) + bias[n]
         for (n, g, _), k in zip(combos, keys)]
    m = [jnp.maximum(jnp.max(t, axis=-1, keepdims=True), sk) for t, sk in zip(s, sink)]
    p = [jnp.exp2(t - mx) for t, mx in zip(s, m)]
    den = [jnp.sum(t, axis=-1, keepdims=True) + jnp.exp2(sk - mx) for t, sk, mx in zip(p, sink, m)]
    o = [jnp.dot(t.astype(BF16), v, preferred_element_type=F32) / d for t, v, d in zip(p, vals, den)]
    for n in range(ATT_BLOCKS):
        cols = []
        for g in range(KV_HEADS):
            acc = o[4 * n + 2 * g] + o[4 * n + 2 * g + 1]
            cols += [acc[:r], acc[r:]]
        o_ref[n * r:(n + 1) * r, :] = jnp.concatenate(cols, axis=1).astype(o_ref.dtype)


def _band_bias():
    qi = lax.broadcasted_iota(jnp.int32, (WINDOW, 2 * WINDOW), 0)
    kj = lax.broadcasted_iota(jnp.int32, (WINDOW, 2 * WINDOW), 1)
    band = jnp.logical_and(kj >= qi, kj <= qi + WINDOW)
    first = jnp.logical_and(band, kj >= WINDOW)
    return jnp.where(jnp.stack([first, band]), 0.0, NEG_BIG).astype(F32)


def _attn_prompt(q, ks, vs, sink, batch, seq):
    steps = seq // (ATT_BLOCKS * WINDOW)
    assert seq % (ATT_BLOCKS * WINDOW) == 0
    cur = lambda b, i: (b * steps + i, 0)
    prev = lambda b, i: ((b * steps + i) * ATT_BLOCKS - jnp.minimum(i, 1), 0)
    prev_spec = pl.BlockSpec((WINDOW, 4 * KV_W), prev)
    cur_spec = pl.BlockSpec((ATT_BLOCKS * WINDOW, 4 * KV_W), cur)
    return pl.pallas_call(
        _attn_prompt_kernel,
        grid=(batch, steps),
        in_specs=[pl.BlockSpec(memory_space=pltpu.SMEM),
                  _resident((2, WINDOW, 2 * WINDOW), lambda b, i: (0, 0, 0)),
                  pl.BlockSpec((ATT_BLOCKS * WINDOW, ATT_W), cur),
                  prev_spec, cur_spec, prev_spec, cur_spec],
        out_specs=pl.BlockSpec((ATT_BLOCKS * WINDOW, ATT_W), cur),
        out_shape=jax.ShapeDtypeStruct((batch * seq, ATT_W), BF16),
        compiler_params=_cparams("parallel", "parallel"),
        name="attn_prompt",
    )(sink, _band_bias(), q, ks, ks, vs, vs)


def _attn_sample_kernel(sink_ref, q_ref, kvn_ref, ck_ref, cv_ref, *rest):
    o_ref, nk_ref, nv_ref = rest[-3:]
    lane = lax.broadcasted_iota(jnp.int32, (ROWS_PER_STEP, KV_W), 1)
    q16 = q_ref[...].astype(F32)
    blocks = []
    for h in range(N_HEADS):
        c, half, g = h // 2, h % 2, h // (N_HEADS // KV_HEADS)
        own = lane >= HEAD_DIM if half else lane < HEAD_DIM
        qh = jnp.where(own, q16[:, c * 128:(c + 1) * 128], 0.0)
        blocks.append(qh if half == g else pltpu.roll(qh, HEAD_DIM, 1))
    wt = jnp.concatenate(blocks, axis=0).astype(BF16)

    row = lax.broadcasted_iota(jnp.int32, (WINDOW, WINDOW), 0)
    col = lax.broadcasted_iota(jnp.int32, (WINDOW, WINDOW), 1)
    tok = jnp.bitwise_and(row, DEC_SEQ - 1)
    row_seq = jnp.bitwise_and(jnp.right_shift(row, 2), SEQ_PER_STEP - 1)
    pad = jnp.zeros((WINDOW - ROWS_PER_STEP, KV_W), F32)
    kn = jnp.concatenate([kvn_ref[:, :KV_W], pad], axis=0)
    vn = jnp.concatenate([kvn_ref[:, KV_W:], pad], axis=0)
    kn_t = jnp.transpose(kn)
    vn_t = jnp.transpose(vn)

    s_c = jnp.zeros((WINDOW, WINDOW), F32)
    for j in range(SEQ_PER_STEP):
        sj = jnp.dot(wt, ck_ref[j].astype(BF16), preferred_element_type=F32)
        s_c = jnp.where(row_seq == j, sj, s_c)
    s_n = jnp.dot(wt, kn_t.astype(BF16), preferred_element_type=F32)
    s_c = jnp.where(col >= tok, s_c, NEG_BIG)
    new_ok = jnp.logical_and(jnp.right_shift(col, 2) == row_seq, jnp.bitwise_and(col, DEC_SEQ - 1) <= tok)
    s_n = jnp.where(new_ok, s_n, NEG_BIG)
    head = jnp.right_shift(lax.broadcasted_iota(jnp.int32, (WINDOW, 1), 0), 4)
    sink = jnp.zeros((WINDOW, 1), F32)
    for h in range(N_HEADS):
        sink = jnp.where(head == h, sink_ref[0, h] * LOG2E, sink)
    m = jnp.maximum(jnp.maximum(jnp.max(s_c, axis=-1, keepdims=True), jnp.max(s_n, axis=-1, keepdims=True)), sink)
    p_c = jnp.exp2(s_c - m)
    p_n = jnp.exp2(s_n - m)
    den = jnp.sum(p_c, axis=-1, keepdims=True) + jnp.sum(p_n, axis=-1, keepdims=True) + jnp.exp2(sink - m)
    p_cb = p_c.astype(BF16)
    o = jnp.dot(p_n.astype(BF16), vn.astype(BF16), preferred_element_type=F32)
    for j in range(SEQ_PER_STEP):
        oj = lax.dot_general(p_cb, cv_ref[j].astype(BF16), (((1,), (1,)), ((), ())), preferred_element_type=F32)
        o = o + jnp.where(row_seq == j, oj, 0.0)
    o = o / den
    outs = []
    for c in range(4):
        lo, hi = o[32 * c:32 * c + 16], o[32 * c + 16:32 * c + 32]
        if c < 2:
            outs.append(jnp.where(lane < HEAD_DIM, lo, pltpu.roll(hi, HEAD_DIM, 1)))
        else:
            outs.append(jnp.where(lane < HEAD_DIM, pltpu.roll(lo, HEAD_DIM, 1), hi))
    o_ref[...] = jnp.concatenate(outs, axis=1).astype(o_ref.dtype)
    for j in range(SEQ_PER_STEP):
        shift = (WINDOW - DEC_SEQ - DEC_SEQ * j) % WINDOW
        for new_t, cache_ref, dst in ((kn_t, ck_ref, nk_ref), (vn_t, cv_ref, nv_ref)):
            dst[j] = jnp.where(col >= WINDOW - DEC_SEQ, pltpu.roll(new_t, shift, 1),
                               pltpu.roll(cache_ref[j], WINDOW - DEC_SEQ, 1))


def _attn_sample(q, kvn, sink, cache_kt, cache_vt, layer, prev_k, prev_v):
    depth, db = cache_kt.shape[:2]
    cache_spec = pl.BlockSpec((None, SEQ_PER_STEP, KV_W, WINDOW), lambda s: (layer, s, 0, 0))
    rows = lambda w: pl.BlockSpec((ROWS_PER_STEP, w), lambda s: (s, 0))
    n_in = 5

    def call(extra, aliases):
        return pl.pallas_call(
            _attn_sample_kernel,
            grid=(db // SEQ_PER_STEP,),
            in_specs=[pl.BlockSpec(memory_space=pltpu.SMEM), rows(ATT_W), rows(2 * KV_W), cache_spec, cache_spec]
            + [pl.BlockSpec(memory_space=pl.ANY)] * len(extra),
            out_specs=[rows(ATT_W), cache_spec, cache_spec],
            out_shape=[jax.ShapeDtypeStruct((db * DEC_SEQ, ATT_W), BF16),
                       jax.ShapeDtypeStruct((depth, db, KV_W, WINDOW), F32),
                       jax.ShapeDtypeStruct((depth, db, KV_W, WINDOW), F32)],
            input_output_aliases=aliases,
            compiler_params=_cparams("parallel"),
            name="attn_sample",
        )(sink, q, kvn, cache_kt, cache_vt, *extra)

    return _stacked_out(call, layer, {1: prev_k, 2: prev_v}, n_in)


def _hgrn_out(o, g_act, normw):
    ms = jnp.mean(o * o, axis=-1, keepdims=True)
    return o * lax.rsqrt(ms + RMS_EPS) * normw * g_act


def _cumsum_rows(x, tri):
    hi = x.astype(BF16)
    r1 = x - hi.astype(F32)
    mid = r1.astype(BF16)
    lo = (r1 - mid.astype(F32)).astype(BF16)
    dot = lambda t: jnp.dot(tri, t, preferred_element_type=F32)
    return dot(hi) + dot(mid) + dot(lo)


def _heads(t):
    return [t[:, h * HG_D:(h + 1) * HG_D] for h in range(HG_HEADS)]


def _level_reference(b, n, row):
    if n == 2:
        return jnp.where(jnp.bitwise_and(row, 1) == 0, b, pltpu.roll(b, 1, 0))
    if n == 4:
        m4 = jnp.bitwise_and(row, 3)
        return jnp.where(m4 == 0, pltpu.roll(b, CHUNK - 1, 0),
                         jnp.where(m4 == 1, b, jnp.where(m4 == 2, pltpu.roll(b, 1, 0), pltpu.roll(b, 2, 0))))
    bounds = [m * n + n // 2 - 1 for m in range(CHUNK // n)]
    pieces = [jnp.broadcast_to(b[t:t + 1], (n, b.shape[1])) for t in bounds]
    return pieces[0] if len(pieces) == 1 else jnp.concatenate(pieces, axis=0)


def _scores(q, k, e_q, e_k):
    qs, ks = _heads(q * e_q.astype(BF16)), _heads(k * e_k.astype(BF16))
    return [lax.dot_general(a, c, (((1,), (1,)), ((), ())), preferred_element_type=F32) for a, c in zip(qs, ks)]


def _intra_chunk_scores(q, k, b, lvl, fast):
    if fast:
        b_mid, b_end = b[HALF - 1:HALF], b[CHUNK - 1:CHUNK]
        r = jnp.concatenate([jnp.broadcast_to(0.5 * b_mid, (HALF, HG_W)),
                             jnp.broadcast_to(0.5 * (b_mid + b_end), (HALF, HG_W))], axis=0)
        e = jnp.exp(b - r)
        e6 = jnp.exp(-jnp.abs(b - b_mid))
        return [jnp.where(lvl == 6, across, jnp.where(lvl >= -1, within, 0.0))
                for within, across in zip(_scores(q, k, e, 1.0 / e), _scores(q, k, e6, e6))]
    row = lax.broadcasted_iota(jnp.int32, (CHUNK, HG_W), 0)
    atts = [jnp.where(lvl == -1, jnp.sum(qk, axis=-1, keepdims=True), 0.0) for qk in _heads((q * k).astype(F32))]
    for level in range(7):
        e = jnp.exp(-jnp.abs(b - _level_reference(b, 2 << level, row)))
        atts = [jnp.where(lvl == level, p, att) for p, att in zip(_scores(q, k, e, e), atts)]
    return atts


def _hgrn_chunks(fast, hq_ref, hk_ref, lf_ref, hv_ref, hg_ref, normw_ref, o_ref, st_ref):
    row = lax.broadcasted_iota(jnp.int32, (CHUNK, CHUNK), 0)
    col = lax.broadcasted_iota(jnp.int32, (CHUNK, CHUNK), 1)
    tri = jnp.where(row >= col, 1.0, 0.0).astype(BF16)
    lvl = jnp.where(row >= col, 31 - lax.clz(jnp.bitwise_xor(row, col)), -2)
    subs = [slice(c * CHUNK, (c + 1) * CHUNK) for c in range(hq_ref.shape[0] // CHUNK)]
    q, k, v = ([ref[rs] for rs in subs] for ref in (hq_ref, hk_ref, hv_ref))
    b = [_cumsum_rows(lf_ref[rs], tri) for rs in subs]
    b_end = [t[CHUNK - 1:CHUNK] for t in b]
    atts = [_intra_chunk_scores(qc, kc, bc, lvl, fast) for qc, kc, bc in zip(q, k, b)]
    q_in = [_heads(qc * jnp.exp(bc).astype(BF16)) for qc, bc in zip(q, b)]
    k_out = [_heads(kc.astype(F32) * jnp.exp(be - bc)) for kc, bc, be in zip(k, b, b_end)]
    k_t = [[jnp.transpose(t).astype(BF16) for t in per_head] for per_head in k_out]
    decay = [[jnp.transpose(jnp.broadcast_to(t, (HG_D, HG_D))) for t in _heads(jnp.exp(be))] for be in b_end]
    vs = [_heads(vc) for vc in v]
    local = [[jnp.dot(a.astype(BF16), vh, preferred_element_type=F32) for a, vh in zip(ac, vc)]
             for ac, vc in zip(atts, vs)]
    grow = [[jnp.dot(kt, vh, preferred_element_type=F32) for kt, vh in zip(kc, vc)] for kc, vc in zip(k_t, vs)]
    state = [st_ref[0, h] for h in range(HG_HEADS)]
    for c, rs in enumerate(subs):
        gs = _heads(hg_ref[rs])
        outs = [_hgrn_out(jnp.dot(q_in[c][h], state[h].astype(BF16), preferred_element_type=F32) + local[c][h],
                          gs[h], normw_ref[...]) for h in range(HG_HEADS)]
        o_ref[rs] = jnp.concatenate(outs, axis=1).astype(o_ref.dtype)
        state = [decay[c][h] * state[h] + grow[c][h] for h in range(HG_HEADS)]
    for h in range(HG_HEADS):
        st_ref[0, h] = state[h]


def _hgrn_prompt_kernel(ok_ref, *refs):
    st_ref = refs[-1]

    @pl.when(pl.program_id(1) == 0)
    def _():
        st_ref[...] = jnp.zeros_like(st_ref)

    first = (pl.program_id(0) * pl.num_programs(1) + pl.program_id(1)) * HGRN_CHUNKS
    ok = ok_ref[first] != 0
    for c in range(1, HGRN_CHUNKS):
        ok = jnp.logical_and(ok, ok_ref[first + c] != 0)

    @pl.when(ok)
    def _():
        _hgrn_chunks(True, *refs)

    @pl.when(jnp.logical_not(ok))
    def _():
        _hgrn_chunks(False, *refs)


def _hgrn_prompt(chunk_ok, hq, hk, lf, hv, hg, normw, batch, seq):
    nc = seq // (HGRN_CHUNKS * CHUNK)
    assert seq % (HGRN_CHUNKS * CHUNK) == 0
    rows = pl.BlockSpec((HGRN_CHUNKS * CHUNK, HG_W), lambda b, i, ok: (b * nc + i, 0))
    return pl.pallas_call(
        _hgrn_prompt_kernel,
        grid_spec=pltpu.PrefetchScalarGridSpec(
            num_scalar_prefetch=1,
            grid=(batch, nc),
            in_specs=[rows, rows, rows, rows, rows, pl.BlockSpec((1, HG_D), lambda b, i, ok: (0, 0))],
            out_specs=[rows, pl.BlockSpec((1, HG_HEADS, HG_D, HG_D), lambda b, i, ok: (b, 0, 0, 0))]),
        out_shape=[jax.ShapeDtypeStruct((batch * seq, HG_W), BF16),
                   jax.ShapeDtypeStruct((batch, HG_HEADS, HG_D, HG_D), F32)],
        compiler_params=_cparams("parallel", "arbitrary"),
        name="hgrn_prompt",
    )(chunk_ok, hq, hk, lf, hv, hg, normw)


def _hgrn_sample_kernel(hq_ref, hk_ref, lf_ref, hv_ref, hg_ref, normw_ref, st_ref, *rest):
    o_ref, ns_ref = rest[-2:]
    q, k, v, logf = hq_ref[...].astype(F32), hk_ref[...].astype(F32), hv_ref[...].astype(F32), lf_ref[...]
    row = lax.broadcasted_iota(jnp.int32, (ROWS_PER_STEP, HG_W), 0)
    tok = jnp.bitwise_and(row, DEC_SEQ - 1)
    seq_of_row = jnp.right_shift(row[:, :HG_D], 2)
    down = lambda t, d: pltpu.roll(t, d, 0)
    up = lambda t, d: pltpu.roll(t, ROWS_PER_STEP - d, 0)
    b = logf
    for d in range(1, DEC_SEQ):
        b = b + jnp.where(tok >= d, down(logf, d), 0.0)
    b_end = jnp.where(tok == 3, b, jnp.where(tok == 2, up(b, 1), jnp.where(tok == 1, up(b, 2), up(b, 3))))
    qe = (q * jnp.exp(b)).astype(BF16)
    kd = k * jnp.exp(b_end - b)
    e_end = jnp.exp(b_end)
    intra = [jnp.zeros((ROWS_PER_STEP, HG_D), F32) for _ in range(HG_HEADS)]
    for d in range(DEC_SEQ):
        kd_, bd_, vd_ = (k, b, v) if d == 0 else (down(k, d), down(b, d), down(v, d))
        w = jnp.where(tok >= d, q * kd_ * jnp.exp(jnp.minimum(b - bd_, 0.0)), 0.0)
        for h in range(HG_HEADS):
            sl = slice(h * HG_D, (h + 1) * HG_D)
            intra[h] = intra[h] + jnp.sum(w[:, sl], axis=-1, keepdims=True) * vd_[:, sl]
    pad = jnp.zeros((HG_D - 2 * ROWS_PER_STEP, HG_D), F32)
    colh = lax.broadcasted_iota(jnp.int32, (HG_D, HG_D), 1)
    normw = normw_ref[...]
    for h in range(HG_HEADS):
        sl = slice(h * HG_D, (h + 1) * HG_D)
        vb = jnp.concatenate([v[:, sl], jnp.zeros((HG_D - ROWS_PER_STEP, HG_D), F32)], axis=0).astype(BF16)
        tt = jnp.transpose(jnp.concatenate([kd[:, sl], e_end[:, sl], pad], axis=0))
        inter = jnp.zeros((ROWS_PER_STEP, HG_D), F32)
        for j in range(SEQ_PER_STEP):
            s0 = st_ref[j, h]
            oi = jnp.dot(qe[:, sl], s0.astype(BF16), preferred_element_type=F32)
            inter = jnp.where(seq_of_row == j, oi, inter)
            kd_t = jnp.where(jnp.right_shift(colh, 2) == j, tt, 0.0).astype(BF16)
            e_col = ROWS_PER_STEP + DEC_SEQ * j
            decay = jnp.broadcast_to(tt[:, e_col:e_col + 1], (HG_D, HG_D))
            ns_ref[j, h] = decay * s0 + jnp.dot(kd_t, vb, preferred_element_type=F32)
        o_ref[:, sl] = _hgrn_out(inter + intra[h], hg_ref[:, sl], normw).astype(o_ref.dtype)


def _hgrn_sample(hq, hk, lf, hv, hg, normw, state, layer, prev_state):
    depth, db = state.shape[:2]
    rows = pl.BlockSpec((ROWS_PER_STEP, HG_W), lambda s: (s, 0))
    st_spec = pl.BlockSpec((None, SEQ_PER_STEP, HG_HEADS, HG_D, HG_D), lambda s: (layer, s, 0, 0, 0))
    n_in = 7

    def call(extra, aliases):
        return pl.pallas_call(
            _hgrn_sample_kernel,
            grid=(db // SEQ_PER_STEP,),
            in_specs=[rows, rows, rows, rows, rows, pl.BlockSpec((1, HG_D), lambda s: (0, 0)), st_spec]
            + [pl.BlockSpec(memory_space=pl.ANY)] * len(extra),
            out_specs=[rows, st_spec],
            out_shape=[jax.ShapeDtypeStruct((db * DEC_SEQ, HG_W), BF16), jax.ShapeDtypeStruct(state.shape, F32)],
            input_output_aliases=aliases,
            compiler_params=_cparams("parallel"),
            name="hgrn_sample",
        )(hq, hk, lf, hv, hg, normw, state, *extra)

    return _stacked_out(call, layer, {1: prev_state}, n_in)


def _layer_norm(y, g, b):
    mu = jnp.mean(y, axis=-1, keepdims=True)
    yc = y - mu
    var = jnp.mean(yc * yc, axis=-1, keepdims=True)
    return yc * lax.rsqrt(var + LN_EPS) * g + b


def _post_kernel(alpha, n_cast, a_ref, h_ref, g_ref, x_ref, wua_ref, wuh_ref, wo_ref, l1g_ref, l1b_ref,
                 wf1_ref, wf2_ref, l2g_ref, l2b_ref, *refs):
    o_ref = refs[n_cast]
    _cast_rows(refs[:n_cast], refs[n_cast + 1:])
    groups = [slice(r0, r0 + POST_GROUP) for r0 in range(0, x_ref.shape[0], POST_GROUP)]
    dot = lambda lhs, w_ref: jnp.dot(lhs, w_ref[...], preferred_element_type=F32)
    ua = [dot(a_ref[rs], wua_ref) for rs in groups]
    uh = [dot(h_ref[rs], wuh_ref) for rs in groups]
    merged = [(g_ref[rs, :D_MODEL] * a + g_ref[rs, D_MODEL:] * h).astype(BF16) for rs, a, h in zip(groups, ua, uh)]
    m = [dot(t, wo_ref) for t in merged]
    x1 = [_layer_norm(alpha * x_ref[rs] + t, l1g_ref[...], l1b_ref[...]) for rs, t in zip(groups, m)]
    hid = [jnp.square(jnp.maximum(dot(t.astype(BF16), wf1_ref), 0.0)).astype(BF16) for t in x1]
    ff = [dot(t, wf2_ref) for t in hid]
    for rs, t, f in zip(groups, x1, ff):
        o_ref[rs] = _layer_norm(alpha * t + f, l2g_ref[...], l2b_ref[...])


def _post(a, h, g, x, wb, p, layer, alpha, cast=(), cast_layer=0):
    t = x.shape[0]
    tm = min(POST_ROWS, t)
    assert t % tm == 0
    rows = lambda w: pl.BlockSpec((tm, w), lambda i: (i, 0))
    vec = lambda w: _resident((None, 1, w), lambda i: (layer, 0, 0))
    mat = lambda w: _resident(w.shape, lambda i: (0, 0))
    cast_src, cast_dst, cast_shapes = _cast_specs(cast, cast_layer, t // tm)
    outs = pl.pallas_call(
        functools.partial(_post_kernel, alpha, len(cast)),
        grid=(t // tm,),
        in_specs=[rows(ATT_W), rows(HG_W), rows(GATE_W), rows(D_MODEL), mat(wb[0]), mat(wb[1]), mat(wb[2]),
                  vec(D_MODEL), vec(D_MODEL), mat(wb[3]), mat(wb[4]), vec(D_MODEL), vec(D_MODEL)] + cast_src,
        out_specs=[rows(D_MODEL)] + cast_dst,
        out_shape=[jax.ShapeDtypeStruct((t, D_MODEL), F32)] + cast_shapes,
        compiler_params=_cparams("parallel"),
        name="post",
    )(a, h, g, x, wb[0], wb[1], wb[2], p["ln1_g"], p["ln1_b"], wb[3], wb[4], p["ln2_g"], p["ln2_b"], *cast)
    return outs[0], outs[1:]


def kernel(x_prompt, x_sample, cache_k, cache_v, state_hgrn, w_in, b_gate, attn_sink, hgrn_lb_logits, hgrn_norm_w,
           w_up_attn, w_up_hgrn, w_out, ln1_g, ln1_b, w_ff1, w_ff2, ln2_g, ln2_b):
    depth = w_in.shape[0]
    batch, seq, _ = x_prompt.shape
    db, dec_seq, _ = x_sample.shape
    assert dec_seq == DEC_SEQ and seq % CHUNK == 0 and db % SEQ_PER_STEP == 0
    assert cache_k.shape[2:] == (WINDOW, KV_HEADS, HEAD_DIM)
    alpha = (2 * depth) ** 0.25

    vec = lambda t: t.reshape(depth, 1, t.shape[-1])
    b_gate3 = vec(b_gate)
    p = dict(ln1_g=vec(ln1_g), ln1_b=vec(ln1_b), ln2_g=vec(ln2_g), ln2_b=vec(ln2_b))
    mixer_w = (w_up_attn, w_up_hgrn, w_out, w_ff1, w_ff2)
    w_in_b = w_in[0].astype(BF16)
    to_t = lambda c: jnp.transpose(c, (0, 1, 3, 4, 2)).reshape(depth, db, KV_W, WINDOW)
    from_t = lambda c: jnp.transpose(c.reshape(depth, db, KV_HEADS, HEAD_DIM, WINDOW), (0, 1, 4, 2, 3))
    ck_t, cv_t = to_t(cache_k), to_t(cache_v)

    xp = x_prompt.reshape(batch * seq, D_MODEL)
    xs = x_sample.reshape(db * DEC_SEQ, D_MODEL)
    pk, pv, ps = [], [], []
    sk = sv = ss = None
    for l in range(depth):
        sink = attn_sink[l].reshape(1, N_HEADS)
        normw = hgrn_norm_w[l].reshape(1, HG_D)
        q, kv, ks, vs, hq, hk, lf, hv, hg, g, chunk_ok, *wb = _in_proj(xp, w_in_b, b_gate3, hgrn_lb_logits, l,
                                                                       cast=mixer_w)
        a = _attn_prompt(q, ks, vs, sink, batch, seq)
        h, st = _hgrn_prompt(chunk_ok[:, 0, 0], hq, hk, lf, hv, hg, normw, batch, seq)
        xp, next_w_in = _post(a, h, g, xp, wb, p, l, alpha, cast=(w_in,) if l + 1 < depth else (), cast_layer=l + 1)
        win = kv.reshape(batch, seq, 2 * KV_W)[:, seq - WINDOW:].reshape(batch, WINDOW, 2, KV_HEADS, HEAD_DIM)
        pk.append(win[:, :, 0])
        pv.append(win[:, :, 1])
        ps.append(st)
        q, kvn, _, _, hq, hk, lf, hv, hg, g, _ = _in_proj(xs, w_in_b, b_gate3, hgrn_lb_logits, l)
        a, sk, sv = _attn_sample(q, kvn, sink, ck_t, cv_t, l, sk, sv)
        h, ss = _hgrn_sample(hq, hk, lf, hv, hg, normw, state_hgrn, l, ss)
        xs, _ = _post(a, h, g, xs, wb, p, l, alpha)
        if next_w_in:
            w_in_b = next_w_in[0]
    return (xp.reshape(batch, seq, D_MODEL), xs.reshape(db, DEC_SEQ, D_MODEL), jnp.stack(pk), jnp.stack(pv),
            jnp.stack(ps), from_t(sk), from_t(sv), ss)
```

```python
import functools

import jax
import jax.numpy as jnp
from jax import lax
from jax.experimental import pallas as pl
from jax.experimental.pallas import tpu as pltpu

F32 = jnp.float32
BF16 = jnp.bfloat16

D_MODEL = 1024
WINDOW = 128
N_HEADS = 8
KV_HEADS = 2
HEAD_DIM = 64
ATT_W = N_HEADS * HEAD_DIM
KV_W = KV_HEADS * HEAD_DIM
HG_HEADS = 4
HG_D = 128
HG_W = HG_HEADS * HG_D
GATE_W = 2 * D_MODEL
D_FF = 4 * D_MODEL
DEC_SEQ = 4
LN_EPS = 1e-5
RMS_EPS = 1e-6
NEG_BIG = -1e30
LB_FLOOR = 1e-30
LOG2E = 1.4426950408889634
QK_SCALE = HEAD_DIM ** -0.5 * LOG2E

AQ0, AK0, AV0, HQ0, HF0, HI0, HG0, GT0, IN_COLS = 0, 512, 640, 768, 1280, 1792, 2304, 2816, 4864

ATT_BLOCKS = 4
CHUNK = 128
HGRN_CHUNKS = 4
HALF = CHUNK // 2
FAST_DECAY_LIMIT = 120.0
FAST_Q_LIMIT = 1e9
SEQ_PER_STEP = 4
ROWS_PER_STEP = SEQ_PER_STEP * DEC_SEQ
SAMPLE_GROUPS = 4
IN_PROJ_ROWS = 512
POST_ROWS = 512
POST_GROUP = 256
V7X_VMEM_LIMIT = 56 * 1024 * 1024


def _cparams(*sem):
    return pltpu.CompilerParams(dimension_semantics=sem, vmem_limit_bytes=V7X_VMEM_LIMIT)


def _resident(shape, index_map):
    return pl.BlockSpec(shape, index_map, pipeline_mode=pl.Buffered(1))


def _stacked_out(call, layer, stacked, n_in):
    extra, aliases = [], {}
    if layer > 0:
        for out_idx, prev in stacked.items():
            aliases[n_in + len(extra)] = out_idx
            extra.append(prev)
    return call(extra, aliases)


def _lower_bound(logits_ref, layer):
    lg = logits_ref[...]
    e = jnp.exp(lg - jnp.max(lg, axis=0, keepdims=True))
    p = e / jnp.sum(e, axis=0, keepdims=True)
    lb = jnp.zeros((1, HG_W), F32)
    for i in range(1, layer + 1):
        lb = lb + p[i:i + 1]
    return lb


def _swish(z):
    return z * jax.nn.sigmoid(z)


def _forget_and_key(zf, lb):
    e = jnp.exp(-jnp.abs(zf))
    r = 1.0 / (1.0 + e)
    pos = zf >= 0
    sig_p = jnp.where(pos, r, e * r)
    sig_n = jnp.where(pos, e * r, r)
    logf = jnp.log(jnp.maximum(lb, LB_FLOOR) + (1.0 - lb) * sig_p)
    return logf, (1.0 - lb) * sig_n


def _cast_rows(src_refs, dst_refs):
    for src, dst in zip(src_refs, dst_refs):
        dst[...] = src[...].astype(BF16)


def _in_proj_kernel(layer, n_cast, x_ref, w_ref, bg_ref, logits_ref, *refs):
    cast_src, refs = refs[:n_cast], refs[n_cast:]
    (q_ref, kv_ref, ks_ref, vs_ref, hq_ref, hk_ref, lf_ref, hv_ref, hg_ref, g_ref, ok_ref), refs = refs[:11], refs[11:]
    cast_dst, xb_ref = refs[:n_cast], refs[n_cast]
    _cast_rows(cast_src, cast_dst)
    xb = x_ref[...].astype(BF16)
    lb = _lower_bound(logits_ref, layer)
    chunks = x_ref.shape[0] // CHUNK
    q_ok = []
    decay_ok = []

    def put_q(z):
        q_ref[...] = (z * QK_SCALE).astype(BF16)

    def put_kv(z):
        kv_ref[...] = z
        ks_ref[...] = _split_heads(z[:, :KV_W])
        vs_ref[...] = _split_heads(z[:, KV_W:])

    def put_hq(z):
        hq = _swish(z)
        hq_ref[...] = hq.astype(BF16)
        for c in range(chunks):
            q_ok.append(jnp.max(jnp.abs(hq[c * CHUNK:(c + 1) * CHUNK]), keepdims=True) <= FAST_Q_LIMIT)

    def put_hf(z):
        logf, key = _forget_and_key(z, lb)
        lf_ref[...] = logf
        hk_ref[...] = key.astype(BF16)
        for lo in range(0, chunks * CHUNK, HALF):
            total = jnp.sum(logf[lo:lo + HALF], axis=0, keepdims=True)
            decay_ok.append(jnp.min(total, keepdims=True) >= -FAST_DECAY_LIMIT)

    def put_hv(z):
        hv_ref[...] = z.astype(BF16)

    def put_hg(z):
        hg_ref[...] = _swish(z).astype(BF16)

    def put_gates(lo):
        def put(z):
            g_ref[:, lo:lo + 512] = jax.nn.sigmoid(z + bg_ref[:, lo:lo + 512]).astype(BF16)
        return put

    gates = [(GT0 + lo, GT0 + lo + 512, put_gates(lo)) for lo in range(0, GATE_W, 512)]
    stages = [(HF0, HI0, put_hf), (AQ0, AK0, put_q), (HQ0, HF0, put_hq), gates[0], (HG0, GT0, put_hg), gates[1],
              gates[2], gates[3], (HI0, HG0, put_hv), (AK0, HQ0, put_kv)]
    pending = None
    xb_ref[...] = xb
    for lo, hi, put in stages:
        z = jnp.dot(xb_ref[...], w_ref[:, lo:hi], preferred_element_type=F32)
        if pending is not None:
            pending[0](pending[1])
        pending = (put, z)
    pending[0](pending[1])
    for c in range(chunks):
        ok = jnp.logical_and(q_ok[c], jnp.logical_and(decay_ok[2 * c], decay_ok[2 * c + 1]))
        ok_ref[c] = jnp.broadcast_to(jnp.where(ok, 1, 0), (1, 128)).astype(jnp.int32)


def _split_heads(t):
    low = lax.broadcasted_iota(jnp.int32, t.shape, 1) < HEAD_DIM
    h0_lo = jnp.where(low, t, 0.0)
    h1_hi = jnp.where(low, 0.0, t)
    blocks = [h0_lo, pltpu.roll(h0_lo, HEAD_DIM, 1), pltpu.roll(h1_hi, HEAD_DIM, 1), h1_hi]
    return jnp.concatenate(blocks, axis=1).astype(BF16)


def _cast_specs(weights, layer, steps):
    src, dst, shapes = [], [], []
    for w in weights:
        _, r, c = w.shape
        slab = r // steps
        assert r % steps == 0 and (slab % 16 == 0 or steps == 1)
        src.append(pl.BlockSpec((None, slab, c), lambda i, layer=layer: (layer, i, 0)))
        dst.append(pl.BlockSpec((slab, c), lambda i: (i, 0)))
        shapes.append(jax.ShapeDtypeStruct((r, c), BF16))
    return src, dst, shapes


def _in_proj(x, w_b, b_gate, logits, layer, cast=()):
    t = x.shape[0]
    tm = min(IN_PROJ_ROWS, t)
    assert t % tm == 0 and tm % CHUNK == 0
    rows = lambda w: pl.BlockSpec((tm, w), lambda i: (i, 0))
    out = lambda w, dt: jax.ShapeDtypeStruct((t, w), dt)
    cast_src, cast_dst, cast_shapes = _cast_specs(cast, layer, t // tm)
    return pl.pallas_call(
        functools.partial(_in_proj_kernel, layer, len(cast)),
        grid=(t // tm,),
        in_specs=[rows(D_MODEL),
                  _resident((D_MODEL, IN_COLS), lambda i: (0, 0)),
                  _resident((None, 1, GATE_W), lambda i: (layer, 0, 0)),
                  _resident(logits.shape, lambda i: (0, 0))] + cast_src,
        out_specs=[rows(ATT_W), rows(2 * KV_W), rows(4 * KV_W), rows(4 * KV_W),
                   rows(HG_W), rows(HG_W), rows(HG_W), rows(HG_W), rows(HG_W),
                   rows(GATE_W), pl.BlockSpec((tm // CHUNK, 1, 128), lambda i: (i, 0, 0))] + cast_dst,
        out_shape=[out(ATT_W, BF16), out(2 * KV_W, F32), out(4 * KV_W, BF16), out(4 * KV_W, BF16),
                   out(HG_W, BF16), out(HG_W, BF16), out(HG_W, F32), out(HG_W, BF16), out(HG_W, BF16),
                   out(GATE_W, BF16),
                   jax.ShapeDtypeStruct((t // CHUNK, 1, 128), jnp.int32)] + cast_shapes,
        scratch_shapes=[pltpu.VMEM((tm, D_MODEL), BF16)],
        compiler_params=_cparams("parallel"),
        name="in_proj",
    )(x, w_b, b_gate, logits, *cast)


def _attn_prompt_kernel(sink_ref, bias_ref, q_ref, kp_ref, kc_ref, vp_ref, vc_ref, o_ref):
    r = WINDOW
    first_bias = bias_ref[jnp.minimum(pl.program_id(1), 1)]
    top = lax.broadcasted_iota(jnp.int32, (2 * r, 1), 0) < r
    combos = [(n, g, half) for n in range(ATT_BLOCKS) for g in range(KV_HEADS) for half in range(2)]
    lanes = lambda g, half: slice((2 * g + half) * KV_W, (2 * g + half + 1) * KV_W)

    def kv_rows(prev_ref, cur_ref, n, g, half):
        before = prev_ref[:, lanes(g, half)] if n == 0 else cur_ref[(n - 1) * r:n * r, lanes(g, half)]
        return jnp.concatenate([before, cur_ref[n * r:(n + 1) * r, lanes(g, half)]], axis=0)

    def q_rows(n, g):
        return jnp.concatenate([q_ref[n * r:(n + 1) * r, 2 * g * 128:(2 * g + 1) * 128],
                                q_ref[n * r:(n + 1) * r, (2 * g + 1) * 128:(2 * g + 2) * 128]], axis=0)

    bias = [jnp.concatenate([b, b], axis=0) for b in [first_bias] + [bias_ref[1]] * (ATT_BLOCKS - 1)]
    keys = [kv_rows(kp_ref, kc_ref, *c) for c in combos]
    vals = [kv_rows(vp_ref, vc_ref, *c) for c in combos]
    sink = [jnp.where(top, sink_ref[0, 4 * g + half], sink_ref[0, 4 * g + 2 + half]) * LOG2E for _, g, half in combos]
    s = [lax.dot_general(q_rows(n, g), k, (((1,), (1,)), ((), ())), preferred_element_type=F32) + bias[n]
         for (n, g, _), k in zip(combos, keys)]
    m = [jnp.maximum(jnp.max(t, axis=-1, keepdims=True), sk) for t, sk in zip(s, sink)]
    p = [jnp.exp2(t - mx) for t, mx in zip(s, m)]
    den = [jnp.sum(t, axis=-1, keepdims=True) + jnp.exp2(sk - mx) for t, sk, mx in zip(p, sink, m)]
    o = [jnp.dot(t.astype(BF16), v, preferred_element_type=F32) / d for t, v, d in zip(p, vals, den)]
    for n in range(ATT_BLOCKS):
        cols = []
        for g in range(KV_HEADS):
            acc = o[4 * n + 2 * g] + o[4 * n + 2 * g + 1]
            cols += [acc[:r], acc[r:]]
        o_ref[n * r:(n + 1) * r, :] = jnp.concatenate(cols, axis=1).astype(o_ref.dtype)


def _band_bias():
    qi = lax.broadcasted_iota(jnp.int32, (WINDOW, 2 * WINDOW), 0)
    kj = lax.broadcasted_iota(jnp.int32, (WINDOW, 2 * WINDOW), 1)
    band = jnp.logical_and(kj >= qi, kj <= qi + WINDOW)
    first = jnp.logical_and(band, kj >= WINDOW)
    return jnp.where(jnp.stack([first, band]), 0.0, NEG_BIG).astype(F32)


def _attn_prompt(q, ks, vs, sink, batch, seq):
    steps = seq // (ATT_BLOCKS * WINDOW)
    assert seq % (ATT_BLOCKS * WINDOW) == 0
    cur = lambda b, i: (b * steps + i, 0)
    prev = lambda b, i: ((b * steps + i) * ATT_BLOCKS - jnp.minimum(i, 1), 0)
    prev_spec = pl.BlockSpec((WINDOW, 4 * KV_W), prev)
    cur_spec = pl.BlockSpec((ATT_BLOCKS * WINDOW, 4 * KV_W), cur)
    return pl.pallas_call(
        _attn_prompt_kernel,
        grid=(batch, steps),
        in_specs=[pl.BlockSpec(memory_space=pltpu.SMEM),
                  _resident((2, WINDOW, 2 * WINDOW), lambda b, i: (0, 0, 0)),
                  pl.BlockSpec((ATT_BLOCKS * WINDOW, ATT_W), cur),
                  prev_spec, cur_spec, prev_spec, cur_spec],
        out_specs=pl.BlockSpec((ATT_BLOCKS * WINDOW, ATT_W), cur),
        out_shape=jax.ShapeDtypeStruct((batch * seq, ATT_W), BF16),
        compiler_params=_cparams("parallel", "parallel"),
        name="attn_prompt",
    )(sink, _band_bias(), q, ks, ks, vs, vs)


def _interleave(stage_generators):
    pending = list(stage_generators)
    while pending:
        for gen in list(pending):
            if next(gen, StopIteration) is StopIteration:
                pending.remove(gen)


def _attn_sample_kernel(sink_ref, q_ref, kvn_ref, ck_ref, cv_ref, *rest):
    o_ref, nk_ref, nv_ref = rest[-3:]
    _interleave(_attn_sample_group(grp, sink_ref, q_ref, kvn_ref, ck_ref, cv_ref, o_ref, nk_ref, nv_ref)
                for grp in range(SAMPLE_GROUPS))


def _attn_sample_group(grp, sink_ref, q_ref, kvn_ref, ck_ref, cv_ref, o_ref, nk_ref, nv_ref):
    rows = slice(grp * ROWS_PER_STEP, (grp + 1) * ROWS_PER_STEP)
    ck_ref, cv_ref, nk_ref, nv_ref = (r.at[grp * SEQ_PER_STEP:(grp + 1) * SEQ_PER_STEP]
                                      for r in (ck_ref, cv_ref, nk_ref, nv_ref))
    lane = lax.broadcasted_iota(jnp.int32, (ROWS_PER_STEP, KV_W), 1)
    q16 = q_ref[rows, :].astype(F32)
    blocks = []
    for h in range(N_HEADS):
        c, half, g = h // 2, h % 2, h // (N_HEADS // KV_HEADS)
        own = lane >= HEAD_DIM if half else lane < HEAD_DIM
        qh = jnp.where(own, q16[:, c * 128:(c + 1) * 128], 0.0)
        blocks.append(qh if half == g else pltpu.roll(qh, HEAD_DIM, 1))
    wt = jnp.concatenate(blocks, axis=0).astype(BF16)

    row = lax.broadcasted_iota(jnp.int32, (WINDOW, WINDOW), 0)
    col = lax.broadcasted_iota(jnp.int32, (WINDOW, WINDOW), 1)
    tok = jnp.bitwise_and(row, DEC_SEQ - 1)
    row_seq = jnp.bitwise_and(jnp.right_shift(row, 2), SEQ_PER_STEP - 1)
    pad = jnp.zeros((WINDOW - ROWS_PER_STEP, KV_W), F32)
    kn = jnp.concatenate([kvn_ref[rows, :KV_W], pad], axis=0)
    vn = jnp.concatenate([kvn_ref[rows, KV_W:], pad], axis=0)
    kn_t = jnp.transpose(kn)
    vn_t = jnp.transpose(vn)
    yield

    s_c = jnp.zeros((WINDOW, WINDOW), F32)
    for j in range(SEQ_PER_STEP):
        sj = jnp.dot(wt, ck_ref[j].astype(BF16), preferred_element_type=F32)
        s_c = jnp.where(row_seq == j, sj, s_c)
    s_n = jnp.dot(wt, kn_t.astype(BF16), preferred_element_type=F32)
    yield
    s_c = jnp.where(col >= tok, s_c, NEG_BIG)
    new_ok = jnp.logical_and(jnp.right_shift(col, 2) == row_seq, jnp.bitwise_and(col, DEC_SEQ - 1) <= tok)
    s_n = jnp.where(new_ok, s_n, NEG_BIG)
    head = jnp.right_shift(lax.broadcasted_iota(jnp.int32, (WINDOW, 1), 0), 4)
    sink = jnp.zeros((WINDOW, 1), F32)
    for h in range(N_HEADS):
        sink = jnp.where(head == h, sink_ref[0, h] * LOG2E, sink)
    m = jnp.maximum(jnp.maximum(jnp.max(s_c, axis=-1, keepdims=True), jnp.max(s_n, axis=-1, keepdims=True)), sink)
    p_c = jnp.exp2(s_c - m)
    p_n = jnp.exp2(s_n - m)
    den = jnp.sum(p_c, axis=-1, keepdims=True) + jnp.sum(p_n, axis=-1, keepdims=True) + jnp.exp2(sink - m)
    p_cb = p_c.astype(BF16)
    yield
    o = jnp.dot(p_n.astype(BF16), vn.astype(BF16), preferred_element_type=F32)
    for j in range(SEQ_PER_STEP):
        oj = lax.dot_general(p_cb, cv_ref[j].astype(BF16), (((1,), (1,)), ((), ())), preferred_element_type=F32)
        o = o + jnp.where(row_seq == j, oj, 0.0)
    yield
    o = o / den
    outs = []
    for c in range(4):
        lo, hi = o[32 * c:32 * c + 16], o[32 * c + 16:32 * c + 32]
        if c < 2:
            outs.append(jnp.where(lane < HEAD_DIM, lo, pltpu.roll(hi, HEAD_DIM, 1)))
        else:
            outs.append(jnp.where(lane < HEAD_DIM, pltpu.roll(lo, HEAD_DIM, 1), hi))
    o_ref[rows, :] = jnp.concatenate(outs, axis=1).astype(o_ref.dtype)
    yield
    for j in range(SEQ_PER_STEP):
        shift = (WINDOW - DEC_SEQ - DEC_SEQ * j) % WINDOW
        for new_t, cache_ref, dst in ((kn_t, ck_ref, nk_ref), (vn_t, cv_ref, nv_ref)):
            dst[j] = jnp.where(col >= WINDOW - DEC_SEQ, pltpu.roll(new_t, shift, 1),
                               pltpu.roll(cache_ref[j], WINDOW - DEC_SEQ, 1))


def _attn_sample(q, kvn, sink, cache_kt, cache_vt, layer, prev_k, prev_v):
    depth, db = cache_kt.shape[:2]
    seqs = SAMPLE_GROUPS * SEQ_PER_STEP
    cache_spec = pl.BlockSpec((None, seqs, KV_W, WINDOW), lambda s: (layer, s, 0, 0))
    rows = lambda w: pl.BlockSpec((seqs * DEC_SEQ, w), lambda s: (s, 0))
    n_in = 5

    def call(extra, aliases):
        return pl.pallas_call(
            _attn_sample_kernel,
            grid=(db // seqs,),
            in_specs=[pl.BlockSpec(memory_space=pltpu.SMEM), rows(ATT_W), rows(2 * KV_W), cache_spec, cache_spec]
            + [pl.BlockSpec(memory_space=pl.ANY)] * len(extra),
            out_specs=[rows(ATT_W), cache_spec, cache_spec],
            out_shape=[jax.ShapeDtypeStruct((db * DEC_SEQ, ATT_W), BF16),
                       jax.ShapeDtypeStruct((depth, db, KV_W, WINDOW), F32),
                       jax.ShapeDtypeStruct((depth, db, KV_W, WINDOW), F32)],
            input_output_aliases=aliases,
            compiler_params=_cparams("parallel"),
            name="attn_sample",
        )(sink, q, kvn, cache_kt, cache_vt, *extra)

    return _stacked_out(call, layer, {1: prev_k, 2: prev_v}, n_in)


def _hgrn_out(o, g_act, normw):
    ms = jnp.mean(o * o, axis=-1, keepdims=True)
    return o * lax.rsqrt(ms + RMS_EPS) * normw * g_act


def _cumsum_rows(x, tri):
    hi = x.astype(BF16)
    r1 = x - hi.astype(F32)
    mid = r1.astype(BF16)
    lo = (r1 - mid.astype(F32)).astype(BF16)
    dot = lambda t: jnp.dot(tri, t, preferred_element_type=F32)
    return dot(hi) + dot(mid) + dot(lo)


def _heads(t):
    return [t[:, h * HG_D:(h + 1) * HG_D] for h in range(HG_HEADS)]


def _level_reference(b, n, row):
    if n == 2:
        return jnp.where(jnp.bitwise_and(row, 1) == 0, b, pltpu.roll(b, 1, 0))
    if n == 4:
        m4 = jnp.bitwise_and(row, 3)
        return jnp.where(m4 == 0, pltpu.roll(b, CHUNK - 1, 0),
                         jnp.where(m4 == 1, b, jnp.where(m4 == 2, pltpu.roll(b, 1, 0), pltpu.roll(b, 2, 0))))
    bounds = [m * n + n // 2 - 1 for m in range(CHUNK // n)]
    pieces = [jnp.broadcast_to(b[t:t + 1], (n, b.shape[1])) for t in bounds]
    return pieces[0] if len(pieces) == 1 else jnp.concatenate(pieces, axis=0)


def _scores(q, k, e_q, e_k):
    qs, ks = _heads(q * e_q.astype(BF16)), _heads(k * e_k.astype(BF16))
    return [lax.dot_general(a, c, (((1,), (1,)), ((), ())), preferred_element_type=F32) for a, c in zip(qs, ks)]


def _intra_chunk_scores(q, k, b, lvl, fast):
    if fast:
        b_mid, b_end = b[HALF - 1:HALF], b[CHUNK - 1:CHUNK]
        r = jnp.concatenate([jnp.broadcast_to(0.5 * b_mid, (HALF, HG_W)),
                             jnp.broadcast_to(0.5 * (b_mid + b_end), (HALF, HG_W))], axis=0)
        e = jnp.exp(b - r)
        e6 = jnp.exp(-jnp.abs(b - b_mid))
        return [jnp.where(lvl == 6, across, jnp.where(lvl >= -1, within, 0.0))
                for within, across in zip(_scores(q, k, e, 1.0 / e), _scores(q, k, e6, e6))]
    row = lax.broadcasted_iota(jnp.int32, (CHUNK, HG_W), 0)
    atts = [jnp.where(lvl == -1, jnp.sum(qk, axis=-1, keepdims=True), 0.0) for qk in _heads((q * k).astype(F32))]
    for level in range(7):
        e = jnp.exp(-jnp.abs(b - _level_reference(b, 2 << level, row)))
        atts = [jnp.where(lvl == level, p, att) for p, att in zip(_scores(q, k, e, e), atts)]
    return atts


def _hgrn_chunks(fast, hq_ref, hk_ref, lf_ref, hv_ref, hg_ref, normw_ref, o_ref, st_ref):
    row = lax.broadcasted_iota(jnp.int32, (CHUNK, CHUNK), 0)
    col = lax.broadcasted_iota(jnp.int32, (CHUNK, CHUNK), 1)
    tri = jnp.where(row >= col, 1.0, 0.0).astype(BF16)
    lvl = jnp.where(row >= col, 31 - lax.clz(jnp.bitwise_xor(row, col)), -2)
    subs = [slice(c * CHUNK, (c + 1) * CHUNK) for c in range(hq_ref.shape[0] // CHUNK)]
    q, k, v = ([ref[rs] for rs in subs] for ref in (hq_ref, hk_ref, hv_ref))
    b = [_cumsum_rows(lf_ref[rs], tri) for rs in subs]
    b_end = [t[CHUNK - 1:CHUNK] for t in b]
    atts = [_intra_chunk_scores(qc, kc, bc, lvl, fast) for qc, kc, bc in zip(q, k, b)]
    q_in = [_heads(qc * jnp.exp(bc).astype(BF16)) for qc, bc in zip(q, b)]
    k_out = [_heads(kc.astype(F32) * jnp.exp(be - bc)) for kc, bc, be in zip(k, b, b_end)]
    k_t = [[jnp.transpose(t).astype(BF16) for t in per_head] for per_head in k_out]
    decay = [[jnp.transpose(jnp.broadcast_to(t, (HG_D, HG_D))) for t in _heads(jnp.exp(be))] for be in b_end]
    vs = [_heads(vc) for vc in v]
    local = [[jnp.dot(a.astype(BF16), vh, preferred_element_type=F32) for a, vh in zip(ac, vc)]
             for ac, vc in zip(atts, vs)]
    grow = [[jnp.dot(kt, vh, preferred_element_type=F32) for kt, vh in zip(kc, vc)] for kc, vc in zip(k_t, vs)]
    state = [st_ref[0, h] for h in range(HG_HEADS)]
    for c, rs in enumerate(subs):
        gs = _heads(hg_ref[rs])
        outs = [_hgrn_out(jnp.dot(q_in[c][h], state[h].astype(BF16), preferred_element_type=F32) + local[c][h],
                          gs[h], normw_ref[...]) for h in range(HG_HEADS)]
        o_ref[rs] = jnp.concatenate(outs, axis=1).astype(o_ref.dtype)
        state = [decay[c][h] * state[h] + grow[c][h] for h in range(HG_HEADS)]
    for h in range(HG_HEADS):
        st_ref[0, h] = state[h]


def _hgrn_prompt_kernel(ok_ref, *refs):
    st_ref = refs[-1]

    @pl.when(pl.program_id(1) == 0)
    def _():
        st_ref[...] = jnp.zeros_like(st_ref)

    first = (pl.program_id(0) * pl.num_programs(1) + pl.program_id(1)) * HGRN_CHUNKS
    ok = ok_ref[first] != 0
    for c in range(1, HGRN_CHUNKS):
        ok = jnp.logical_and(ok, ok_ref[first + c] != 0)

    @pl.when(ok)
    def _():
        _hgrn_chunks(True, *refs)

    @pl.when(jnp.logical_not(ok))
    def _():
        _hgrn_chunks(False, *refs)


def _hgrn_prompt(chunk_ok, hq, hk, lf, hv, hg, normw, batch, seq):
    nc = seq // (HGRN_CHUNKS * CHUNK)
    assert seq % (HGRN_CHUNKS * CHUNK) == 0
    rows = pl.BlockSpec((HGRN_CHUNKS * CHUNK, HG_W), lambda b, i, ok: (b * nc + i, 0))
    return pl.pallas_call(
        _hgrn_prompt_kernel,
        grid_spec=pltpu.PrefetchScalarGridSpec(
            num_scalar_prefetch=1,
            grid=(batch, nc),
            in_specs=[rows, rows, rows, rows, rows, pl.BlockSpec((1, HG_D), lambda b, i, ok: (0, 0))],
            out_specs=[rows, pl.BlockSpec((1, HG_HEADS, HG_D, HG_D), lambda b, i, ok: (b, 0, 0, 0))]),
        out_shape=[jax.ShapeDtypeStruct((batch * seq, HG_W), BF16),
                   jax.ShapeDtypeStruct((batch, HG_HEADS, HG_D, HG_D), F32)],
        compiler_params=_cparams("parallel", "arbitrary"),
        name="hgrn_prompt",
    )(chunk_ok, hq, hk, lf, hv, hg, normw)


def _hgrn_sample_kernel(hq_ref, hk_ref, lf_ref, hv_ref, hg_ref, normw_ref, st_ref, *rest):
    o_ref, ns_ref = rest[-2:]
    _interleave(_hgrn_sample_group(grp, hq_ref, hk_ref, lf_ref, hv_ref, hg_ref, normw_ref, st_ref, o_ref, ns_ref)
                for grp in range(SAMPLE_GROUPS))


def _hgrn_sample_group(grp, hq_ref, hk_ref, lf_ref, hv_ref, hg_ref, normw_ref, st_ref, o_ref, ns_ref):
    rows = slice(grp * ROWS_PER_STEP, (grp + 1) * ROWS_PER_STEP)
    st_ref, ns_ref = (r.at[grp * SEQ_PER_STEP:(grp + 1) * SEQ_PER_STEP] for r in (st_ref, ns_ref))
    q, k, v = (r[rows, :].astype(F32) for r in (hq_ref, hk_ref, hv_ref))
    logf = lf_ref[rows, :]
    row = lax.broadcasted_iota(jnp.int32, (ROWS_PER_STEP, HG_W), 0)
    tok = jnp.bitwise_and(row, DEC_SEQ - 1)
    seq_of_row = jnp.right_shift(row[:, :HG_D], 2)
    down = lambda t, d: pltpu.roll(t, d, 0)
    up = lambda t, d: pltpu.roll(t, ROWS_PER_STEP - d, 0)
    b = logf
    for d in range(1, DEC_SEQ):
        b = b + jnp.where(tok >= d, down(logf, d), 0.0)
    b_end = jnp.where(tok == 3, b, jnp.where(tok == 2, up(b, 1), jnp.where(tok == 1, up(b, 2), up(b, 3))))
    qe = (q * jnp.exp(b)).astype(BF16)
    kd = k * jnp.exp(b_end - b)
    e_end = jnp.exp(b_end)
    intra = [jnp.zeros((ROWS_PER_STEP, HG_D), F32) for _ in range(HG_HEADS)]
    for d in range(DEC_SEQ):
        kd_, bd_, vd_ = (k, b, v) if d == 0 else (down(k, d), down(b, d), down(v, d))
        w = jnp.where(tok >= d, q * kd_ * jnp.exp(jnp.minimum(b - bd_, 0.0)), 0.0)
        for h in range(HG_HEADS):
            sl = slice(h * HG_D, (h + 1) * HG_D)
            intra[h] = intra[h] + jnp.sum(w[:, sl], axis=-1, keepdims=True) * vd_[:, sl]
    pad = jnp.zeros((HG_D - 2 * ROWS_PER_STEP, HG_D), F32)
    colh = lax.broadcasted_iota(jnp.int32, (HG_D, HG_D), 1)
    normw = normw_ref[...]
    yield
    for h in range(HG_HEADS):
        sl = slice(h * HG_D, (h + 1) * HG_D)
        vb = jnp.concatenate([v[:, sl], jnp.zeros((HG_D - ROWS_PER_STEP, HG_D), F32)], axis=0).astype(BF16)
        tt = jnp.transpose(jnp.concatenate([kd[:, sl], e_end[:, sl], pad], axis=0))
        inter = jnp.zeros((ROWS_PER_STEP, HG_D), F32)
        for j in range(SEQ_PER_STEP):
            s0 = st_ref[j, h]
            oi = jnp.dot(qe[:, sl], s0.astype(BF16), preferred_element_type=F32)
            inter = jnp.where(seq_of_row == j, oi, inter)
            kd_t = jnp.where(jnp.right_shift(colh, 2) == j, tt, 0.0).astype(BF16)
            e_col = ROWS_PER_STEP + DEC_SEQ * j
            decay = jnp.broadcast_to(tt[:, e_col:e_col + 1], (HG_D, HG_D))
            ns_ref[j, h] = decay * s0 + jnp.dot(kd_t, vb, preferred_element_type=F32)
        o_ref[rows, sl] = _hgrn_out(inter + intra[h], hg_ref[rows, sl], normw).astype(o_ref.dtype)
        yield


def _hgrn_sample(hq, hk, lf, hv, hg, normw, state, layer, prev_state):
    depth, db = state.shape[:2]
    seqs = SAMPLE_GROUPS * SEQ_PER_STEP
    rows = pl.BlockSpec((seqs * DEC_SEQ, HG_W), lambda s: (s, 0))
    st_spec = pl.BlockSpec((None, seqs, HG_HEADS, HG_D, HG_D), lambda s: (layer, s, 0, 0, 0))
    n_in = 7

    def call(extra, aliases):
        return pl.pallas_call(
            _hgrn_sample_kernel,
            grid=(db // seqs,),
            in_specs=[rows, rows, rows, rows, rows, pl.BlockSpec((1, HG_D), lambda s: (0, 0)), st_spec]
            + [pl.BlockSpec(memory_space=pl.ANY)] * len(extra),
            out_specs=[rows, st_spec],
            out_shape=[jax.ShapeDtypeStruct((db * DEC_SEQ, HG_W), BF16), jax.ShapeDtypeStruct(state.shape, F32)],
            input_output_aliases=aliases,
            compiler_params=_cparams("parallel"),
            name="hgrn_sample",
        )(hq, hk, lf, hv, hg, normw, state, *extra)

    return _stacked_out(call, layer, {1: prev_state}, n_in)


def _layer_norm(y, g, b):
    mu = jnp.mean(y, axis=-1, keepdims=True)
    yc = y - mu
    var = jnp.mean(yc * yc, axis=-1, keepdims=True)
    return yc * lax.rsqrt(var + LN_EPS) * g + b


def _post_kernel(alpha, n_cast, a_ref, h_ref, g_ref, x_ref, wua_ref, wuh_ref, wo_ref, l1g_ref, l1b_ref,
                 wf1_ref, wf2_ref, l2g_ref, l2b_ref, *refs):
    o_ref = refs[n_cast]
    _cast_rows(refs[:n_cast], refs[n_cast + 1:])
    groups = [slice(r0, r0 + POST_GROUP) for r0 in range(0, x_ref.shape[0], POST_GROUP)]
    dot = lambda lhs, w_ref: jnp.dot(lhs, w_ref[...], preferred_element_type=F32)
    ua = [dot(a_ref[rs], wua_ref) for rs in groups]
    uh = [dot(h_ref[rs], wuh_ref) for rs in groups]
    merged = [(g_ref[rs, :D_MODEL] * a + g_ref[rs, D_MODEL:] * h).astype(BF16) for rs, a, h in zip(groups, ua, uh)]
    m = [dot(t, wo_ref) for t in merged]
    x1 = [_layer_norm(alpha * x_ref[rs] + t, l1g_ref[...], l1b_ref[...]) for rs, t in zip(groups, m)]
    hid = [jnp.square(jnp.maximum(dot(t.astype(BF16), wf1_ref), 0.0)).astype(BF16) for t in x1]
    ff = [dot(t, wf2_ref) for t in hid]
    for rs, t, f in zip(groups, x1, ff):
        o_ref[rs] = _layer_norm(alpha * t + f, l2g_ref[...], l2b_ref[...])


def _post(a, h, g, x, wb, p, layer, alpha, cast=(), cast_layer=0):
    t = x.shape[0]
    tm = min(POST_ROWS, t)
    assert t % tm == 0
    rows = lambda w: pl.BlockSpec((tm, w), lambda i: (i, 0))
    vec = lambda w: _resident((None, 1, w), lambda i: (layer, 0, 0))
    mat = lambda w: _resident(w.shape, lambda i: (0, 0))
    cast_src, cast_dst, cast_shapes = _cast_specs(cast, cast_layer, t // tm)
    outs = pl.pallas_call(
        functools.partial(_post_kernel, alpha, len(cast)),
        grid=(t // tm,),
        in_specs=[rows(ATT_W), rows(HG_W), rows(GATE_W), rows(D_MODEL), mat(wb[0]), mat(wb[1]), mat(wb[2]),
                  vec(D_MODEL), vec(D_MODEL), mat(wb[3]), mat(wb[4]), vec(D_MODEL), vec(D_MODEL)] + cast_src,
        out_specs=[rows(D_MODEL)] + cast_dst,
        out_shape=[jax.ShapeDtypeStruct((t, D_MODEL), F32)] + cast_shapes,
        compiler_params=_cparams("parallel"),
        name="post",
    )(a, h, g, x, wb[0], wb[1], wb[2], p["ln1_g"], p["ln1_b"], wb[3], wb[4], p["ln2_g"], p["ln2_b"], *cast)
    return outs[0], outs[1:]


def kernel(x_prompt, x_sample, cache_k, cache_v, state_hgrn, w_in, b_gate, attn_sink, hgrn_lb_logits, hgrn_norm_w,
           w_up_attn, w_up_hgrn, w_out, ln1_g, ln1_b, w_ff1, w_ff2, ln2_g, ln2_b):
    depth = w_in.shape[0]
    batch, seq, _ = x_prompt.shape
    db, dec_seq, _ = x_sample.shape
    assert dec_seq == DEC_SEQ and seq % CHUNK == 0 and db % (SAMPLE_GROUPS * SEQ_PER_STEP) == 0
    assert cache_k.shape[2:] == (WINDOW, KV_HEADS, HEAD_DIM)
    alpha = (2 * depth) ** 0.25

    vec = lambda t: t.reshape(depth, 1, t.shape[-1])
    b_gate3 = vec(b_gate)
    p = dict(ln1_g=vec(ln1_g), ln1_b=vec(ln1_b), ln2_g=vec(ln2_g), ln2_b=vec(ln2_b))
    mixer_w = (w_up_attn, w_up_hgrn, w_out, w_ff1, w_ff2)
    w_in_b = w_in[0].astype(BF16)
    to_t = lambda c: jnp.transpose(c, (0, 1, 3, 4, 2)).reshape(depth, db, KV_W, WINDOW)
    from_t = lambda c: jnp.transpose(c.reshape(depth, db, KV_HEADS, HEAD_DIM, WINDOW), (0, 1, 4, 2, 3))
    ck_t, cv_t = to_t(cache_k), to_t(cache_v)

    xp = x_prompt.reshape(batch * seq, D_MODEL)
    xs = x_sample.reshape(db * DEC_SEQ, D_MODEL)
    pk, pv, ps = [], [], []
    sk = sv = ss = None
    for l in range(depth):
        sink = attn_sink[l].reshape(1, N_HEADS)
        normw = hgrn_norm_w[l].reshape(1, HG_D)
        q, kv, ks, vs, hq, hk, lf, hv, hg, g, chunk_ok, *wb = _in_proj(xp, w_in_b, b_gate3, hgrn_lb_logits, l,
                                                                       cast=mixer_w)
        a = _attn_prompt(q, ks, vs, sink, batch, seq)
        h, st = _hgrn_prompt(chunk_ok[:, 0, 0], hq, hk, lf, hv, hg, normw, batch, seq)
        xp, next_w_in = _post(a, h, g, xp, wb, p, l, alpha, cast=(w_in,) if l + 1 < depth else (), cast_layer=l + 1)
        win = kv.reshape(batch, seq, 2 * KV_W)[:, seq - WINDOW:].reshape(batch, WINDOW, 2, KV_HEADS, HEAD_DIM)
        pk.append(win[:, :, 0])
        pv.append(win[:, :, 1])
        ps.append(st)
        q, kvn, _, _, hq, hk, lf, hv, hg, g, _ = _in_proj(xs, w_in_b, b_gate3, hgrn_lb_logits, l)
        a, sk, sv = _attn_sample(q, kvn, sink, ck_t, cv_t, l, sk, sv)
        h, ss = _hgrn_sample(hq, hk, lf, hv, hg, normw, state_hgrn, l, ss)
        xs, _ = _post(a, h, g, xs, wb, p, l, alpha)
        if next_w_in:
            w_in_b = next_w_in[0]
    return (xp.reshape(batch, seq, D_MODEL), xs.reshape(db, DEC_SEQ, D_MODEL), jnp.stack(pk), jnp.stack(pv),
            jnp.stack(ps), from_t(sk), from_t(sv), ss)
```

```python
import functools

import jax
import jax.numpy as jnp
from jax import lax
from jax.experimental import pallas as pl
from jax.experimental.pallas import tpu as pltpu

F32 = jnp.float32
BF16 = jnp.bfloat16

D_MODEL = 1024
WINDOW = 128
N_HEADS = 8
KV_HEADS = 2
HEAD_DIM = 64
ATT_W = N_HEADS * HEAD_DIM
KV_W = KV_HEADS * HEAD_DIM
HG_HEADS = 4
HG_D = 128
HG_W = HG_HEADS * HG_D
GATE_W = 2 * D_MODEL
D_FF = 4 * D_MODEL
DEC_SEQ = 4
LN_EPS = 1e-5
RMS_EPS = 1e-6
NEG_BIG = -1e30
LB_FLOOR = 1e-30
LOG2E = 1.4426950408889634
QK_SCALE = HEAD_DIM ** -0.5 * LOG2E

AQ0, AK0, AV0, HQ0, HF0, HI0, HG0, GT0, IN_COLS = 0, 512, 640, 768, 1280, 1792, 2304, 2816, 4864

ATT_BLOCKS = 4
CHUNK = 128
HGRN_CHUNKS = 4
HALF = CHUNK // 2
FAST_DECAY_LIMIT = 120.0
FAST_Q_LIMIT = 1e9
SEQ_PER_STEP = 4
ROWS_PER_STEP = SEQ_PER_STEP * DEC_SEQ
SAMPLE_GROUPS = 4
IN_PROJ_ROWS = 512
POST_ROWS = 512
POST_GROUP = 256
V7X_VMEM_LIMIT = 56 * 1024 * 1024


def _cparams(*sem):
    return pltpu.CompilerParams(dimension_semantics=sem, vmem_limit_bytes=V7X_VMEM_LIMIT)


def _resident(shape, index_map):
    return pl.BlockSpec(shape, index_map, pipeline_mode=pl.Buffered(1))


def _stacked_out(call, layer, stacked, n_in):
    extra, aliases = [], {}
    if layer > 0:
        for out_idx, prev in stacked.items():
            aliases[n_in + len(extra)] = out_idx
            extra.append(prev)
    return call(extra, aliases)


def _lower_bound(logits_ref, layer):
    lg = logits_ref[...]
    e = jnp.exp(lg - jnp.max(lg, axis=0, keepdims=True))
    p = e / jnp.sum(e, axis=0, keepdims=True)
    lb = jnp.zeros((1, HG_W), F32)
    for i in range(1, layer + 1):
        lb = lb + p[i:i + 1]
    return lb


def _swish(z):
    return z * jax.nn.sigmoid(z)


def _forget_and_key(zf, lb):
    e = jnp.exp(-jnp.abs(zf))
    r = 1.0 / (1.0 + e)
    pos = zf >= 0
    sig_p = jnp.where(pos, r, e * r)
    sig_n = jnp.where(pos, e * r, r)
    logf = jnp.log(jnp.maximum(lb, LB_FLOOR) + (1.0 - lb) * sig_p)
    return logf, (1.0 - lb) * sig_n


def _cast_rows(src_refs, dst_refs):
    for src, dst in zip(src_refs, dst_refs):
        dst[...] = src[...].astype(BF16)


def _in_proj_kernel(layer, n_cast, x_ref, w_ref, bg_ref, logits_ref, *refs):
    cast_src, refs = refs[:n_cast], refs[n_cast:]
    outs, refs = refs[:11], refs[11:]
    cast_dst, xb_ref = refs[:n_cast], refs[n_cast]
    _cast_rows(cast_src, cast_dst)
    for _ in _in_proj_stages(layer, x_ref, w_ref, bg_ref, logits_ref, outs, xb_ref):
        pass


def _in_proj_stages(layer, x_ref, w_ref, bg_ref, logits_ref, outs, xb_ref):
    q_ref, kv_ref, ks_ref, vs_ref, hq_ref, hk_ref, lf_ref, hv_ref, hg_ref, g_ref, ok_ref = outs
    xb = x_ref[...].astype(BF16)
    lb = _lower_bound(logits_ref, layer)
    chunks = x_ref.shape[0] // CHUNK
    q_ok = []
    decay_ok = []

    def put_q(z):
        q_ref[...] = (z * QK_SCALE).astype(BF16)

    def put_kv(z):
        kv_ref[...] = z
        ks_ref[...] = _split_heads(z[:, :KV_W])
        vs_ref[...] = _split_heads(z[:, KV_W:])

    def put_hq(z):
        hq = _swish(z)
        hq_ref[...] = hq.astype(BF16)
        for c in range(chunks):
            q_ok.append(jnp.max(jnp.abs(hq[c * CHUNK:(c + 1) * CHUNK]), keepdims=True) <= FAST_Q_LIMIT)

    def put_hf(z):
        logf, key = _forget_and_key(z, lb)
        lf_ref[...] = logf
        hk_ref[...] = key.astype(BF16)
        for lo in range(0, chunks * CHUNK, HALF):
            total = jnp.sum(logf[lo:lo + HALF], axis=0, keepdims=True)
            decay_ok.append(jnp.min(total, keepdims=True) >= -FAST_DECAY_LIMIT)

    def put_hv(z):
        hv_ref[...] = z.astype(BF16)

    def put_hg(z):
        hg_ref[...] = _swish(z)

    def put_gates(lo):
        def put(z):
            g_ref[:, lo:lo + 512] = z + bg_ref[:, lo:lo + 512]
        return put

    gates = [(GT0 + lo, GT0 + lo + 512, put_gates(lo)) for lo in range(0, GATE_W, 512)]
    stages = [(HF0, HI0, put_hf), gates[0], (HQ0, HF0, put_hq), gates[1], (AK0, HQ0, put_kv), gates[2],
              (HG0, GT0, put_hg), (AQ0, AK0, put_q), (HI0, HG0, put_hv), gates[3]]
    pending = None
    xb_ref[...] = xb
    for lo, hi, put in stages:
        z = jnp.dot(xb_ref[...], w_ref[:, lo:hi], preferred_element_type=F32)
        if pending is not None:
            pending[0](pending[1])
        pending = (put, z)
        yield
    pending[0](pending[1])
    for c in range(chunks):
        ok = jnp.logical_and(q_ok[c], jnp.logical_and(decay_ok[2 * c], decay_ok[2 * c + 1]))
        ok_ref[c] = jnp.broadcast_to(jnp.where(ok, 1, 0), (1, 128)).astype(jnp.int32)


def _split_heads(t):
    low = lax.broadcasted_iota(jnp.int32, t.shape, 1) < HEAD_DIM
    h0_lo = jnp.where(low, t, 0.0)
    h1_hi = jnp.where(low, 0.0, t)
    blocks = [h0_lo, pltpu.roll(h0_lo, HEAD_DIM, 1), pltpu.roll(h1_hi, HEAD_DIM, 1), h1_hi]
    return jnp.concatenate(blocks, axis=1).astype(BF16)


def _cast_specs(weights, layer, steps):
    src, dst, shapes = [], [], []
    for w in weights:
        _, r, c = w.shape
        slab = r // steps
        assert r % steps == 0 and (slab % 16 == 0 or steps == 1)
        last = steps - 1
        src.append(pl.BlockSpec((None, slab, c), lambda i, layer=layer, last=last: (layer, jnp.minimum(i, last), 0)))
        dst.append(pl.BlockSpec((slab, c), lambda i, last=last: (jnp.minimum(i, last), 0)))
        shapes.append(jax.ShapeDtypeStruct((r, c), BF16))
    return src, dst, shapes


def _in_proj(x, w_b, b_gate, logits, layer, cast=()):
    t = x.shape[0]
    tm = min(IN_PROJ_ROWS, t)
    assert t % tm == 0 and tm % CHUNK == 0
    rows = lambda w: pl.BlockSpec((tm, w), lambda i: (i, 0))
    out = lambda w, dt: jax.ShapeDtypeStruct((t, w), dt)
    cast_src, cast_dst, cast_shapes = _cast_specs(cast, layer, t // tm)
    return pl.pallas_call(
        functools.partial(_in_proj_kernel, layer, len(cast)),
        grid=(t // tm,),
        in_specs=[rows(D_MODEL),
                  _resident((D_MODEL, IN_COLS), lambda i: (0, 0)),
                  _resident((None, 1, GATE_W), lambda i: (layer, 0, 0)),
                  _resident(logits.shape, lambda i: (0, 0))] + cast_src,
        out_specs=[rows(ATT_W), rows(2 * KV_W), rows(4 * KV_W), rows(4 * KV_W),
                   rows(HG_W), rows(HG_W), rows(HG_W), rows(HG_W), rows(HG_W),
                   rows(GATE_W), pl.BlockSpec((tm // CHUNK, 1, 128), lambda i: (i, 0, 0))] + cast_dst,
        out_shape=[out(ATT_W, BF16), out(2 * KV_W, F32), out(4 * KV_W, BF16), out(4 * KV_W, BF16),
                   out(HG_W, BF16), out(HG_W, BF16), out(HG_W, F32), out(HG_W, BF16), out(HG_W, F32),
                   out(GATE_W, F32),
                   jax.ShapeDtypeStruct((t // CHUNK, 1, 128), jnp.int32)] + cast_shapes,
        scratch_shapes=[pltpu.VMEM((tm, D_MODEL), BF16)],
        compiler_params=_cparams("parallel"),
        name="in_proj",
    )(x, w_b, b_gate, logits, *cast)


def _attn_block(n, sink_ref, bias_ref, first, q_ref, kp_ref, kc_ref, vp_ref, vc_ref, o_ref):
    r = WINDOW
    top = lax.broadcasted_iota(jnp.int32, (2 * r, 1), 0) < r
    combos = [(g, half) for g in range(KV_HEADS) for half in range(2)]
    lanes = lambda g, half: slice((2 * g + half) * KV_W, (2 * g + half + 1) * KV_W)
    sink = [jnp.where(top, sink_ref[0, 4 * g + half], sink_ref[0, 4 * g + 2 + half]) * LOG2E for g, half in combos]
    own = slice(n * r, (n + 1) * r)

    def kv_rows(prev_ref, cur_ref, g, half):
        before = prev_ref[:, lanes(g, half)] if n == 0 else cur_ref[(n - 1) * r:n * r, lanes(g, half)]
        return jnp.concatenate([before, cur_ref[own, lanes(g, half)]], axis=0)

    qg = [jnp.concatenate([q_ref[own, 2 * g * 128:(2 * g + 1) * 128],
                           q_ref[own, (2 * g + 1) * 128:(2 * g + 2) * 128]], axis=0) for g in range(KV_HEADS)]
    b = bias_ref[first] if n == 0 else bias_ref[1]
    bias = jnp.concatenate([b, b], axis=0)
    s = [lax.dot_general(qg[g], kv_rows(kp_ref, kc_ref, g, half), (((1,), (1,)), ((), ())),
                         preferred_element_type=F32) + bias for g, half in combos]
    yield
    m = [jnp.maximum(jnp.max(t, axis=-1, keepdims=True), sk) for t, sk in zip(s, sink)]
    p = [jnp.exp2(t - mx) for t, mx in zip(s, m)]
    yield
    den = [jnp.sum(t, axis=-1, keepdims=True) + jnp.exp2(sk - mx) for t, sk, mx in zip(p, sink, m)]
    o = [jnp.dot(t.astype(BF16), kv_rows(vp_ref, vc_ref, g, half), preferred_element_type=F32) / d
         for t, d, (g, half) in zip(p, den, combos)]
    yield
    cols = []
    for g in range(KV_HEADS):
        acc = o[2 * g] + o[2 * g + 1]
        cols += [acc[:r], acc[r:]]
    o_ref[own, :] = jnp.concatenate(cols, axis=1).astype(o_ref.dtype)


def _band_bias():
    qi = lax.broadcasted_iota(jnp.int32, (WINDOW, 2 * WINDOW), 0)
    kj = lax.broadcasted_iota(jnp.int32, (WINDOW, 2 * WINDOW), 1)
    band = jnp.logical_and(kj >= qi, kj <= qi + WINDOW)
    first = jnp.logical_and(band, kj >= WINDOW)
    return jnp.where(jnp.stack([first, band]), 0.0, NEG_BIG).astype(F32)


def _attn_prompt_kernel(sink_ref, bias_ref, q_ref, kp_ref, kc_ref, vp_ref, vc_ref, o_ref):
    first = jnp.minimum(pl.program_id(1), 1)
    _interleave(_attn_block(n, sink_ref, bias_ref, first, q_ref, kp_ref, kc_ref, vp_ref, vc_ref, o_ref)
                for n in range(ATT_BLOCKS))


def _attn_prompt(q, ks, vs, sink, batch, seq):
    steps = seq // (ATT_BLOCKS * WINDOW)
    assert seq % (ATT_BLOCKS * WINDOW) == 0
    cur = lambda b, i: (b * steps + i, 0)
    prev = lambda b, i: ((b * steps + i) * ATT_BLOCKS - jnp.minimum(i, 1), 0)
    prev_spec = pl.BlockSpec((WINDOW, 4 * KV_W), prev)
    cur_spec = pl.BlockSpec((ATT_BLOCKS * WINDOW, 4 * KV_W), cur)
    return pl.pallas_call(
        _attn_prompt_kernel,
        grid=(batch, steps),
        in_specs=[pl.BlockSpec(memory_space=pltpu.SMEM),
                  _resident((2, WINDOW, 2 * WINDOW), lambda b, i: (0, 0, 0)),
                  pl.BlockSpec((ATT_BLOCKS * WINDOW, ATT_W), cur),
                  prev_spec, cur_spec, prev_spec, cur_spec],
        out_specs=pl.BlockSpec((ATT_BLOCKS * WINDOW, ATT_W), cur),
        out_shape=jax.ShapeDtypeStruct((batch * seq, ATT_W), BF16),
        compiler_params=_cparams("parallel", "parallel"),
        name="attn_prompt",
    )(sink, _band_bias(), q, ks, ks, vs, vs)


def _interleave(stage_generators):
    pending = list(stage_generators)
    while pending:
        for gen in list(pending):
            if next(gen, StopIteration) is StopIteration:
                pending.remove(gen)


def _attn_sample_kernel(sink_ref, q_ref, kvn_ref, ck_ref, cv_ref, *rest):
    o_ref, nk_ref, nv_ref = rest[-3:]
    _interleave(_attn_sample_group(grp, sink_ref, q_ref, kvn_ref, ck_ref, cv_ref, o_ref, nk_ref, nv_ref)
                for grp in range(SAMPLE_GROUPS))


def _attn_sample_group(grp, sink_ref, q_ref, kvn_ref, ck_ref, cv_ref, o_ref, nk_ref, nv_ref):
    rows = slice(grp * ROWS_PER_STEP, (grp + 1) * ROWS_PER_STEP)
    ck_ref, cv_ref, nk_ref, nv_ref = (r.at[grp * SEQ_PER_STEP:(grp + 1) * SEQ_PER_STEP]
                                      for r in (ck_ref, cv_ref, nk_ref, nv_ref))
    lane = lax.broadcasted_iota(jnp.int32, (ROWS_PER_STEP, KV_W), 1)
    q16 = q_ref[rows, :].astype(F32)
    blocks = []
    for h in range(N_HEADS):
        c, half, g = h // 2, h % 2, h // (N_HEADS // KV_HEADS)
        own = lane >= HEAD_DIM if half else lane < HEAD_DIM
        qh = jnp.where(own, q16[:, c * 128:(c + 1) * 128], 0.0)
        blocks.append(qh if half == g else pltpu.roll(qh, HEAD_DIM, 1))
    wt = jnp.concatenate(blocks, axis=0).astype(BF16)

    row = lax.broadcasted_iota(jnp.int32, (WINDOW, WINDOW), 0)
    col = lax.broadcasted_iota(jnp.int32, (WINDOW, WINDOW), 1)
    tok = jnp.bitwise_and(row, DEC_SEQ - 1)
    row_seq = jnp.bitwise_and(jnp.right_shift(row, 2), SEQ_PER_STEP - 1)
    pad = jnp.zeros((WINDOW - ROWS_PER_STEP, KV_W), F32)
    kn = jnp.concatenate([kvn_ref[rows, :KV_W], pad], axis=0)
    vn = jnp.concatenate([kvn_ref[rows, KV_W:], pad], axis=0)
    kn_t = jnp.transpose(kn)
    vn_t = jnp.transpose(vn)
    yield

    s_c = jnp.zeros((WINDOW, WINDOW), F32)
    for j in range(SEQ_PER_STEP):
        sj = jnp.dot(wt, ck_ref[j].astype(BF16), preferred_element_type=F32)
        s_c = jnp.where(row_seq == j, sj, s_c)
    s_n = jnp.dot(wt, kn_t.astype(BF16), preferred_element_type=F32)
    yield
    s_c = jnp.where(col >= tok, s_c, NEG_BIG)
    new_ok = jnp.logical_and(jnp.right_shift(col, 2) == row_seq, jnp.bitwise_and(col, DEC_SEQ - 1) <= tok)
    s_n = jnp.where(new_ok, s_n, NEG_BIG)
    head = jnp.right_shift(lax.broadcasted_iota(jnp.int32, (WINDOW, 1), 0), 4)
    sink = jnp.zeros((WINDOW, 1), F32)
    for h in range(N_HEADS):
        sink = jnp.where(head == h, sink_ref[0, h] * LOG2E, sink)
    m = jnp.maximum(jnp.maximum(jnp.max(s_c, axis=-1, keepdims=True), jnp.max(s_n, axis=-1, keepdims=True)), sink)
    p_c = jnp.exp2(s_c - m)
    p_n = jnp.exp2(s_n - m)
    den = jnp.sum(p_c, axis=-1, keepdims=True) + jnp.sum(p_n, axis=-1, keepdims=True) + jnp.exp2(sink - m)
    p_cb = p_c.astype(BF16)
    yield
    o = jnp.dot(p_n.astype(BF16), vn.astype(BF16), preferred_element_type=F32)
    for j in range(SEQ_PER_STEP):
        oj = lax.dot_general(p_cb, cv_ref[j].astype(BF16), (((1,), (1,)), ((), ())), preferred_element_type=F32)
        o = o + jnp.where(row_seq == j, oj, 0.0)
    yield
    o = o / den
    outs = []
    for c in range(4):
        lo, hi = o[32 * c:32 * c + 16], o[32 * c + 16:32 * c + 32]
        if c < 2:
            outs.append(jnp.where(lane < HEAD_DIM, lo, pltpu.roll(hi, HEAD_DIM, 1)))
        else:
            outs.append(jnp.where(lane < HEAD_DIM, pltpu.roll(lo, HEAD_DIM, 1), hi))
    o_ref[rows, :] = jnp.concatenate(outs, axis=1).astype(o_ref.dtype)
    yield
    for j in range(SEQ_PER_STEP):
        shift = (WINDOW - DEC_SEQ - DEC_SEQ * j) % WINDOW
        for new_t, cache_ref, dst in ((kn_t, ck_ref, nk_ref), (vn_t, cv_ref, nv_ref)):
            dst[j] = jnp.where(col >= WINDOW - DEC_SEQ, pltpu.roll(new_t, shift, 1),
                               pltpu.roll(cache_ref[j], WINDOW - DEC_SEQ, 1))


def _attn_sample(q, kvn, sink, cache_kt, cache_vt, layer, prev_k, prev_v):
    depth, db = cache_kt.shape[:2]
    seqs = SAMPLE_GROUPS * SEQ_PER_STEP
    cache_spec = pl.BlockSpec((None, seqs, KV_W, WINDOW), lambda s: (layer, s, 0, 0))
    rows = lambda w: pl.BlockSpec((seqs * DEC_SEQ, w), lambda s: (s, 0))
    n_in = 5

    def call(extra, aliases):
        return pl.pallas_call(
            _attn_sample_kernel,
            grid=(db // seqs,),
            in_specs=[pl.BlockSpec(memory_space=pltpu.SMEM), rows(ATT_W), rows(2 * KV_W), cache_spec, cache_spec]
            + [pl.BlockSpec(memory_space=pl.ANY)] * len(extra),
            out_specs=[rows(ATT_W), cache_spec, cache_spec],
            out_shape=[jax.ShapeDtypeStruct((db * DEC_SEQ, ATT_W), BF16),
                       jax.ShapeDtypeStruct((depth, db, KV_W, WINDOW), F32),
                       jax.ShapeDtypeStruct((depth, db, KV_W, WINDOW), F32)],
            input_output_aliases=aliases,
            compiler_params=_cparams("parallel"),
            name="attn_sample",
        )(sink, q, kvn, cache_kt, cache_vt, *extra)

    return _stacked_out(call, layer, {1: prev_k, 2: prev_v}, n_in)


def _hgrn_out(o, g_act, normw):
    ms = jnp.mean(o * o, axis=-1, keepdims=True)
    return o * lax.rsqrt(ms + RMS_EPS) * normw * g_act


def _cumsum_rows(x, tri):
    hi = x.astype(BF16)
    r1 = x - hi.astype(F32)
    mid = r1.astype(BF16)
    lo = (r1 - mid.astype(F32)).astype(BF16)
    dot = lambda t: jnp.dot(tri, t, preferred_element_type=F32)
    return dot(hi) + dot(mid) + dot(lo)


def _heads(t):
    return [t[:, h * HG_D:(h + 1) * HG_D] for h in range(HG_HEADS)]


def _level_reference(b, n, row):
    if n == 2:
        return jnp.where(jnp.bitwise_and(row, 1) == 0, b, pltpu.roll(b, 1, 0))
    if n == 4:
        m4 = jnp.bitwise_and(row, 3)
        return jnp.where(m4 == 0, pltpu.roll(b, CHUNK - 1, 0),
                         jnp.where(m4 == 1, b, jnp.where(m4 == 2, pltpu.roll(b, 1, 0), pltpu.roll(b, 2, 0))))
    bounds = [m * n + n // 2 - 1 for m in range(CHUNK // n)]
    pieces = [jnp.broadcast_to(b[t:t + 1], (n, b.shape[1])) for t in bounds]
    return pieces[0] if len(pieces) == 1 else jnp.concatenate(pieces, axis=0)


def _scores(q, k, e_q, e_k):
    qs, ks = _heads(q * e_q.astype(BF16)), _heads(k * e_k.astype(BF16))
    return [lax.dot_general(a, c, (((1,), (1,)), ((), ())), preferred_element_type=F32) for a, c in zip(qs, ks)]


def _intra_chunk_scores(q, k, b, lvl, fast):
    if fast:
        b_mid, b_end = b[HALF - 1:HALF], b[CHUNK - 1:CHUNK]
        r = jnp.concatenate([jnp.broadcast_to(0.5 * b_mid, (HALF, HG_W)),
                             jnp.broadcast_to(0.5 * (b_mid + b_end), (HALF, HG_W))], axis=0)
        e = jnp.exp(b - r)
        e6 = jnp.exp(-jnp.abs(b - b_mid))
        return [jnp.where(lvl == 6, across, jnp.where(lvl >= -1, within, 0.0))
                for within, across in zip(_scores(q, k, e, 1.0 / e), _scores(q, k, e6, e6))]
    row = lax.broadcasted_iota(jnp.int32, (CHUNK, HG_W), 0)
    atts = [jnp.where(lvl == -1, jnp.sum(qk, axis=-1, keepdims=True), 0.0) for qk in _heads((q * k).astype(F32))]
    for level in range(7):
        e = jnp.exp(-jnp.abs(b - _level_reference(b, 2 << level, row)))
        atts = [jnp.where(lvl == level, p, att) for p, att in zip(_scores(q, k, e, e), atts)]
    return atts


def _hgrn_chunks(fast, hq_ref, hk_ref, lf_ref, hv_ref, hg_ref, normw_ref, o_ref, st_ref):
    row = lax.broadcasted_iota(jnp.int32, (CHUNK, CHUNK), 0)
    col = lax.broadcasted_iota(jnp.int32, (CHUNK, CHUNK), 1)
    tri = jnp.where(row >= col, 1.0, 0.0).astype(BF16)
    lvl = jnp.where(row >= col, 31 - lax.clz(jnp.bitwise_xor(row, col)), -2)
    subs = [slice(c * CHUNK, (c + 1) * CHUNK) for c in range(hq_ref.shape[0] // CHUNK)]
    q, k, v = ([ref[rs] for rs in subs] for ref in (hq_ref, hk_ref, hv_ref))
    b = [_cumsum_rows(lf_ref[rs], tri) for rs in subs]
    b_end = [t[CHUNK - 1:CHUNK] for t in b]
    atts = [_intra_chunk_scores(qc, kc, bc, lvl, fast) for qc, kc, bc in zip(q, k, b)]
    q_in = [_heads(qc * jnp.exp(bc).astype(BF16)) for qc, bc in zip(q, b)]
    k_out = [_heads(kc.astype(F32) * jnp.exp(be - bc)) for kc, bc, be in zip(k, b, b_end)]
    k_t = [[jnp.transpose(t).astype(BF16) for t in per_head] for per_head in k_out]
    decay = [[jnp.transpose(jnp.broadcast_to(t, (HG_D, HG_D))) for t in _heads(jnp.exp(be))] for be in b_end]
    vs = [_heads(vc) for vc in v]
    local = [[jnp.dot(a.astype(BF16), vh, preferred_element_type=F32) for a, vh in zip(ac, vc)]
             for ac, vc in zip(atts, vs)]
    grow = [[jnp.dot(kt, vh, preferred_element_type=F32) for kt, vh in zip(kc, vc)] for kc, vc in zip(k_t, vs)]
    state = [st_ref[0, h] for h in range(HG_HEADS)]
    for c, rs in enumerate(subs):
        gs = _heads(hg_ref[rs])
        outs = [_hgrn_out(jnp.dot(q_in[c][h], state[h].astype(BF16), preferred_element_type=F32) + local[c][h],
                          gs[h], normw_ref[...]) for h in range(HG_HEADS)]
        o_ref[rs] = jnp.concatenate(outs, axis=1).astype(o_ref.dtype)
        state = [decay[c][h] * state[h] + grow[c][h] for h in range(HG_HEADS)]
    for h in range(HG_HEADS):
        st_ref[0, h] = state[h]


def _hgrn_prompt_kernel(ok_ref, *refs):
    st_ref = refs[-1]

    @pl.when(pl.program_id(1) == 0)
    def _():
        st_ref[...] = jnp.zeros_like(st_ref)

    first = (pl.program_id(0) * pl.num_programs(1) + pl.program_id(1)) * HGRN_CHUNKS
    ok = ok_ref[first] != 0
    for c in range(1, HGRN_CHUNKS):
        ok = jnp.logical_and(ok, ok_ref[first + c] != 0)

    @pl.when(ok)
    def _():
        _hgrn_chunks(True, *refs)

    @pl.when(jnp.logical_not(ok))
    def _():
        _hgrn_chunks(False, *refs)


def _hgrn_prompt(chunk_ok, hq, hk, lf, hv, hg, normw, batch, seq):
    nc = seq // (HGRN_CHUNKS * CHUNK)
    assert seq % (HGRN_CHUNKS * CHUNK) == 0
    rows = pl.BlockSpec((HGRN_CHUNKS * CHUNK, HG_W), lambda b, i, ok: (b * nc + i, 0))
    return pl.pallas_call(
        _hgrn_prompt_kernel,
        grid_spec=pltpu.PrefetchScalarGridSpec(
            num_scalar_prefetch=1,
            grid=(batch, nc),
            in_specs=[rows, rows, rows, rows, rows, pl.BlockSpec((1, HG_D), lambda b, i, ok: (0, 0))],
            out_specs=[rows, pl.BlockSpec((1, HG_HEADS, HG_D, HG_D), lambda b, i, ok: (b, 0, 0, 0))]),
        out_shape=[jax.ShapeDtypeStruct((batch * seq, HG_W), BF16),
                   jax.ShapeDtypeStruct((batch, HG_HEADS, HG_D, HG_D), F32)],
        compiler_params=_cparams("parallel", "arbitrary"),
        name="hgrn_prompt",
    )(chunk_ok, hq, hk, lf, hv, hg, normw)


def _hgrn_sample_kernel(hq_ref, hk_ref, lf_ref, hv_ref, hg_ref, normw_ref, st_ref, *rest):
    o_ref, ns_ref = rest[-2:]
    _interleave(_hgrn_sample_group(grp, hq_ref, hk_ref, lf_ref, hv_ref, hg_ref, normw_ref, st_ref, o_ref, ns_ref)
                for grp in range(SAMPLE_GROUPS))


def _hgrn_sample_group(grp, hq_ref, hk_ref, lf_ref, hv_ref, hg_ref, normw_ref, st_ref, o_ref, ns_ref):
    rows = slice(grp * ROWS_PER_STEP, (grp + 1) * ROWS_PER_STEP)
    st_ref, ns_ref = (r.at[grp * SEQ_PER_STEP:(grp + 1) * SEQ_PER_STEP] for r in (st_ref, ns_ref))
    q, k, v = (r[rows, :].astype(F32) for r in (hq_ref, hk_ref, hv_ref))
    logf = lf_ref[rows, :]
    row = lax.broadcasted_iota(jnp.int32, (ROWS_PER_STEP, HG_W), 0)
    tok = jnp.bitwise_and(row, DEC_SEQ - 1)
    seq_of_row = jnp.right_shift(row[:, :HG_D], 2)
    down = lambda t, d: pltpu.roll(t, d, 0)
    up = lambda t, d: pltpu.roll(t, ROWS_PER_STEP - d, 0)
    b = logf
    for d in range(1, DEC_SEQ):
        b = b + jnp.where(tok >= d, down(logf, d), 0.0)
    b_end = jnp.where(tok == 3, b, jnp.where(tok == 2, up(b, 1), jnp.where(tok == 1, up(b, 2), up(b, 3))))
    qe = (q * jnp.exp(b)).astype(BF16)
    kd = k * jnp.exp(b_end - b)
    e_end = jnp.exp(b_end)
    intra = [jnp.zeros((ROWS_PER_STEP, HG_D), F32) for _ in range(HG_HEADS)]
    for d in range(DEC_SEQ):
        kd_, bd_, vd_ = (k, b, v) if d == 0 else (down(k, d), down(b, d), down(v, d))
        w = jnp.where(tok >= d, q * kd_ * jnp.exp(jnp.minimum(b - bd_, 0.0)), 0.0)
        for h in range(HG_HEADS):
            sl = slice(h * HG_D, (h + 1) * HG_D)
            intra[h] = intra[h] + jnp.sum(w[:, sl], axis=-1, keepdims=True) * vd_[:, sl]
    pad = jnp.zeros((HG_D - 2 * ROWS_PER_STEP, HG_D), F32)
    colh = lax.broadcasted_iota(jnp.int32, (HG_D, HG_D), 1)
    normw = normw_ref[...]
    yield
    for h in range(HG_HEADS):
        sl = slice(h * HG_D, (h + 1) * HG_D)
        vb = jnp.concatenate([v[:, sl], jnp.zeros((HG_D - ROWS_PER_STEP, HG_D), F32)], axis=0).astype(BF16)
        tt = jnp.transpose(jnp.concatenate([kd[:, sl], e_end[:, sl], pad], axis=0))
        inter = jnp.zeros((ROWS_PER_STEP, HG_D), F32)
        for j in range(SEQ_PER_STEP):
            s0 = st_ref[j, h]
            oi = jnp.dot(qe[:, sl], s0.astype(BF16), preferred_element_type=F32)
            inter = jnp.where(seq_of_row == j, oi, inter)
            kd_t = jnp.where(jnp.right_shift(colh, 2) == j, tt, 0.0).astype(BF16)
            e_col = ROWS_PER_STEP + DEC_SEQ * j
            decay = jnp.broadcast_to(tt[:, e_col:e_col + 1], (HG_D, HG_D))
            ns_ref[j, h] = decay * s0 + jnp.dot(kd_t, vb, preferred_element_type=F32)
        o_ref[rows, sl] = _hgrn_out(inter + intra[h], hg_ref[rows, sl], normw).astype(o_ref.dtype)
        yield


def _hgrn_sample(hq, hk, lf, hv, hg, normw, state, layer, prev_state):
    depth, db = state.shape[:2]
    seqs = SAMPLE_GROUPS * SEQ_PER_STEP
    rows = pl.BlockSpec((seqs * DEC_SEQ, HG_W), lambda s: (s, 0))
    st_spec = pl.BlockSpec((None, seqs, HG_HEADS, HG_D, HG_D), lambda s: (layer, s, 0, 0, 0))
    n_in = 7

    def call(extra, aliases):
        return pl.pallas_call(
            _hgrn_sample_kernel,
            grid=(db // seqs,),
            in_specs=[rows, rows, rows, rows, rows, pl.BlockSpec((1, HG_D), lambda s: (0, 0)), st_spec]
            + [pl.BlockSpec(memory_space=pl.ANY)] * len(extra),
            out_specs=[rows, st_spec],
            out_shape=[jax.ShapeDtypeStruct((db * DEC_SEQ, HG_W), BF16), jax.ShapeDtypeStruct(state.shape, F32)],
            input_output_aliases=aliases,
            compiler_params=_cparams("parallel"),
            name="hgrn_sample",
        )(hq, hk, lf, hv, hg, normw, state, *extra)

    return _stacked_out(call, layer, {1: prev_state}, n_in)


def _layer_norm(y, g, b):
    mu = jnp.mean(y, axis=-1, keepdims=True)
    yc = y - mu
    var = jnp.mean(yc * yc, axis=-1, keepdims=True)
    return yc * lax.rsqrt(var + LN_EPS) * g + b


def _post_kernel(alpha, n_cast, a_ref, h_ref, g_ref, x_ref, wua_ref, wuh_ref, wo_ref, l1g_ref, l1b_ref,
                 wf1_ref, wf2_ref, l2g_ref, l2b_ref, *refs):
    o_ref = refs[n_cast]
    _cast_rows(refs[:n_cast], refs[n_cast + 1:])
    groups = [slice(r0, r0 + POST_GROUP) for r0 in range(0, x_ref.shape[0], POST_GROUP)]
    dot = lambda lhs, w_ref: jnp.dot(lhs, w_ref[...], preferred_element_type=F32)
    ua = [dot(a_ref[rs], wua_ref) for rs in groups]
    uh = [dot(h_ref[rs], wuh_ref) for rs in groups]
    gate = jax.nn.sigmoid
    merged = [(gate(g_ref[rs, :D_MODEL]) * a + gate(g_ref[rs, D_MODEL:]) * h).astype(BF16)
              for rs, a, h in zip(groups, ua, uh)]
    m = [dot(t, wo_ref) for t in merged]
    x1 = [_layer_norm(alpha * x_ref[rs] + t, l1g_ref[...], l1b_ref[...]) for rs, t in zip(groups, m)]
    hid = [jnp.square(jnp.maximum(dot(t.astype(BF16), wf1_ref), 0.0)).astype(BF16) for t in x1]
    ff = [dot(t, wf2_ref) for t in hid]
    for rs, t, f in zip(groups, x1, ff):
        o_ref[rs] = _layer_norm(alpha * t + f, l2g_ref[...], l2b_ref[...])


def _post(a, h, g, x, wb, p, layer, alpha, cast=(), cast_layer=0):
    t = x.shape[0]
    tm = min(POST_ROWS, t)
    assert t % tm == 0
    rows = lambda w: pl.BlockSpec((tm, w), lambda i: (i, 0))
    vec = lambda w: _resident((None, 1, w), lambda i: (layer, 0, 0))
    mat = lambda w: _resident(w.shape, lambda i: (0, 0))
    cast_src, cast_dst, cast_shapes = _cast_specs(cast, cast_layer, t // tm)
    outs = pl.pallas_call(
        functools.partial(_post_kernel, alpha, len(cast)),
        grid=(t // tm,),
        in_specs=[rows(ATT_W), rows(HG_W), rows(GATE_W), rows(D_MODEL), mat(wb[0]), mat(wb[1]), mat(wb[2]),
                  vec(D_MODEL), vec(D_MODEL), mat(wb[3]), mat(wb[4]), vec(D_MODEL), vec(D_MODEL)] + cast_src,
        out_specs=[rows(D_MODEL)] + cast_dst,
        out_shape=[jax.ShapeDtypeStruct((t, D_MODEL), F32)] + cast_shapes,
        compiler_params=_cparams("parallel"),
        name="post",
    )(a, h, g, x, wb[0], wb[1], wb[2], p["ln1_g"], p["ln1_b"], wb[3], wb[4], p["ln2_g"], p["ln2_b"], *cast)
    return outs[0], outs[1:]


def kernel(x_prompt, x_sample, cache_k, cache_v, state_hgrn, w_in, b_gate, attn_sink, hgrn_lb_logits, hgrn_norm_w,
           w_up_attn, w_up_hgrn, w_out, ln1_g, ln1_b, w_ff1, w_ff2, ln2_g, ln2_b):
    depth = w_in.shape[0]
    batch, seq, _ = x_prompt.shape
    db, dec_seq, _ = x_sample.shape
    assert dec_seq == DEC_SEQ and seq % CHUNK == 0 and db % (SAMPLE_GROUPS * SEQ_PER_STEP) == 0
    assert cache_k.shape[2:] == (WINDOW, KV_HEADS, HEAD_DIM)
    alpha = (2 * depth) ** 0.25

    vec = lambda t: t.reshape(depth, 1, t.shape[-1])
    b_gate3 = vec(b_gate)
    p = dict(ln1_g=vec(ln1_g), ln1_b=vec(ln1_b), ln2_g=vec(ln2_g), ln2_b=vec(ln2_b))
    mixer_w = (w_up_attn, w_up_hgrn, w_out, w_ff1, w_ff2)
    w_in_b = w_in[0].astype(BF16)
    to_t = lambda c: jnp.transpose(c, (0, 1, 3, 4, 2)).reshape(depth, db, KV_W, WINDOW)
    from_t = lambda c: jnp.transpose(c.reshape(depth, db, KV_HEADS, HEAD_DIM, WINDOW), (0, 1, 4, 2, 3))
    ck_t, cv_t = to_t(cache_k), to_t(cache_v)

    xp = x_prompt.reshape(batch * seq, D_MODEL)
    xs = x_sample.reshape(db * DEC_SEQ, D_MODEL)
    pk, pv, ps = [], [], []
    sk = sv = ss = None
    for l in range(depth):
        sink = attn_sink[l].reshape(1, N_HEADS)
        normw = hgrn_norm_w[l].reshape(1, HG_D)
        q, kv, ks, vs, hq, hk, lf, hv, hg, g, chunk_ok, *wb = _in_proj(xp, w_in_b, b_gate3, hgrn_lb_logits, l,
                                                                       cast=mixer_w)
        a = _attn_prompt(q, ks, vs, sink, batch, seq)
        h, st = _hgrn_prompt(chunk_ok[:, 0, 0], hq, hk, lf, hv, hg, normw, batch, seq)
        xp, next_w_in = _post(a, h, g, xp, wb, p, l, alpha, cast=(w_in,) if l + 1 < depth else (), cast_layer=l + 1)
        win = kv.reshape(batch, seq, 2 * KV_W)[:, seq - WINDOW:].reshape(batch, WINDOW, 2, KV_HEADS, HEAD_DIM)
        pk.append(win[:, :, 0])
        pv.append(win[:, :, 1])
        ps.append(st)
        q, kvn, _, _, hq, hk, lf, hv, hg, g, _ = _in_proj(xs, w_in_b, b_gate3, hgrn_lb_logits, l)
        a, sk, sv = _attn_sample(q, kvn, sink, ck_t, cv_t, l, sk, sv)
        h, ss = _hgrn_sample(hq, hk, lf, hv, hg, normw, state_hgrn, l, ss)
        xs, _ = _post(a, h, g, xs, wb, p, l, alpha)
        if next_w_in:
            w_in_b = next_w_in[0]
    return (xp.reshape(batch, seq, D_MODEL), xs.reshape(db, DEC_SEQ, D_MODEL), jnp.stack(pk), jnp.stack(pv),
            jnp.stack(ps), from_t(sk), from_t(sv), ss)
```

```python
import functools

import jax
import jax.numpy as jnp
from jax import lax
from jax.experimental import pallas as pl
from jax.experimental.pallas import tpu as pltpu

F32 = jnp.float32
BF16 = jnp.bfloat16

D_MODEL = 1024
WINDOW = 128
N_HEADS = 8
KV_HEADS = 2
HEAD_DIM = 64
ATT_W = N_HEADS * HEAD_DIM
KV_W = KV_HEADS * HEAD_DIM
HG_HEADS = 4
HG_D = 128
HG_W = HG_HEADS * HG_D
GATE_W = 2 * D_MODEL
D_FF = 4 * D_MODEL
DEC_SEQ = 4
LN_EPS = 1e-5
RMS_EPS = 1e-6
NEG_BIG = -1e30
LB_FLOOR = 1e-30
LOG2E = 1.4426950408889634
QK_SCALE = HEAD_DIM ** -0.5 * LOG2E

AQ0, AK0, AV0, HQ0, HF0, HI0, HG0, GT0, IN_COLS = 0, 512, 640, 768, 1280, 1792, 2304, 2816, 4864

ATT_BLOCKS = 4
CHUNK = 128
HGRN_CHUNKS = 4
HALF = CHUNK // 2
FAST_DECAY_LIMIT = 120.0
FAST_Q_LIMIT = 1e9
SEQ_PER_STEP = 4
ROWS_PER_STEP = SEQ_PER_STEP * DEC_SEQ
SAMPLE_GROUPS = 4
IN_PROJ_ROWS = 512
POST_ROWS = 512
POST_GROUP = 256
V7X_VMEM_LIMIT = 56 * 1024 * 1024


def _cparams(*sem):
    return pltpu.CompilerParams(dimension_semantics=sem, vmem_limit_bytes=V7X_VMEM_LIMIT)


def _resident(shape, index_map):
    return pl.BlockSpec(shape, index_map, pipeline_mode=pl.Buffered(1))


def _stacked_out(call, layer, stacked, n_in):
    extra, aliases = [], {}
    if layer > 0:
        for out_idx, prev in stacked.items():
            aliases[n_in + len(extra)] = out_idx
            extra.append(prev)
    return call(extra, aliases)


def _lower_bound(logits_ref, layer):
    lg = logits_ref[...]
    e = jnp.exp(lg - jnp.max(lg, axis=0, keepdims=True))
    p = e / jnp.sum(e, axis=0, keepdims=True)
    lb = jnp.zeros((1, HG_W), F32)
    for i in range(1, layer + 1):
        lb = lb + p[i:i + 1]
    return lb


def _swish(z):
    return z * jax.nn.sigmoid(z)


def _forget_and_key(zf, lb):
    e = jnp.exp(-jnp.abs(zf))
    r = 1.0 / (1.0 + e)
    pos = zf >= 0
    sig_p = jnp.where(pos, r, e * r)
    sig_n = jnp.where(pos, e * r, r)
    logf = jnp.log(jnp.maximum(lb, LB_FLOOR) + (1.0 - lb) * sig_p)
    return logf, (1.0 - lb) * sig_n


def _cast_rows(src_refs, dst_refs):
    for src, dst in zip(src_refs, dst_refs):
        dst[...] = src[...].astype(BF16)


def _in_proj_kernel(layer, n_cast, x_ref, w_ref, bg_ref, logits_ref, *refs):
    cast_src, refs = refs[:n_cast], refs[n_cast:]
    outs, refs = refs[:11], refs[11:]
    cast_dst, xb_ref = refs[:n_cast], refs[n_cast]
    _cast_rows(cast_src, cast_dst)
    for _ in _in_proj_stages(layer, x_ref, w_ref, bg_ref, logits_ref, outs, xb_ref):
        pass


def _in_proj_stages(layer, x_ref, w_ref, bg_ref, logits_ref, outs, xb_ref):
    q_ref, kv_ref, ks_ref, vs_ref, hq_ref, hk_ref, lf_ref, hv_ref, hg_ref, g_ref, ok_ref = outs
    xb = x_ref[...].astype(BF16)
    lb = _lower_bound(logits_ref, layer)
    chunks = x_ref.shape[0] // CHUNK
    q_ok = []
    decay_ok = []

    def put_q(z):
        q_ref[...] = (z * QK_SCALE).astype(BF16)

    def put_kv(z):
        kv_ref[...] = z
        ks_ref[...] = _split_heads(z[:, :KV_W])
        vs_ref[...] = _split_heads(z[:, KV_W:])

    def put_hq(z):
        hq = _swish(z)
        hq_ref[...] = hq.astype(BF16)
        for c in range(chunks):
            q_ok.append(jnp.max(jnp.abs(hq[c * CHUNK:(c + 1) * CHUNK]), keepdims=True) <= FAST_Q_LIMIT)

    def put_hf(z):
        logf, key = _forget_and_key(z, lb)
        lf_ref[...] = logf
        hk_ref[...] = key.astype(BF16)
        for lo in range(0, chunks * CHUNK, HALF):
            total = jnp.sum(logf[lo:lo + HALF], axis=0, keepdims=True)
            decay_ok.append(jnp.min(total, keepdims=True) >= -FAST_DECAY_LIMIT)

    def put_hv(z):
        hv_ref[...] = z.astype(BF16)

    def put_hg(z):
        hg_ref[...] = _swish(z)

    def put_gates(lo):
        def put(z):
            g_ref[:, lo:lo + 512] = z + bg_ref[:, lo:lo + 512]
        return put

    gates = [(GT0 + lo, GT0 + lo + 512, put_gates(lo)) for lo in range(0, GATE_W, 512)]
    stages = [(HF0, HI0, put_hf), gates[0], (HQ0, HF0, put_hq), gates[1], (AK0, HQ0, put_kv), gates[2],
              (HG0, GT0, put_hg), (AQ0, AK0, put_q), (HI0, HG0, put_hv), gates[3]]
    pending = None
    xb_ref[...] = xb
    for lo, hi, put in stages:
        z = jnp.dot(xb_ref[...], w_ref[:, lo:hi], preferred_element_type=F32)
        if pending is not None:
            pending[0](pending[1])
        pending = (put, z)
        yield
    pending[0](pending[1])
    for c in range(chunks):
        ok = jnp.logical_and(q_ok[c], jnp.logical_and(decay_ok[2 * c], decay_ok[2 * c + 1]))
        ok_ref[c] = jnp.broadcast_to(jnp.where(ok, 1, 0), (1, 128)).astype(jnp.int32)


def _split_heads(t):
    low = lax.broadcasted_iota(jnp.int32, t.shape, 1) < HEAD_DIM
    h0_lo = jnp.where(low, t, 0.0)
    h1_hi = jnp.where(low, 0.0, t)
    blocks = [h0_lo, pltpu.roll(h0_lo, HEAD_DIM, 1), pltpu.roll(h1_hi, HEAD_DIM, 1), h1_hi]
    return jnp.concatenate(blocks, axis=1).astype(BF16)


def _cast_specs(weights, layer, steps):
    src, dst, shapes = [], [], []
    for w in weights:
        _, r, c = w.shape
        slab = r // steps
        assert r % steps == 0 and (slab % 16 == 0 or steps == 1)
        last = steps - 1
        src.append(pl.BlockSpec((None, slab, c), lambda i, layer=layer, last=last: (layer, jnp.minimum(i, last), 0)))
        dst.append(pl.BlockSpec((slab, c), lambda i, last=last: (jnp.minimum(i, last), 0)))
        shapes.append(jax.ShapeDtypeStruct((r, c), BF16))
    return src, dst, shapes


def _in_proj(x, w_b, b_gate, logits, layer, cast=()):
    t = x.shape[0]
    tm = min(IN_PROJ_ROWS, t)
    assert t % tm == 0 and tm % CHUNK == 0
    rows = lambda w: pl.BlockSpec((tm, w), lambda i: (i, 0))
    out = lambda w, dt: jax.ShapeDtypeStruct((t, w), dt)
    cast_src, cast_dst, cast_shapes = _cast_specs(cast, layer, t // tm)
    return pl.pallas_call(
        functools.partial(_in_proj_kernel, layer, len(cast)),
        grid=(t // tm,),
        in_specs=[rows(D_MODEL),
                  _resident((D_MODEL, IN_COLS), lambda i: (0, 0)),
                  _resident((None, 1, GATE_W), lambda i: (layer, 0, 0)),
                  _resident(logits.shape, lambda i: (0, 0))] + cast_src,
        out_specs=[rows(ATT_W), rows(2 * KV_W), rows(4 * KV_W), rows(4 * KV_W),
                   rows(HG_W), rows(HG_W), rows(HG_W), rows(HG_W), rows(HG_W),
                   rows(GATE_W), pl.BlockSpec((tm // CHUNK, 1, 128), lambda i: (i, 0, 0))] + cast_dst,
        out_shape=[out(ATT_W, BF16), out(2 * KV_W, F32), out(4 * KV_W, BF16), out(4 * KV_W, BF16),
                   out(HG_W, BF16), out(HG_W, BF16), out(HG_W, F32), out(HG_W, BF16), out(HG_W, F32),
                   out(GATE_W, F32),
                   jax.ShapeDtypeStruct((t // CHUNK, 1, 128), jnp.int32)] + cast_shapes,
        scratch_shapes=[pltpu.VMEM((tm, D_MODEL), BF16)],
        compiler_params=_cparams("parallel"),
        name="in_proj",
    )(x, w_b, b_gate, logits, *cast)


def _attn_block(n, sink_ref, bias_ref, first, q_ref, kp_ref, kc_ref, vp_ref, vc_ref, o_ref):
    r = WINDOW
    top = lax.broadcasted_iota(jnp.int32, (2 * r, 1), 0) < r
    combos = [(g, half) for g in range(KV_HEADS) for half in range(2)]
    lanes = lambda g, half: slice((2 * g + half) * KV_W, (2 * g + half + 1) * KV_W)
    sink = [jnp.where(top, sink_ref[0, 4 * g + half], sink_ref[0, 4 * g + 2 + half]) * LOG2E for g, half in combos]
    own = slice(n * r, (n + 1) * r)

    def kv_rows(prev_ref, cur_ref, g, half):
        before = prev_ref[:, lanes(g, half)] if n == 0 else cur_ref[(n - 1) * r:n * r, lanes(g, half)]
        return jnp.concatenate([before, cur_ref[own, lanes(g, half)]], axis=0)

    qg = [jnp.concatenate([q_ref[own, 2 * g * 128:(2 * g + 1) * 128],
                           q_ref[own, (2 * g + 1) * 128:(2 * g + 2) * 128]], axis=0) for g in range(KV_HEADS)]
    b = bias_ref[first] if n == 0 else bias_ref[1]
    bias = jnp.concatenate([b, b], axis=0)
    s = [lax.dot_general(qg[g], kv_rows(kp_ref, kc_ref, g, half), (((1,), (1,)), ((), ())),
                         preferred_element_type=F32) + bias for g, half in combos]
    yield
    m = [jnp.maximum(jnp.max(t, axis=-1, keepdims=True), sk) for t, sk in zip(s, sink)]
    p = [jnp.exp2(t - mx) for t, mx in zip(s, m)]
    yield
    den = [jnp.sum(t, axis=-1, keepdims=True) + jnp.exp2(sk - mx) for t, sk, mx in zip(p, sink, m)]
    o = [jnp.dot(t.astype(BF16), kv_rows(vp_ref, vc_ref, g, half), preferred_element_type=F32) / d
         for t, d, (g, half) in zip(p, den, combos)]
    yield
    cols = []
    for g in range(KV_HEADS):
        acc = o[2 * g] + o[2 * g + 1]
        cols += [acc[:r], acc[r:]]
    o_ref[own, :] = jnp.concatenate(cols, axis=1).astype(o_ref.dtype)


def _band_bias():
    qi = lax.broadcasted_iota(jnp.int32, (WINDOW, 2 * WINDOW), 0)
    kj = lax.broadcasted_iota(jnp.int32, (WINDOW, 2 * WINDOW), 1)
    band = jnp.logical_and(kj >= qi, kj <= qi + WINDOW)
    first = jnp.logical_and(band, kj >= WINDOW)
    return jnp.where(jnp.stack([first, band]), 0.0, NEG_BIG).astype(F32)


def _attn_prompt_kernel(sink_ref, bias_ref, q_ref, kp_ref, kc_ref, vp_ref, vc_ref, o_ref):
    first = jnp.minimum(pl.program_id(1), 1)
    _interleave(_attn_block(n, sink_ref, bias_ref, first, q_ref, kp_ref, kc_ref, vp_ref, vc_ref, o_ref)
                for n in range(ATT_BLOCKS))


def _attn_prompt(q, ks, vs, sink, batch, seq):
    steps = seq // (ATT_BLOCKS * WINDOW)
    assert seq % (ATT_BLOCKS * WINDOW) == 0
    cur = lambda b, i: (b * steps + i, 0)
    prev = lambda b, i: ((b * steps + i) * ATT_BLOCKS - jnp.minimum(i, 1), 0)
    prev_spec = pl.BlockSpec((WINDOW, 4 * KV_W), prev)
    cur_spec = pl.BlockSpec((ATT_BLOCKS * WINDOW, 4 * KV_W), cur)
    return pl.pallas_call(
        _attn_prompt_kernel,
        grid=(batch, steps),
        in_specs=[pl.BlockSpec(memory_space=pltpu.SMEM),
                  _resident((2, WINDOW, 2 * WINDOW), lambda b, i: (0, 0, 0)),
                  pl.BlockSpec((ATT_BLOCKS * WINDOW, ATT_W), cur),
                  prev_spec, cur_spec, prev_spec, cur_spec],
        out_specs=pl.BlockSpec((ATT_BLOCKS * WINDOW, ATT_W), cur),
        out_shape=jax.ShapeDtypeStruct((batch * seq, ATT_W), BF16),
        compiler_params=_cparams("parallel", "parallel"),
        name="attn_prompt",
    )(sink, _band_bias(), q, ks, ks, vs, vs)


def _interleave(stage_generators):
    pending = list(stage_generators)
    while pending:
        for gen in list(pending):
            if next(gen, StopIteration) is StopIteration:
                pending.remove(gen)


def _attn_sample_kernel(sink_ref, q_ref, kvn_ref, ck_ref, cv_ref, *rest):
    o_ref, nk_ref, nv_ref = rest[-3:]
    _interleave(_attn_sample_group(grp, sink_ref, q_ref, kvn_ref, ck_ref, cv_ref, o_ref, nk_ref, nv_ref)
                for grp in range(SAMPLE_GROUPS))


def _attn_sample_group(grp, sink_ref, q_ref, kvn_ref, ck_ref, cv_ref, o_ref, nk_ref, nv_ref):
    rows = slice(grp * ROWS_PER_STEP, (grp + 1) * ROWS_PER_STEP)
    ck_ref, cv_ref, nk_ref, nv_ref = (r.at[grp * SEQ_PER_STEP:(grp + 1) * SEQ_PER_STEP]
                                      for r in (ck_ref, cv_ref, nk_ref, nv_ref))
    lane = lax.broadcasted_iota(jnp.int32, (ROWS_PER_STEP, KV_W), 1)
    q16 = q_ref[rows, :].astype(F32)
    blocks = []
    for h in range(N_HEADS):
        c, half, g = h // 2, h % 2, h // (N_HEADS // KV_HEADS)
        own = lane >= HEAD_DIM if half else lane < HEAD_DIM
        qh = jnp.where(own, q16[:, c * 128:(c + 1) * 128], 0.0)
        blocks.append(qh if half == g else pltpu.roll(qh, HEAD_DIM, 1))
    wt = jnp.concatenate(blocks, axis=0).astype(BF16)

    row = lax.broadcasted_iota(jnp.int32, (WINDOW, WINDOW), 0)
    col = lax.broadcasted_iota(jnp.int32, (WINDOW, WINDOW), 1)
    tok = jnp.bitwise_and(row, DEC_SEQ - 1)
    row_seq = jnp.bitwise_and(jnp.right_shift(row, 2), SEQ_PER_STEP - 1)
    pad = jnp.zeros((WINDOW - ROWS_PER_STEP, KV_W), F32)
    kn = jnp.concatenate([kvn_ref[rows, :KV_W], pad], axis=0)
    vn = jnp.concatenate([kvn_ref[rows, KV_W:], pad], axis=0)
    kn_t = jnp.transpose(kn)
    vn_t = jnp.transpose(vn)
    yield

    s_c = jnp.zeros((WINDOW, WINDOW), F32)
    for j in range(SEQ_PER_STEP):
        sj = jnp.dot(wt, ck_ref[j].astype(BF16), preferred_element_type=F32)
        s_c = jnp.where(row_seq == j, sj, s_c)
    s_n = jnp.dot(wt, kn_t.astype(BF16), preferred_element_type=F32)
    yield
    s_c = jnp.where(col >= tok, s_c, NEG_BIG)
    new_ok = jnp.logical_and(jnp.right_shift(col, 2) == row_seq, jnp.bitwise_and(col, DEC_SEQ - 1) <= tok)
    s_n = jnp.where(new_ok, s_n, NEG_BIG)
    head = jnp.right_shift(lax.broadcasted_iota(jnp.int32, (WINDOW, 1), 0), 4)
    sink = jnp.zeros((WINDOW, 1), F32)
    for h in range(N_HEADS):
        sink = jnp.where(head == h, sink_ref[0, h] * LOG2E, sink)
    m = jnp.maximum(jnp.maximum(jnp.max(s_c, axis=-1, keepdims=True), jnp.max(s_n, axis=-1, keepdims=True)), sink)
    p_c = jnp.exp2(s_c - m)
    p_n = jnp.exp2(s_n - m)
    den = jnp.sum(p_c, axis=-1, keepdims=True) + jnp.sum(p_n, axis=-1, keepdims=True) + jnp.exp2(sink - m)
    p_cb = p_c.astype(BF16)
    yield
    o = jnp.dot(p_n.astype(BF16), vn.astype(BF16), preferred_element_type=F32)
    for j in range(SEQ_PER_STEP):
        oj = lax.dot_general(p_cb, cv_ref[j].astype(BF16), (((1,), (1,)), ((), ())), preferred_element_type=F32)
        o = o + jnp.where(row_seq == j, oj, 0.0)
    yield
    o = o / den
    outs = []
    for c in range(4):
        lo, hi = o[32 * c:32 * c + 16], o[32 * c + 16:32 * c + 32]
        if c < 2:
            outs.append(jnp.where(lane < HEAD_DIM, lo, pltpu.roll(hi, HEAD_DIM, 1)))
        else:
            outs.append(jnp.where(lane < HEAD_DIM, pltpu.roll(lo, HEAD_DIM, 1), hi))
    o_ref[rows, :] = jnp.concatenate(outs, axis=1).astype(o_ref.dtype)
    yield
    for j in range(SEQ_PER_STEP):
        shift = (WINDOW - DEC_SEQ - DEC_SEQ * j) % WINDOW
        for new_t, cache_ref, dst in ((kn_t, ck_ref, nk_ref), (vn_t, cv_ref, nv_ref)):
            dst[j] = jnp.where(col >= WINDOW - DEC_SEQ, pltpu.roll(new_t, shift, 1),
                               pltpu.roll(cache_ref[j], WINDOW - DEC_SEQ, 1))


def _attn_sample(q, kvn, sink, cache_kt, cache_vt, layer, prev_k, prev_v):
    depth, db = cache_kt.shape[:2]
    seqs = SAMPLE_GROUPS * SEQ_PER_STEP
    cache_spec = pl.BlockSpec((None, seqs, KV_W, WINDOW), lambda s: (layer, s, 0, 0))
    rows = lambda w: pl.BlockSpec((seqs * DEC_SEQ, w), lambda s: (s, 0))
    n_in = 5

    def call(extra, aliases):
        return pl.pallas_call(
            _attn_sample_kernel,
            grid=(db // seqs,),
            in_specs=[pl.BlockSpec(memory_space=pltpu.SMEM), rows(ATT_W), rows(2 * KV_W), cache_spec, cache_spec]
            + [pl.BlockSpec(memory_space=pl.ANY)] * len(extra),
            out_specs=[rows(ATT_W), cache_spec, cache_spec],
            out_shape=[jax.ShapeDtypeStruct((db * DEC_SEQ, ATT_W), BF16),
                       jax.ShapeDtypeStruct((depth, db, KV_W, WINDOW), F32),
                       jax.ShapeDtypeStruct((depth, db, KV_W, WINDOW), F32)],
            input_output_aliases=aliases,
            compiler_params=_cparams("parallel"),
            name="attn_sample",
        )(sink, q, kvn, cache_kt, cache_vt, *extra)

    return _stacked_out(call, layer, {1: prev_k, 2: prev_v}, n_in)


def _hgrn_out(o, g_act, normw):
    ms = jnp.mean(o * o, axis=-1, keepdims=True)
    return o * lax.rsqrt(ms + RMS_EPS) * normw * g_act


def _cumsum_rows(x, tri):
    hi = x.astype(BF16)
    lo = (x - hi.astype(F32)).astype(BF16)
    dot = lambda t: jnp.dot(tri, t, preferred_element_type=F32)
    return dot(hi) + dot(lo)


def _heads(t):
    return [t[:, h * HG_D:(h + 1) * HG_D] for h in range(HG_HEADS)]


def _level_reference(b, n, row):
    if n == 2:
        return jnp.where(jnp.bitwise_and(row, 1) == 0, b, pltpu.roll(b, 1, 0))
    if n == 4:
        m4 = jnp.bitwise_and(row, 3)
        return jnp.where(m4 == 0, pltpu.roll(b, CHUNK - 1, 0),
                         jnp.where(m4 == 1, b, jnp.where(m4 == 2, pltpu.roll(b, 1, 0), pltpu.roll(b, 2, 0))))
    bounds = [m * n + n // 2 - 1 for m in range(CHUNK // n)]
    pieces = [jnp.broadcast_to(b[t:t + 1], (n, b.shape[1])) for t in bounds]
    return pieces[0] if len(pieces) == 1 else jnp.concatenate(pieces, axis=0)


def _scores(q, k, e_q, e_k):
    qs, ks = _heads(q * e_q.astype(BF16)), _heads(k * e_k.astype(BF16))
    return [lax.dot_general(a, c, (((1,), (1,)), ((), ())), preferred_element_type=F32) for a, c in zip(qs, ks)]


def _intra_chunk_scores(q, k, b, lvl, fast):
    if fast:
        b_mid, b_end = b[HALF - 1:HALF], b[CHUNK - 1:CHUNK]
        r = jnp.concatenate([jnp.broadcast_to(0.5 * b_mid, (HALF, HG_W)),
                             jnp.broadcast_to(0.5 * (b_mid + b_end), (HALF, HG_W))], axis=0)
        e = jnp.exp(b - r)
        e6 = jnp.exp(-jnp.abs(b - b_mid))
        return [jnp.where(lvl == 6, across, jnp.where(lvl >= -1, within, 0.0))
                for within, across in zip(_scores(q, k, e, 1.0 / e), _scores(q, k, e6, e6))]
    row = lax.broadcasted_iota(jnp.int32, (CHUNK, HG_W), 0)
    atts = [jnp.where(lvl == -1, jnp.sum(qk, axis=-1, keepdims=True), 0.0) for qk in _heads((q * k).astype(F32))]
    for level in range(7):
        e = jnp.exp(-jnp.abs(b - _level_reference(b, 2 << level, row)))
        atts = [jnp.where(lvl == level, p, att) for p, att in zip(_scores(q, k, e, e), atts)]
    return atts


def _hgrn_chunks(fast, hq_ref, hk_ref, lf_ref, hv_ref, hg_ref, normw_ref, o_ref, st_ref):
    row = lax.broadcasted_iota(jnp.int32, (CHUNK, CHUNK), 0)
    col = lax.broadcasted_iota(jnp.int32, (CHUNK, CHUNK), 1)
    tri = jnp.where(row >= col, 1.0, 0.0).astype(BF16)
    lvl = jnp.where(row >= col, 31 - lax.clz(jnp.bitwise_xor(row, col)), -2)
    subs = [slice(c * CHUNK, (c + 1) * CHUNK) for c in range(hq_ref.shape[0] // CHUNK)]
    q, k, v = ([ref[rs] for rs in subs] for ref in (hq_ref, hk_ref, hv_ref))
    b = [_cumsum_rows(lf_ref[rs], tri) for rs in subs]
    b_end = [t[CHUNK - 1:CHUNK] for t in b]
    atts = [_intra_chunk_scores(qc, kc, bc, lvl, fast) for qc, kc, bc in zip(q, k, b)]
    q_in = [_heads(qc * jnp.exp(bc).astype(BF16)) for qc, bc in zip(q, b)]
    k_out = [_heads(kc.astype(F32) * jnp.exp(be - bc)) for kc, bc, be in zip(k, b, b_end)]
    k_t = [[jnp.transpose(t).astype(BF16) for t in per_head] for per_head in k_out]
    decay = [[jnp.transpose(jnp.broadcast_to(t, (HG_D, HG_D))) for t in _heads(jnp.exp(be))] for be in b_end]
    vs = [_heads(vc) for vc in v]
    local = [[jnp.dot(a.astype(BF16), vh, preferred_element_type=F32) for a, vh in zip(ac, vc)]
             for ac, vc in zip(atts, vs)]
    grow = [[jnp.dot(kt, vh, preferred_element_type=F32) for kt, vh in zip(kc, vc)] for kc, vc in zip(k_t, vs)]
    state = [st_ref[0, h] for h in range(HG_HEADS)]
    for c, rs in enumerate(subs):
        gs = _heads(hg_ref[rs])
        outs = [_hgrn_out(jnp.dot(q_in[c][h], state[h].astype(BF16), preferred_element_type=F32) + local[c][h],
                          gs[h], normw_ref[...]) for h in range(HG_HEADS)]
        o_ref[rs] = jnp.concatenate(outs, axis=1).astype(o_ref.dtype)
        state = [decay[c][h] * state[h] + grow[c][h] for h in range(HG_HEADS)]
    for h in range(HG_HEADS):
        st_ref[0, h] = state[h]


def _hgrn_prompt_kernel(ok_ref, *refs):
    st_ref = refs[-1]

    @pl.when(pl.program_id(1) == 0)
    def _():
        st_ref[...] = jnp.zeros_like(st_ref)

    first = (pl.program_id(0) * pl.num_programs(1) + pl.program_id(1)) * HGRN_CHUNKS
    ok = ok_ref[first] != 0
    for c in range(1, HGRN_CHUNKS):
        ok = jnp.logical_and(ok, ok_ref[first + c] != 0)

    @pl.when(ok)
    def _():
        _hgrn_chunks(True, *refs)

    @pl.when(jnp.logical_not(ok))
    def _():
        _hgrn_chunks(False, *refs)


def _hgrn_prompt(chunk_ok, hq, hk, lf, hv, hg, normw, batch, seq):
    nc = seq // (HGRN_CHUNKS * CHUNK)
    assert seq % (HGRN_CHUNKS * CHUNK) == 0
    rows = pl.BlockSpec((HGRN_CHUNKS * CHUNK, HG_W), lambda b, i, ok: (b * nc + i, 0))
    return pl.pallas_call(
        _hgrn_prompt_kernel,
        grid_spec=pltpu.PrefetchScalarGridSpec(
            num_scalar_prefetch=1,
            grid=(batch, nc),
            in_specs=[rows, rows, rows, rows, rows, pl.BlockSpec((1, HG_D), lambda b, i, ok: (0, 0))],
            out_specs=[rows, pl.BlockSpec((1, HG_HEADS, HG_D, HG_D), lambda b, i, ok: (b, 0, 0, 0))]),
        out_shape=[jax.ShapeDtypeStruct((batch * seq, HG_W), BF16),
                   jax.ShapeDtypeStruct((batch, HG_HEADS, HG_D, HG_D), F32)],
        compiler_params=_cparams("parallel", "arbitrary"),
        name="hgrn_prompt",
    )(chunk_ok, hq, hk, lf, hv, hg, normw)


def _hgrn_sample_kernel(hq_ref, hk_ref, lf_ref, hv_ref, hg_ref, normw_ref, st_ref, *rest):
    o_ref, ns_ref = rest[-2:]
    _interleave(_hgrn_sample_group(grp, hq_ref, hk_ref, lf_ref, hv_ref, hg_ref, normw_ref, st_ref, o_ref, ns_ref)
                for grp in range(SAMPLE_GROUPS))


def _hgrn_sample_group(grp, hq_ref, hk_ref, lf_ref, hv_ref, hg_ref, normw_ref, st_ref, o_ref, ns_ref):
    rows = slice(grp * ROWS_PER_STEP, (grp + 1) * ROWS_PER_STEP)
    st_ref, ns_ref = (r.at[grp * SEQ_PER_STEP:(grp + 1) * SEQ_PER_STEP] for r in (st_ref, ns_ref))
    q, k, v = (r[rows, :].astype(F32) for r in (hq_ref, hk_ref, hv_ref))
    logf = lf_ref[rows, :]
    row = lax.broadcasted_iota(jnp.int32, (ROWS_PER_STEP, HG_W), 0)
    tok = jnp.bitwise_and(row, DEC_SEQ - 1)
    seq_of_row = jnp.right_shift(row[:, :HG_D], 2)
    down = lambda t, d: pltpu.roll(t, d, 0)
    up = lambda t, d: pltpu.roll(t, ROWS_PER_STEP - d, 0)
    b = logf
    for d in range(1, DEC_SEQ):
        b = b + jnp.where(tok >= d, down(logf, d), 0.0)
    b_end = jnp.where(tok == 3, b, jnp.where(tok == 2, up(b, 1), jnp.where(tok == 1, up(b, 2), up(b, 3))))
    qe = (q * jnp.exp(b)).astype(BF16)
    kd = k * jnp.exp(b_end - b)
    e_end = jnp.exp(b_end)
    intra = [jnp.zeros((ROWS_PER_STEP, HG_D), F32) for _ in range(HG_HEADS)]
    for d in range(DEC_SEQ):
        kd_, bd_, vd_ = (k, b, v) if d == 0 else (down(k, d), down(b, d), down(v, d))
        w = jnp.where(tok >= d, q * kd_ * jnp.exp(jnp.minimum(b - bd_, 0.0)), 0.0)
        for h in range(HG_HEADS):
            sl = slice(h * HG_D, (h + 1) * HG_D)
            intra[h] = intra[h] + jnp.sum(w[:, sl], axis=-1, keepdims=True) * vd_[:, sl]
    pad = jnp.zeros((HG_D - 2 * ROWS_PER_STEP, HG_D), F32)
    colh = lax.broadcasted_iota(jnp.int32, (HG_D, HG_D), 1)
    normw = normw_ref[...]
    yield
    for h in range(HG_HEADS):
        sl = slice(h * HG_D, (h + 1) * HG_D)
        vb = jnp.concatenate([v[:, sl], jnp.zeros((HG_D - ROWS_PER_STEP, HG_D), F32)], axis=0).astype(BF16)
        tt = jnp.transpose(jnp.concatenate([kd[:, sl], e_end[:, sl], pad], axis=0))
        inter = jnp.zeros((ROWS_PER_STEP, HG_D), F32)
        for j in range(SEQ_PER_STEP):
            s0 = st_ref[j, h]
            oi = jnp.dot(qe[:, sl], s0.astype(BF16), preferred_element_type=F32)
            inter = jnp.where(seq_of_row == j, oi, inter)
            kd_t = jnp.where(jnp.right_shift(colh, 2) == j, tt, 0.0).astype(BF16)
            e_col = ROWS_PER_STEP + DEC_SEQ * j
            decay = jnp.broadcast_to(tt[:, e_col:e_col + 1], (HG_D, HG_D))
            ns_ref[j, h] = decay * s0 + jnp.dot(kd_t, vb, preferred_element_type=F32)
        o_ref[rows, sl] = _hgrn_out(inter + intra[h], hg_ref[rows, sl], normw).astype(o_ref.dtype)
        yield


def _hgrn_sample(hq, hk, lf, hv, hg, normw, state, layer, prev_state):
    depth, db = state.shape[:2]
    seqs = SAMPLE_GROUPS * SEQ_PER_STEP
    rows = pl.BlockSpec((seqs * DEC_SEQ, HG_W), lambda s: (s, 0))
    st_spec = pl.BlockSpec((None, seqs, HG_HEADS, HG_D, HG_D), lambda s: (layer, s, 0, 0, 0))
    n_in = 7

    def call(extra, aliases):
        return pl.pallas_call(
            _hgrn_sample_kernel,
            grid=(db // seqs,),
            in_specs=[rows, rows, rows, rows, rows, pl.BlockSpec((1, HG_D), lambda s: (0, 0)), st_spec]
            + [pl.BlockSpec(memory_space=pl.ANY)] * len(extra),
            out_specs=[rows, st_spec],
            out_shape=[jax.ShapeDtypeStruct((db * DEC_SEQ, HG_W), BF16), jax.ShapeDtypeStruct(state.shape, F32)],
            input_output_aliases=aliases,
            compiler_params=_cparams("parallel"),
            name="hgrn_sample",
        )(hq, hk, lf, hv, hg, normw, state, *extra)

    return _stacked_out(call, layer, {1: prev_state}, n_in)


def _layer_norm(y, g, b):
    mu = jnp.mean(y, axis=-1, keepdims=True)
    yc = y - mu
    var = jnp.mean(yc * yc, axis=-1, keepdims=True)
    return yc * lax.rsqrt(var + LN_EPS) * g + b


def _post_kernel(alpha, n_cast, a_ref, h_ref, g_ref, x_ref, wua_ref, wuh_ref, wo_ref, l1g_ref, l1b_ref,
                 wf1_ref, wf2_ref, l2g_ref, l2b_ref, *refs):
    o_ref = refs[n_cast]
    _cast_rows(refs[:n_cast], refs[n_cast + 1:])
    groups = [slice(r0, r0 + POST_GROUP) for r0 in range(0, x_ref.shape[0], POST_GROUP)]
    dot = lambda lhs, w_ref: jnp.dot(lhs, w_ref[...], preferred_element_type=F32)
    ua = [dot(a_ref[rs], wua_ref) for rs in groups]
    uh = [dot(h_ref[rs], wuh_ref) for rs in groups]
    gate = jax.nn.sigmoid
    merged = [(gate(g_ref[rs, :D_MODEL]) * a + gate(g_ref[rs, D_MODEL:]) * h).astype(BF16)
              for rs, a, h in zip(groups, ua, uh)]
    m = [dot(t, wo_ref) for t in merged]
    x1 = [_layer_norm(alpha * x_ref[rs] + t, l1g_ref[...], l1b_ref[...]) for rs, t in zip(groups, m)]
    hid = [jnp.square(jnp.maximum(dot(t.astype(BF16), wf1_ref), 0.0)).astype(BF16) for t in x1]
    ff = [dot(t, wf2_ref) for t in hid]
    for rs, t, f in zip(groups, x1, ff):
        o_ref[rs] = _layer_norm(alpha * t + f, l2g_ref[...], l2b_ref[...])


def _post(a, h, g, x, wb, p, layer, alpha, cast=(), cast_layer=0):
    t = x.shape[0]
    tm = min(POST_ROWS, t)
    assert t % tm == 0
    rows = lambda w: pl.BlockSpec((tm, w), lambda i: (i, 0))
    vec = lambda w: _resident((None, 1, w), lambda i: (layer, 0, 0))
    mat = lambda w: _resident(w.shape, lambda i: (0, 0))
    cast_src, cast_dst, cast_shapes = _cast_specs(cast, cast_layer, t // tm)
    outs = pl.pallas_call(
        functools.partial(_post_kernel, alpha, len(cast)),
        grid=(t // tm,),
        in_specs=[rows(ATT_W), rows(HG_W), rows(GATE_W), rows(D_MODEL), mat(wb[0]), mat(wb[1]), mat(wb[2]),
                  vec(D_MODEL), vec(D_MODEL), mat(wb[3]), mat(wb[4]), vec(D_MODEL), vec(D_MODEL)] + cast_src,
        out_specs=[rows(D_MODEL)] + cast_dst,
        out_shape=[jax.ShapeDtypeStruct((t, D_MODEL), F32)] + cast_shapes,
        compiler_params=_cparams("parallel"),
        name="post",
    )(a, h, g, x, wb[0], wb[1], wb[2], p["ln1_g"], p["ln1_b"], wb[3], wb[4], p["ln2_g"], p["ln2_b"], *cast)
    return outs[0], outs[1:]


def kernel(x_prompt, x_sample, cache_k, cache_v, state_hgrn, w_in, b_gate, attn_sink, hgrn_lb_logits, hgrn_norm_w,
           w_up_attn, w_up_hgrn, w_out, ln1_g, ln1_b, w_ff1, w_ff2, ln2_g, ln2_b):
    depth = w_in.shape[0]
    batch, seq, _ = x_prompt.shape
    db, dec_seq, _ = x_sample.shape
    assert dec_seq == DEC_SEQ and seq % CHUNK == 0 and db % (SAMPLE_GROUPS * SEQ_PER_STEP) == 0
    assert cache_k.shape[2:] == (WINDOW, KV_HEADS, HEAD_DIM)
    alpha = (2 * depth) ** 0.25

    vec = lambda t: t.reshape(depth, 1, t.shape[-1])
    b_gate3 = vec(b_gate)
    p = dict(ln1_g=vec(ln1_g), ln1_b=vec(ln1_b), ln2_g=vec(ln2_g), ln2_b=vec(ln2_b))
    mixer_w = (w_up_attn, w_up_hgrn, w_out, w_ff1, w_ff2)
    w_in_b = w_in[0].astype(BF16)
    to_t = lambda c: jnp.transpose(c, (0, 1, 3, 4, 2)).reshape(depth, db, KV_W, WINDOW)
    from_t = lambda c: jnp.transpose(c.reshape(depth, db, KV_HEADS, HEAD_DIM, WINDOW), (0, 1, 4, 2, 3))
    ck_t, cv_t = to_t(cache_k), to_t(cache_v)

    xp = x_prompt.reshape(batch * seq, D_MODEL)
    xs = x_sample.reshape(db * DEC_SEQ, D_MODEL)
    pk, pv, ps = [], [], []
    sk = sv = ss = None
    for l in range(depth):
        sink = attn_sink[l].reshape(1, N_HEADS)
        normw = hgrn_norm_w[l].reshape(1, HG_D)
        q, kv, ks, vs, hq, hk, lf, hv, hg, g, chunk_ok, *wb = _in_proj(xp, w_in_b, b_gate3, hgrn_lb_logits, l,
                                                                       cast=mixer_w)
        a = _attn_prompt(q, ks, vs, sink, batch, seq)
        h, st = _hgrn_prompt(chunk_ok[:, 0, 0], hq, hk, lf, hv, hg, normw, batch, seq)
        xp, next_w_in = _post(a, h, g, xp, wb, p, l, alpha, cast=(w_in,) if l + 1 < depth else (), cast_layer=l + 1)
        win = kv.reshape(batch, seq, 2 * KV_W)[:, seq - WINDOW:].reshape(batch, WINDOW, 2, KV_HEADS, HEAD_DIM)
        pk.append(win[:, :, 0])
        pv.append(win[:, :, 1])
        ps.append(st)
        q, kvn, _, _, hq, hk, lf, hv, hg, g, _ = _in_proj(xs, w_in_b, b_gate3, hgrn_lb_logits, l)
        a, sk, sv = _attn_sample(q, kvn, sink, ck_t, cv_t, l, sk, sv)
        h, ss = _hgrn_sample(hq, hk, lf, hv, hg, normw, state_hgrn, l, ss)
        xs, _ = _post(a, h, g, xs, wb, p, l, alpha)
        if next_w_in:
            w_in_b = next_w_in[0]
    return (xp.reshape(batch, seq, D_MODEL), xs.reshape(db, DEC_SEQ, D_MODEL), jnp.stack(pk), jnp.stack(pv),
            jnp.stack(ps), from_t(sk), from_t(sv), ss)
```

```python
import functools

import jax
import jax.numpy as jnp
from jax import lax
from jax.experimental import pallas as pl
from jax.experimental.pallas import tpu as pltpu

F32 = jnp.float32
BF16 = jnp.bfloat16

D_MODEL = 1024
WINDOW = 128
N_HEADS = 8
KV_HEADS = 2
HEAD_DIM = 64
ATT_W = N_HEADS * HEAD_DIM
KV_W = KV_HEADS * HEAD_DIM
HG_HEADS = 4
HG_D = 128
HG_W = HG_HEADS * HG_D
GATE_W = 2 * D_MODEL
D_FF = 4 * D_MODEL
DEC_SEQ = 4
LN_EPS = 1e-5
RMS_EPS = 1e-6
NEG_BIG = -1e30
LB_FLOOR = 1e-30
LOG2E = 1.4426950408889634
QK_SCALE = HEAD_DIM ** -0.5 * LOG2E

AQ0, AK0, AV0, HQ0, HF0, HI0, HG0, GT0, IN_COLS = 0, 512, 640, 768, 1280, 1792, 2304, 2816, 4864

ATT_BLOCKS = 8
CHUNK = 128
HGRN_CHUNKS = 8
HALF = CHUNK // 2
FAST_DECAY_LIMIT = 120.0
FAST_Q_LIMIT = 1e9
SEQ_PER_STEP = 4
ROWS_PER_STEP = SEQ_PER_STEP * DEC_SEQ
SAMPLE_GROUPS = 8
IN_PROJ_ROWS = 512
POST_ROWS = 512
POST_GROUP = 256
V7X_VMEM_LIMIT = 56 * 1024 * 1024


def _cparams(*sem):
    return pltpu.CompilerParams(dimension_semantics=sem, vmem_limit_bytes=V7X_VMEM_LIMIT)


def _resident(shape, index_map):
    return pl.BlockSpec(shape, index_map, pipeline_mode=pl.Buffered(1))


def _stacked_out(call, layer, stacked, n_in):
    extra, aliases = [], {}
    if layer > 0:
        for out_idx, prev in stacked.items():
            aliases[n_in + len(extra)] = out_idx
            extra.append(prev)
    return call(extra, aliases)


def _lower_bound(logits_ref, layer):
    lg = logits_ref[...]
    e = jnp.exp(lg - jnp.max(lg, axis=0, keepdims=True))
    p = e / jnp.sum(e, axis=0, keepdims=True)
    lb = jnp.zeros((1, HG_W), F32)
    for i in range(1, layer + 1):
        lb = lb + p[i:i + 1]
    return lb


def _swish(z):
    return z * jax.nn.sigmoid(z)


def _forget_and_key(zf, lb):
    e = jnp.exp(-jnp.abs(zf))
    r = 1.0 / (1.0 + e)
    pos = zf >= 0
    sig_p = jnp.where(pos, r, e * r)
    sig_n = jnp.where(pos, e * r, r)
    logf = jnp.log(jnp.maximum(lb, LB_FLOOR) + (1.0 - lb) * sig_p)
    return logf, (1.0 - lb) * sig_n


def _cast_rows(src_refs, dst_refs):
    for src, dst in zip(src_refs, dst_refs):
        dst[...] = src[...].astype(BF16)


def _in_proj_kernel(layer, n_cast, x_ref, w_ref, bg_ref, logits_ref, *refs):
    cast_src, refs = refs[:n_cast], refs[n_cast:]
    outs, refs = refs[:11], refs[11:]
    cast_dst, xb_ref = refs[:n_cast], refs[n_cast]
    _cast_rows(cast_src, cast_dst)
    for _ in _in_proj_stages(layer, x_ref, w_ref, bg_ref, logits_ref, outs, xb_ref):
        pass


def _in_proj_stages(layer, x_ref, w_ref, bg_ref, logits_ref, outs, xb_ref):
    q_ref, kv_ref, ks_ref, vs_ref, hq_ref, hk_ref, lf_ref, hv_ref, hg_ref, g_ref, ok_ref = outs
    xb = x_ref[...].astype(BF16)
    lb = _lower_bound(logits_ref, layer)
    chunks = x_ref.shape[0] // CHUNK
    q_ok = []
    decay_ok = []

    def put_q(z):
        q_ref[...] = (z * QK_SCALE).astype(BF16)

    def put_kv(z):
        kv_ref[...] = z
        ks_ref[...] = _split_heads(z[:, :KV_W])
        vs_ref[...] = _split_heads(z[:, KV_W:])

    def put_hq(z):
        hq = _swish(z)
        hq_ref[...] = hq.astype(BF16)
        for c in range(chunks):
            q_ok.append(jnp.max(jnp.abs(hq[c * CHUNK:(c + 1) * CHUNK]), keepdims=True) <= FAST_Q_LIMIT)

    def put_hf(z):
        logf, key = _forget_and_key(z, lb)
        lf_ref[...] = logf
        hk_ref[...] = key.astype(BF16)
        for lo in range(0, chunks * CHUNK, HALF):
            total = jnp.sum(logf[lo:lo + HALF], axis=0, keepdims=True)
            decay_ok.append(jnp.min(total, keepdims=True) >= -FAST_DECAY_LIMIT)

    def put_hv(z):
        hv_ref[...] = z.astype(BF16)

    def put_hg(z):
        hg_ref[...] = _swish(z)

    def put_gates(lo):
        def put(z):
            g_ref[:, lo:lo + 512] = z + bg_ref[:, lo:lo + 512]
        return put

    gates = [(GT0 + lo, GT0 + lo + 512, put_gates(lo)) for lo in range(0, GATE_W, 512)]
    stages = [(HF0, HI0, put_hf), gates[0], (HQ0, HF0, put_hq), gates[1], (AK0, HQ0, put_kv), gates[2],
              (HG0, GT0, put_hg), (AQ0, AK0, put_q), (HI0, HG0, put_hv), gates[3]]
    pending = None
    xb_ref[...] = xb
    for lo, hi, put in stages:
        z = jnp.dot(xb_ref[...], w_ref[:, lo:hi], preferred_element_type=F32)
        if pending is not None:
            pending[0](pending[1])
        pending = (put, z)
        yield
    pending[0](pending[1])
    for c in range(chunks):
        ok = jnp.logical_and(q_ok[c], jnp.logical_and(decay_ok[2 * c], decay_ok[2 * c + 1]))
        ok_ref[c] = jnp.broadcast_to(jnp.where(ok, 1, 0), (1, 128)).astype(jnp.int32)


def _split_heads(t):
    low = lax.broadcasted_iota(jnp.int32, t.shape, 1) < HEAD_DIM
    h0_lo = jnp.where(low, t, 0.0)
    h1_hi = jnp.where(low, 0.0, t)
    blocks = [h0_lo, pltpu.roll(h0_lo, HEAD_DIM, 1), pltpu.roll(h1_hi, HEAD_DIM, 1), h1_hi]
    return jnp.concatenate(blocks, axis=1).astype(BF16)


def _cast_specs(weights, layer, steps):
    src, dst, shapes = [], [], []
    for w in weights:
        _, r, c = w.shape
        slab = r // steps
        assert r % steps == 0 and (slab % 16 == 0 or steps == 1)
        last = steps - 1
        src.append(pl.BlockSpec((None, slab, c), lambda i, layer=layer, last=last: (layer, jnp.minimum(i, last), 0)))
        dst.append(pl.BlockSpec((slab, c), lambda i, last=last: (jnp.minimum(i, last), 0)))
        shapes.append(jax.ShapeDtypeStruct((r, c), BF16))
    return src, dst, shapes


def _in_proj(x, w_b, b_gate, logits, layer, cast=()):
    t = x.shape[0]
    tm = min(IN_PROJ_ROWS, t)
    assert t % tm == 0 and tm % CHUNK == 0
    rows = lambda w: pl.BlockSpec((tm, w), lambda i: (i, 0))
    out = lambda w, dt: jax.ShapeDtypeStruct((t, w), dt)
    cast_src, cast_dst, cast_shapes = _cast_specs(cast, layer, t // tm)
    return pl.pallas_call(
        functools.partial(_in_proj_kernel, layer, len(cast)),
        grid=(t // tm,),
        in_specs=[rows(D_MODEL),
                  _resident((D_MODEL, IN_COLS), lambda i: (0, 0)),
                  _resident((None, 1, GATE_W), lambda i: (layer, 0, 0)),
                  _resident(logits.shape, lambda i: (0, 0))] + cast_src,
        out_specs=[rows(ATT_W), rows(2 * KV_W), rows(4 * KV_W), rows(4 * KV_W),
                   rows(HG_W), rows(HG_W), rows(HG_W), rows(HG_W), rows(HG_W),
                   rows(GATE_W), pl.BlockSpec((tm // CHUNK, 1, 128), lambda i: (i, 0, 0))] + cast_dst,
        out_shape=[out(ATT_W, BF16), out(2 * KV_W, F32), out(4 * KV_W, BF16), out(4 * KV_W, BF16),
                   out(HG_W, BF16), out(HG_W, BF16), out(HG_W, F32), out(HG_W, BF16), out(HG_W, F32),
                   out(GATE_W, F32),
                   jax.ShapeDtypeStruct((t // CHUNK, 1, 128), jnp.int32)] + cast_shapes,
        scratch_shapes=[pltpu.VMEM((tm, D_MODEL), BF16)],
        compiler_params=_cparams("parallel"),
        name="in_proj",
    )(x, w_b, b_gate, logits, *cast)


def _attn_block(n, sink_ref, bias_ref, first, q_ref, kp_ref, kc_ref, vp_ref, vc_ref, o_ref):
    r = WINDOW
    top = lax.broadcasted_iota(jnp.int32, (2 * r, 1), 0) < r
    combos = [(g, half) for g in range(KV_HEADS) for half in range(2)]
    lanes = lambda g, half: slice((2 * g + half) * KV_W, (2 * g + half + 1) * KV_W)
    sink = [jnp.where(top, sink_ref[0, 4 * g + half], sink_ref[0, 4 * g + 2 + half]) * LOG2E for g, half in combos]
    own = slice(n * r, (n + 1) * r)

    def kv_rows(prev_ref, cur_ref, g, half):
        before = prev_ref[:, lanes(g, half)] if n == 0 else cur_ref[(n - 1) * r:n * r, lanes(g, half)]
        return jnp.concatenate([before, cur_ref[own, lanes(g, half)]], axis=0)

    qg = [jnp.concatenate([q_ref[own, 2 * g * 128:(2 * g + 1) * 128],
                           q_ref[own, (2 * g + 1) * 128:(2 * g + 2) * 128]], axis=0) for g in range(KV_HEADS)]
    b = bias_ref[first] if n == 0 else bias_ref[1]
    bias = jnp.concatenate([b, b], axis=0)
    s = [lax.dot_general(qg[g], kv_rows(kp_ref, kc_ref, g, half), (((1,), (1,)), ((), ())),
                         preferred_element_type=F32) + bias for g, half in combos]
    yield
    m = [jnp.maximum(jnp.max(t, axis=-1, keepdims=True), sk) for t, sk in zip(s, sink)]
    p = [jnp.exp2(t - mx) for t, mx in zip(s, m)]
    yield
    den = [jnp.sum(t, axis=-1, keepdims=True) + jnp.exp2(sk - mx) for t, sk, mx in zip(p, sink, m)]
    o = [jnp.dot(t.astype(BF16), kv_rows(vp_ref, vc_ref, g, half), preferred_element_type=F32) / d
         for t, d, (g, half) in zip(p, den, combos)]
    yield
    cols = []
    for g in range(KV_HEADS):
        acc = o[2 * g] + o[2 * g + 1]
        cols += [acc[:r], acc[r:]]
    o_ref[own, :] = jnp.concatenate(cols, axis=1).astype(o_ref.dtype)


def _band_bias():
    qi = lax.broadcasted_iota(jnp.int32, (WINDOW, 2 * WINDOW), 0)
    kj = lax.broadcasted_iota(jnp.int32, (WINDOW, 2 * WINDOW), 1)
    band = jnp.logical_and(kj >= qi, kj <= qi + WINDOW)
    first = jnp.logical_and(band, kj >= WINDOW)
    return jnp.where(jnp.stack([first, band]), 0.0, NEG_BIG).astype(F32)


def _attn_prompt_kernel(sink_ref, bias_ref, q_ref, kp_ref, kc_ref, vp_ref, vc_ref, o_ref):
    first = jnp.minimum(pl.program_id(1), 1)
    _interleave(_attn_block(n, sink_ref, bias_ref, first, q_ref, kp_ref, kc_ref, vp_ref, vc_ref, o_ref)
                for n in range(ATT_BLOCKS))


def _attn_prompt(q, ks, vs, sink, batch, seq):
    steps = seq // (ATT_BLOCKS * WINDOW)
    assert seq % (ATT_BLOCKS * WINDOW) == 0
    cur = lambda b, i: (b * steps + i, 0)
    prev = lambda b, i: ((b * steps + i) * ATT_BLOCKS - jnp.minimum(i, 1), 0)
    prev_spec = pl.BlockSpec((WINDOW, 4 * KV_W), prev)
    cur_spec = pl.BlockSpec((ATT_BLOCKS * WINDOW, 4 * KV_W), cur)
    return pl.pallas_call(
        _attn_prompt_kernel,
        grid=(batch, steps),
        in_specs=[pl.BlockSpec(memory_space=pltpu.SMEM),
                  _resident((2, WINDOW, 2 * WINDOW), lambda b, i: (0, 0, 0)),
                  pl.BlockSpec((ATT_BLOCKS * WINDOW, ATT_W), cur),
                  prev_spec, cur_spec, prev_spec, cur_spec],
        out_specs=pl.BlockSpec((ATT_BLOCKS * WINDOW, ATT_W), cur),
        out_shape=jax.ShapeDtypeStruct((batch * seq, ATT_W), BF16),
        compiler_params=_cparams("parallel", "parallel"),
        name="attn_prompt",
    )(sink, _band_bias(), q, ks, ks, vs, vs)


def _interleave(stage_generators):
    pending = list(stage_generators)
    while pending:
        for gen in list(pending):
            if next(gen, StopIteration) is StopIteration:
                pending.remove(gen)


def _attn_sample_kernel(sink_ref, q_ref, kvn_ref, ck_ref, cv_ref, *rest):
    o_ref, nk_ref, nv_ref = rest[-3:]
    _interleave(_attn_sample_group(grp, sink_ref, q_ref, kvn_ref, ck_ref, cv_ref, o_ref, nk_ref, nv_ref)
                for grp in range(SAMPLE_GROUPS))


def _attn_sample_group(grp, sink_ref, q_ref, kvn_ref, ck_ref, cv_ref, o_ref, nk_ref, nv_ref):
    rows = slice(grp * ROWS_PER_STEP, (grp + 1) * ROWS_PER_STEP)
    ck_ref, cv_ref, nk_ref, nv_ref = (r.at[grp * SEQ_PER_STEP:(grp + 1) * SEQ_PER_STEP]
                                      for r in (ck_ref, cv_ref, nk_ref, nv_ref))
    lane = lax.broadcasted_iota(jnp.int32, (ROWS_PER_STEP, KV_W), 1)
    q16 = q_ref[rows, :].astype(F32)
    blocks = []
    for h in range(N_HEADS):
        c, half, g = h // 2, h % 2, h // (N_HEADS // KV_HEADS)
        own = lane >= HEAD_DIM if half else lane < HEAD_DIM
        qh = jnp.where(own, q16[:, c * 128:(c + 1) * 128], 0.0)
        blocks.append(qh if half == g else pltpu.roll(qh, HEAD_DIM, 1))
    wt = jnp.concatenate(blocks, axis=0).astype(BF16)

    row = lax.broadcasted_iota(jnp.int32, (WINDOW, WINDOW), 0)
    col = lax.broadcasted_iota(jnp.int32, (WINDOW, WINDOW), 1)
    tok = jnp.bitwise_and(row, DEC_SEQ - 1)
    row_seq = jnp.bitwise_and(jnp.right_shift(row, 2), SEQ_PER_STEP - 1)
    pad = jnp.zeros((WINDOW - ROWS_PER_STEP, KV_W), F32)
    kn = jnp.concatenate([kvn_ref[rows, :KV_W], pad], axis=0)
    vn = jnp.concatenate([kvn_ref[rows, KV_W:], pad], axis=0)
    kn_t = jnp.transpose(kn)
    vn_t = jnp.transpose(vn)
    yield

    s_c = jnp.zeros((WINDOW, WINDOW), F32)
    for j in range(SEQ_PER_STEP):
        sj = jnp.dot(wt, ck_ref[j].astype(BF16), preferred_element_type=F32)
        s_c = jnp.where(row_seq == j, sj, s_c)
    s_n = jnp.dot(wt, kn_t.astype(BF16), preferred_element_type=F32)
    yield
    s_c = jnp.where(col >= tok, s_c, NEG_BIG)
    new_ok = jnp.logical_and(jnp.right_shift(col, 2) == row_seq, jnp.bitwise_and(col, DEC_SEQ - 1) <= tok)
    s_n = jnp.where(new_ok, s_n, NEG_BIG)
    head = jnp.right_shift(lax.broadcasted_iota(jnp.int32, (WINDOW, 1), 0), 4)
    sink = jnp.zeros((WINDOW, 1), F32)
    for h in range(N_HEADS):
        sink = jnp.where(head == h, sink_ref[0, h] * LOG2E, sink)
    m = jnp.maximum(jnp.maximum(jnp.max(s_c, axis=-1, keepdims=True), jnp.max(s_n, axis=-1, keepdims=True)), sink)
    p_c = jnp.exp2(s_c - m)
    p_n = jnp.exp2(s_n - m)
    den = jnp.sum(p_c, axis=-1, keepdims=True) + jnp.sum(p_n, axis=-1, keepdims=True) + jnp.exp2(sink - m)
    p_cb = p_c.astype(BF16)
    yield
    o = jnp.dot(p_n.astype(BF16), vn.astype(BF16), preferred_element_type=F32)
    for j in range(SEQ_PER_STEP):
        oj = lax.dot_general(p_cb, cv_ref[j].astype(BF16), (((1,), (1,)), ((), ())), preferred_element_type=F32)
        o = o + jnp.where(row_seq == j, oj, 0.0)
    yield
    o = o / den
    outs = []
    for c in range(4):
        lo, hi = o[32 * c:32 * c + 16], o[32 * c + 16:32 * c + 32]
        if c < 2:
            outs.append(jnp.where(lane < HEAD_DIM, lo, pltpu.roll(hi, HEAD_DIM, 1)))
        else:
            outs.append(jnp.where(lane < HEAD_DIM, pltpu.roll(lo, HEAD_DIM, 1), hi))
    o_ref[rows, :] = jnp.concatenate(outs, axis=1).astype(o_ref.dtype)
    yield
    for j in range(SEQ_PER_STEP):
        shift = (WINDOW - DEC_SEQ - DEC_SEQ * j) % WINDOW
        for new_t, cache_ref, dst in ((kn_t, ck_ref, nk_ref), (vn_t, cv_ref, nv_ref)):
            dst[j] = jnp.where(col >= WINDOW - DEC_SEQ, pltpu.roll(new_t, shift, 1),
                               pltpu.roll(cache_ref[j], WINDOW - DEC_SEQ, 1))


def _attn_sample(q, kvn, sink, cache_kt, cache_vt, layer, prev_k, prev_v):
    depth, db = cache_kt.shape[:2]
    seqs = SAMPLE_GROUPS * SEQ_PER_STEP
    cache_spec = pl.BlockSpec((None, seqs, KV_W, WINDOW), lambda s: (layer, s, 0, 0))
    rows = lambda w: pl.BlockSpec((seqs * DEC_SEQ, w), lambda s: (s, 0))
    n_in = 5

    def call(extra, aliases):
        return pl.pallas_call(
            _attn_sample_kernel,
            grid=(db // seqs,),
            in_specs=[pl.BlockSpec(memory_space=pltpu.SMEM), rows(ATT_W), rows(2 * KV_W), cache_spec, cache_spec]
            + [pl.BlockSpec(memory_space=pl.ANY)] * len(extra),
            out_specs=[rows(ATT_W), cache_spec, cache_spec],
            out_shape=[jax.ShapeDtypeStruct((db * DEC_SEQ, ATT_W), BF16),
                       jax.ShapeDtypeStruct((depth, db, KV_W, WINDOW), F32),
                       jax.ShapeDtypeStruct((depth, db, KV_W, WINDOW), F32)],
            input_output_aliases=aliases,
            compiler_params=_cparams("parallel"),
            name="attn_sample",
        )(sink, q, kvn, cache_kt, cache_vt, *extra)

    return _stacked_out(call, layer, {1: prev_k, 2: prev_v}, n_in)


def _hgrn_out(o, g_act, normw):
    ms = jnp.mean(o * o, axis=-1, keepdims=True)
    return o * lax.rsqrt(ms + RMS_EPS) * normw * g_act


def _cumsum_rows(x, tri):
    hi = x.astype(BF16)
    lo = (x - hi.astype(F32)).astype(BF16)
    dot = lambda t: jnp.dot(tri, t, preferred_element_type=F32)
    return dot(hi) + dot(lo)


def _heads(t):
    return [t[:, h * HG_D:(h + 1) * HG_D] for h in range(HG_HEADS)]


def _level_reference(b, n, row):
    if n == 2:
        return jnp.where(jnp.bitwise_and(row, 1) == 0, b, pltpu.roll(b, 1, 0))
    if n == 4:
        m4 = jnp.bitwise_and(row, 3)
        return jnp.where(m4 == 0, pltpu.roll(b, CHUNK - 1, 0),
                         jnp.where(m4 == 1, b, jnp.where(m4 == 2, pltpu.roll(b, 1, 0), pltpu.roll(b, 2, 0))))
    bounds = [m * n + n // 2 - 1 for m in range(CHUNK // n)]
    pieces = [jnp.broadcast_to(b[t:t + 1], (n, b.shape[1])) for t in bounds]
    return pieces[0] if len(pieces) == 1 else jnp.concatenate(pieces, axis=0)


def _scores(q, k, e_q, e_k):
    qs, ks = _heads(q * e_q.astype(BF16)), _heads(k * e_k.astype(BF16))
    return [lax.dot_general(a, c, (((1,), (1,)), ((), ())), preferred_element_type=F32) for a, c in zip(qs, ks)]


def _intra_chunk_scores(q, k, b, lvl, fast):
    if fast:
        b_mid, b_end = b[HALF - 1:HALF], b[CHUNK - 1:CHUNK]
        r = jnp.concatenate([jnp.broadcast_to(0.5 * b_mid, (HALF, HG_W)),
                             jnp.broadcast_to(0.5 * (b_mid + b_end), (HALF, HG_W))], axis=0)
        e = jnp.exp(b - r)
        e6 = jnp.exp(-jnp.abs(b - b_mid))
        return [jnp.where(lvl == 6, across, jnp.where(lvl >= -1, within, 0.0))
                for within, across in zip(_scores(q, k, e, 1.0 / e), _scores(q, k, e6, e6))]
    row = lax.broadcasted_iota(jnp.int32, (CHUNK, HG_W), 0)
    atts = [jnp.where(lvl == -1, jnp.sum(qk, axis=-1, keepdims=True), 0.0) for qk in _heads((q * k).astype(F32))]
    for level in range(7):
        e = jnp.exp(-jnp.abs(b - _level_reference(b, 2 << level, row)))
        atts = [jnp.where(lvl == level, p, att) for p, att in zip(_scores(q, k, e, e), atts)]
    return atts


def _hgrn_chunks(fast, hq_ref, hk_ref, lf_ref, hv_ref, hg_ref, normw_ref, o_ref, st_ref):
    row = lax.broadcasted_iota(jnp.int32, (CHUNK, CHUNK), 0)
    col = lax.broadcasted_iota(jnp.int32, (CHUNK, CHUNK), 1)
    tri = jnp.where(row >= col, 1.0, 0.0).astype(BF16)
    lvl = jnp.where(row >= col, 31 - lax.clz(jnp.bitwise_xor(row, col)), -2)
    subs = [slice(c * CHUNK, (c + 1) * CHUNK) for c in range(hq_ref.shape[0] // CHUNK)]
    q, k, v = ([ref[rs] for rs in subs] for ref in (hq_ref, hk_ref, hv_ref))
    b = [_cumsum_rows(lf_ref[rs], tri) for rs in subs]
    b_end = [t[CHUNK - 1:CHUNK] for t in b]
    atts = [_intra_chunk_scores(qc, kc, bc, lvl, fast) for qc, kc, bc in zip(q, k, b)]
    q_in = [_heads(qc * jnp.exp(bc).astype(BF16)) for qc, bc in zip(q, b)]
    k_out = [_heads(kc.astype(F32) * jnp.exp(be - bc)) for kc, bc, be in zip(k, b, b_end)]
    k_t = [[jnp.transpose(t).astype(BF16) for t in per_head] for per_head in k_out]
    decay = [[jnp.transpose(jnp.broadcast_to(t, (HG_D, HG_D))) for t in _heads(jnp.exp(be))] for be in b_end]
    vs = [_heads(vc) for vc in v]
    local = [[jnp.dot(a.astype(BF16), vh, preferred_element_type=F32) for a, vh in zip(ac, vc)]
             for ac, vc in zip(atts, vs)]
    grow = [[jnp.dot(kt, vh, preferred_element_type=F32) for kt, vh in zip(kc, vc)] for kc, vc in zip(k_t, vs)]
    state = [st_ref[0, h] for h in range(HG_HEADS)]
    for c, rs in enumerate(subs):
        gs = _heads(hg_ref[rs])
        outs = [_hgrn_out(jnp.dot(q_in[c][h], state[h].astype(BF16), preferred_element_type=F32) + local[c][h],
                          gs[h], normw_ref[...]) for h in range(HG_HEADS)]
        o_ref[rs] = jnp.concatenate(outs, axis=1).astype(o_ref.dtype)
        state = [decay[c][h] * state[h] + grow[c][h] for h in range(HG_HEADS)]
    for h in range(HG_HEADS):
        st_ref[0, h] = state[h]


def _hgrn_prompt_kernel(ok_ref, *refs):
    st_ref = refs[-1]

    @pl.when(pl.program_id(1) == 0)
    def _():
        st_ref[...] = jnp.zeros_like(st_ref)

    first = (pl.program_id(0) * pl.num_programs(1) + pl.program_id(1)) * HGRN_CHUNKS
    ok = ok_ref[first] != 0
    for c in range(1, HGRN_CHUNKS):
        ok = jnp.logical_and(ok, ok_ref[first + c] != 0)

    @pl.when(ok)
    def _():
        _hgrn_chunks(True, *refs)

    @pl.when(jnp.logical_not(ok))
    def _():
        _hgrn_chunks(False, *refs)


def _hgrn_prompt(chunk_ok, hq, hk, lf, hv, hg, normw, batch, seq):
    nc = seq // (HGRN_CHUNKS * CHUNK)
    assert seq % (HGRN_CHUNKS * CHUNK) == 0
    rows = pl.BlockSpec((HGRN_CHUNKS * CHUNK, HG_W), lambda b, i, ok: (b * nc + i, 0))
    return pl.pallas_call(
        _hgrn_prompt_kernel,
        grid_spec=pltpu.PrefetchScalarGridSpec(
            num_scalar_prefetch=1,
            grid=(batch, nc),
            in_specs=[rows, rows, rows, rows, rows, pl.BlockSpec((1, HG_D), lambda b, i, ok: (0, 0))],
            out_specs=[rows, pl.BlockSpec((1, HG_HEADS, HG_D, HG_D), lambda b, i, ok: (b, 0, 0, 0))]),
        out_shape=[jax.ShapeDtypeStruct((batch * seq, HG_W), BF16),
                   jax.ShapeDtypeStruct((batch, HG_HEADS, HG_D, HG_D), F32)],
        compiler_params=_cparams("parallel", "arbitrary"),
        name="hgrn_prompt",
    )(chunk_ok, hq, hk, lf, hv, hg, normw)


def _hgrn_sample_kernel(hq_ref, hk_ref, lf_ref, hv_ref, hg_ref, normw_ref, st_ref, *rest):
    o_ref, ns_ref = rest[-2:]
    _interleave(_hgrn_sample_group(grp, hq_ref, hk_ref, lf_ref, hv_ref, hg_ref, normw_ref, st_ref, o_ref, ns_ref)
                for grp in range(SAMPLE_GROUPS))


def _hgrn_sample_group(grp, hq_ref, hk_ref, lf_ref, hv_ref, hg_ref, normw_ref, st_ref, o_ref, ns_ref):
    rows = slice(grp * ROWS_PER_STEP, (grp + 1) * ROWS_PER_STEP)
    st_ref, ns_ref = (r.at[grp * SEQ_PER_STEP:(grp + 1) * SEQ_PER_STEP] for r in (st_ref, ns_ref))
    q, k, v = (r[rows, :].astype(F32) for r in (hq_ref, hk_ref, hv_ref))
    logf = lf_ref[rows, :]
    row = lax.broadcasted_iota(jnp.int32, (ROWS_PER_STEP, HG_W), 0)
    tok = jnp.bitwise_and(row, DEC_SEQ - 1)
    seq_of_row = jnp.right_shift(row[:, :HG_D], 2)
    down = lambda t, d: pltpu.roll(t, d, 0)
    up = lambda t, d: pltpu.roll(t, ROWS_PER_STEP - d, 0)
    b = logf
    for d in range(1, DEC_SEQ):
        b = b + jnp.where(tok >= d, down(logf, d), 0.0)
    b_end = jnp.where(tok == 3, b, jnp.where(tok == 2, up(b, 1), jnp.where(tok == 1, up(b, 2), up(b, 3))))
    qe = (q * jnp.exp(b)).astype(BF16)
    kd = k * jnp.exp(b_end - b)
    e_end = jnp.exp(b_end)
    intra = [jnp.zeros((ROWS_PER_STEP, HG_D), F32) for _ in range(HG_HEADS)]
    for d in range(DEC_SEQ):
        kd_, bd_, vd_ = (k, b, v) if d == 0 else (down(k, d), down(b, d), down(v, d))
        w = jnp.where(tok >= d, q * kd_ * jnp.exp(jnp.minimum(b - bd_, 0.0)), 0.0)
        for h in range(HG_HEADS):
            sl = slice(h * HG_D, (h + 1) * HG_D)
            intra[h] = intra[h] + jnp.sum(w[:, sl], axis=-1, keepdims=True) * vd_[:, sl]
    pad = jnp.zeros((HG_D - 2 * ROWS_PER_STEP, HG_D), F32)
    colh = lax.broadcasted_iota(jnp.int32, (HG_D, HG_D), 1)
    normw = normw_ref[...]
    yield
    for h in range(HG_HEADS):
        sl = slice(h * HG_D, (h + 1) * HG_D)
        vb = jnp.concatenate([v[:, sl], jnp.zeros((HG_D - ROWS_PER_STEP, HG_D), F32)], axis=0).astype(BF16)
        tt = jnp.transpose(jnp.concatenate([kd[:, sl], e_end[:, sl], pad], axis=0))
        inter = jnp.zeros((ROWS_PER_STEP, HG_D), F32)
        for j in range(SEQ_PER_STEP):
            s0 = st_ref[j, h]
            oi = jnp.dot(qe[:, sl], s0.astype(BF16), preferred_element_type=F32)
            inter = jnp.where(seq_of_row == j, oi, inter)
            kd_t = jnp.where(jnp.right_shift(colh, 2) == j, tt, 0.0).astype(BF16)
            e_col = ROWS_PER_STEP + DEC_SEQ * j
            decay = jnp.broadcast_to(tt[:, e_col:e_col + 1], (HG_D, HG_D))
            ns_ref[j, h] = decay * s0 + jnp.dot(kd_t, vb, preferred_element_type=F32)
        o_ref[rows, sl] = _hgrn_out(inter + intra[h], hg_ref[rows, sl], normw).astype(o_ref.dtype)
        yield


def _hgrn_sample(hq, hk, lf, hv, hg, normw, state, layer, prev_state):
    depth, db = state.shape[:2]
    seqs = SAMPLE_GROUPS * SEQ_PER_STEP
    rows = pl.BlockSpec((seqs * DEC_SEQ, HG_W), lambda s: (s, 0))
    st_spec = pl.BlockSpec((None, seqs, HG_HEADS, HG_D, HG_D), lambda s: (layer, s, 0, 0, 0))
    n_in = 7

    def call(extra, aliases):
        return pl.pallas_call(
            _hgrn_sample_kernel,
            grid=(db // seqs,),
            in_specs=[rows, rows, rows, rows, rows, pl.BlockSpec((1, HG_D), lambda s: (0, 0)), st_spec]
            + [pl.BlockSpec(memory_space=pl.ANY)] * len(extra),
            out_specs=[rows, st_spec],
            out_shape=[jax.ShapeDtypeStruct((db * DEC_SEQ, HG_W), BF16), jax.ShapeDtypeStruct(state.shape, F32)],
            input_output_aliases=aliases,
            compiler_params=_cparams("parallel"),
            name="hgrn_sample",
        )(hq, hk, lf, hv, hg, normw, state, *extra)

    return _stacked_out(call, layer, {1: prev_state}, n_in)


def _layer_norm(y, g, b):
    mu = jnp.mean(y, axis=-1, keepdims=True)
    yc = y - mu
    var = jnp.mean(yc * yc, axis=-1, keepdims=True)
    return yc * lax.rsqrt(var + LN_EPS) * g + b


def _post_kernel(alpha, n_cast, a_ref, h_ref, g_ref, x_ref, wua_ref, wuh_ref, wo_ref, l1g_ref, l1b_ref,
                 wf1_ref, wf2_ref, l2g_ref, l2b_ref, *refs):
    o_ref = refs[n_cast]
    _cast_rows(refs[:n_cast], refs[n_cast + 1:])
    groups = [slice(r0, r0 + POST_GROUP) for r0 in range(0, x_ref.shape[0], POST_GROUP)]
    dot = lambda lhs, w_ref: jnp.dot(lhs, w_ref[...], preferred_element_type=F32)
    ua = [dot(a_ref[rs], wua_ref) for rs in groups]
    uh = [dot(h_ref[rs], wuh_ref) for rs in groups]
    gate = jax.nn.sigmoid
    merged = [(gate(g_ref[rs, :D_MODEL]) * a + gate(g_ref[rs, D_MODEL:]) * h).astype(BF16)
              for rs, a, h in zip(groups, ua, uh)]
    m = [dot(t, wo_ref) for t in merged]
    x1 = [_layer_norm(alpha * x_ref[rs] + t, l1g_ref[...], l1b_ref[...]) for rs, t in zip(groups, m)]
    hid = [jnp.square(jnp.maximum(dot(t.astype(BF16), wf1_ref), 0.0)).astype(BF16) for t in x1]
    ff = [dot(t, wf2_ref) for t in hid]
    for rs, t, f in zip(groups, x1, ff):
        o_ref[rs] = _layer_norm(alpha * t + f, l2g_ref[...], l2b_ref[...])


def _post(a, h, g, x, wb, p, layer, alpha, cast=(), cast_layer=0):
    t = x.shape[0]
    tm = min(POST_ROWS, t)
    assert t % tm == 0
    rows = lambda w: pl.BlockSpec((tm, w), lambda i: (i, 0))
    vec = lambda w: _resident((None, 1, w), lambda i: (layer, 0, 0))
    mat = lambda w: _resident(w.shape, lambda i: (0, 0))
    cast_src, cast_dst, cast_shapes = _cast_specs(cast, cast_layer, t // tm)
    outs = pl.pallas_call(
        functools.partial(_post_kernel, alpha, len(cast)),
        grid=(t // tm,),
        in_specs=[rows(ATT_W), rows(HG_W), rows(GATE_W), rows(D_MODEL), mat(wb[0]), mat(wb[1]), mat(wb[2]),
                  vec(D_MODEL), vec(D_MODEL), mat(wb[3]), mat(wb[4]), vec(D_MODEL), vec(D_MODEL)] + cast_src,
        out_specs=[rows(D_MODEL)] + cast_dst,
        out_shape=[jax.ShapeDtypeStruct((t, D_MODEL), F32)] + cast_shapes,
        compiler_params=_cparams("parallel"),
        name="post",
    )(a, h, g, x, wb[0], wb[1], wb[2], p["ln1_g"], p["ln1_b"], wb[3], wb[4], p["ln2_g"], p["ln2_b"], *cast)
    return outs[0], outs[1:]


def kernel(x_prompt, x_sample, cache_k, cache_v, state_hgrn, w_in, b_gate, attn_sink, hgrn_lb_logits, hgrn_norm_w,
           w_up_attn, w_up_hgrn, w_out, ln1_g, ln1_b, w_ff1, w_ff2, ln2_g, ln2_b):
    depth = w_in.shape[0]
    batch, seq, _ = x_prompt.shape
    db, dec_seq, _ = x_sample.shape
    assert dec_seq == DEC_SEQ and seq % CHUNK == 0 and db % (SAMPLE_GROUPS * SEQ_PER_STEP) == 0
    assert cache_k.shape[2:] == (WINDOW, KV_HEADS, HEAD_DIM)
    alpha = (2 * depth) ** 0.25

    vec = lambda t: t.reshape(depth, 1, t.shape[-1])
    b_gate3 = vec(b_gate)
    p = dict(ln1_g=vec(ln1_g), ln1_b=vec(ln1_b), ln2_g=vec(ln2_g), ln2_b=vec(ln2_b))
    mixer_w = (w_up_attn, w_up_hgrn, w_out, w_ff1, w_ff2)
    w_in_b = w_in[0].astype(BF16)
    to_t = lambda c: jnp.transpose(c, (0, 1, 3, 4, 2)).reshape(depth, db, KV_W, WINDOW)
    from_t = lambda c: jnp.transpose(c.reshape(depth, db, KV_HEADS, HEAD_DIM, WINDOW), (0, 1, 4, 2, 3))
    ck_t, cv_t = to_t(cache_k), to_t(cache_v)

    xp = x_prompt.reshape(batch * seq, D_MODEL)
    xs = x_sample.reshape(db * DEC_SEQ, D_MODEL)
    pk, pv, ps = [], [], []
    sk = sv = ss = None
    for l in range(depth):
        sink = attn_sink[l].reshape(1, N_HEADS)
        normw = hgrn_norm_w[l].reshape(1, HG_D)
        q, kv, ks, vs, hq, hk, lf, hv, hg, g, chunk_ok, *wb = _in_proj(xp, w_in_b, b_gate3, hgrn_lb_logits, l,
                                                                       cast=mixer_w)
        a = _attn_prompt(q, ks, vs, sink, batch, seq)
        h, st = _hgrn_prompt(chunk_ok[:, 0, 0], hq, hk, lf, hv, hg, normw, batch, seq)
        xp, next_w_in = _post(a, h, g, xp, wb, p, l, alpha, cast=(w_in,) if l + 1 < depth else (), cast_layer=l + 1)
        win = kv.reshape(batch, seq, 2 * KV_W)[:, seq - WINDOW:].reshape(batch, WINDOW, 2, KV_HEADS, HEAD_DIM)
        pk.append(win[:, :, 0])
        pv.append(win[:, :, 1])
        ps.append(st)
        q, kvn, _, _, hq, hk, lf, hv, hg, g, _ = _in_proj(xs, w_in_b, b_gate3, hgrn_lb_logits, l)
        a, sk, sv = _attn_sample(q, kvn, sink, ck_t, cv_t, l, sk, sv)
        h, ss = _hgrn_sample(hq, hk, lf, hv, hg, normw, state_hgrn, l, ss)
        xs, _ = _post(a, h, g, xs, wb, p, l, alpha)
        if next_w_in:
            w_in_b = next_w_in[0]
    return (xp.reshape(batch, seq, D_MODEL), xs.reshape(db, DEC_SEQ, D_MODEL), jnp.stack(pk), jnp.stack(pv),
            jnp.stack(ps), from_t(sk), from_t(sv), ss)
```

```python
import functools

import jax
import jax.numpy as jnp
from jax import lax
from jax.experimental import pallas as pl
from jax.experimental.pallas import tpu as pltpu

F32 = jnp.float32
BF16 = jnp.bfloat16

D_MODEL = 1024
WINDOW = 128
N_HEADS = 8
KV_HEADS = 2
HEAD_DIM = 64
ATT_W = N_HEADS * HEAD_DIM
KV_W = KV_HEADS * HEAD_DIM
HG_HEADS = 4
HG_D = 128
HG_W = HG_HEADS * HG_D
GATE_W = 2 * D_MODEL
D_FF = 4 * D_MODEL
DEC_SEQ = 4
LN_EPS = 1e-5
RMS_EPS = 1e-6
NEG_BIG = -1e30
LB_FLOOR = 1e-30
LOG2E = 1.4426950408889634
QK_SCALE = HEAD_DIM ** -0.5 * LOG2E

AQ0, AK0, AV0, HQ0, HF0, HI0, HG0, GT0, IN_COLS = 0, 512, 640, 768, 1280, 1792, 2304, 2816, 4864
PK_Q, PK_KS, PK_VS, PK_HQ, PK_HK, PK_HV, PK_PARTS = 0, 1, 2, 3, 4, 5, 6
PK_COLS = 512

ATT_BLOCKS = 8
CHUNK = 128
HGRN_CHUNKS = 8
HALF = CHUNK // 2
FAST_DECAY_LIMIT = 120.0
FAST_Q_LIMIT = 1e9
SEQ_PER_STEP = 4
ROWS_PER_STEP = SEQ_PER_STEP * DEC_SEQ
SAMPLE_GROUPS = 8
IN_PROJ_ROWS = 512
POST_ROWS = 512
POST_GROUP = 256
V7X_VMEM_LIMIT = 56 * 1024 * 1024


def _cparams(*sem):
    return pltpu.CompilerParams(dimension_semantics=sem, vmem_limit_bytes=V7X_VMEM_LIMIT)


def _resident(shape, index_map):
    return pl.BlockSpec(shape, index_map, pipeline_mode=pl.Buffered(1))


def _stacked_out(call, layer, stacked, n_in):
    extra, aliases = [], {}
    if layer > 0:
        for out_idx, prev in stacked.items():
            aliases[n_in + len(extra)] = out_idx
            extra.append(prev)
    return call(extra, aliases)


def _lower_bound(logits_ref, layer):
    lg = logits_ref[...]
    e = jnp.exp(lg - jnp.max(lg, axis=0, keepdims=True))
    p = e / jnp.sum(e, axis=0, keepdims=True)
    lb = jnp.zeros((1, HG_W), F32)
    for i in range(1, layer + 1):
        lb = lb + p[i:i + 1]
    return lb


def _swish(z):
    return z * jax.nn.sigmoid(z)


def _forget_and_key(zf, lb):
    e = jnp.exp(-jnp.abs(zf))
    r = 1.0 / (1.0 + e)
    pos = zf >= 0
    sig_p = jnp.where(pos, r, e * r)
    sig_n = jnp.where(pos, e * r, r)
    logf = jnp.log(jnp.maximum(lb, LB_FLOOR) + (1.0 - lb) * sig_p)
    return logf, (1.0 - lb) * sig_n


def _cast_rows(src_refs, dst_refs):
    for src, dst in zip(src_refs, dst_refs):
        dst[...] = src[...].astype(BF16)


def _in_proj_kernel(layer, n_cast, x_ref, w_ref, bg_ref, logits_ref, *refs):
    cast_src, refs = refs[:n_cast], refs[n_cast:]
    (pk_ref, kv_ref, lf_ref, hg_ref, g_ref, ok_ref), refs = refs[:6], refs[6:]
    cast_dst, xb_ref = refs[:n_cast], refs[n_cast]
    _cast_rows(cast_src, cast_dst)
    part = lambda j: pk_ref.at[:, j * PK_COLS:(j + 1) * PK_COLS]
    outs = (part(PK_Q), kv_ref, part(PK_KS), part(PK_VS), part(PK_HQ), part(PK_HK), lf_ref, part(PK_HV), hg_ref,
            g_ref, ok_ref)
    for _ in _in_proj_stages(layer, x_ref, w_ref, bg_ref, logits_ref, outs, xb_ref):
        pass


def _in_proj_stages(layer, x_ref, w_ref, bg_ref, logits_ref, outs, xb_ref):
    q_ref, kv_ref, ks_ref, vs_ref, hq_ref, hk_ref, lf_ref, hv_ref, hg_ref, g_ref, ok_ref = outs
    xb = x_ref[...].astype(BF16)
    lb = _lower_bound(logits_ref, layer)
    chunks = x_ref.shape[0] // CHUNK
    q_ok = []
    decay_ok = []

    def put_q(z):
        q_ref[...] = (z * QK_SCALE).astype(BF16)

    def put_kv(z):
        kv_ref[...] = z
        ks_ref[...] = _split_heads(z[:, :KV_W])
        vs_ref[...] = _split_heads(z[:, KV_W:])

    def put_hq(z):
        hq = _swish(z)
        hq_ref[...] = hq.astype(BF16)
        for c in range(chunks):
            q_ok.append(jnp.max(jnp.abs(hq[c * CHUNK:(c + 1) * CHUNK]), keepdims=True) <= FAST_Q_LIMIT)

    def put_hf(z):
        logf, key = _forget_and_key(z, lb)
        lf_ref[...] = logf
        hk_ref[...] = key.astype(BF16)
        for lo in range(0, chunks * CHUNK, HALF):
            total = jnp.sum(logf[lo:lo + HALF], axis=0, keepdims=True)
            decay_ok.append(jnp.min(total, keepdims=True) >= -FAST_DECAY_LIMIT)

    def put_hv(z):
        hv_ref[...] = z.astype(BF16)

    def put_hg(z):
        hg_ref[...] = _swish(z)

    def put_gates(lo):
        def put(z):
            g_ref[:, lo:lo + 512] = z + bg_ref[:, lo:lo + 512]
        return put

    gates = [(GT0 + lo, GT0 + lo + 512, put_gates(lo)) for lo in range(0, GATE_W, 512)]
    stages = [(HF0, HI0, put_hf), gates[0], (HQ0, HF0, put_hq), gates[1], (AK0, HQ0, put_kv), gates[2],
              (HG0, GT0, put_hg), (AQ0, AK0, put_q), (HI0, HG0, put_hv), gates[3]]
    pending = None
    xb_ref[...] = xb
    for lo, hi, put in stages:
        z = jnp.dot(xb_ref[...], w_ref[:, lo:hi], preferred_element_type=F32)
        if pending is not None:
            pending[0](pending[1])
        pending = (put, z)
        yield
    pending[0](pending[1])
    for c in range(chunks):
        ok = jnp.logical_and(q_ok[c], jnp.logical_and(decay_ok[2 * c], decay_ok[2 * c + 1]))
        ok_ref[c] = jnp.broadcast_to(jnp.where(ok, 1, 0), (1, 128)).astype(jnp.int32)


def _split_heads(t):
    low = lax.broadcasted_iota(jnp.int32, t.shape, 1) < HEAD_DIM
    h0_lo = jnp.where(low, t, 0.0)
    h1_hi = jnp.where(low, 0.0, t)
    blocks = [h0_lo, pltpu.roll(h0_lo, HEAD_DIM, 1), pltpu.roll(h1_hi, HEAD_DIM, 1), h1_hi]
    return jnp.concatenate(blocks, axis=1).astype(BF16)


def _cast_specs(weights, layer, steps):
    src, dst, shapes = [], [], []
    for w in weights:
        _, r, c = w.shape
        slab = r // steps
        assert r % steps == 0 and (slab % 16 == 0 or steps == 1)
        last = steps - 1
        src.append(pl.BlockSpec((None, slab, c), lambda i, layer=layer, last=last: (layer, jnp.minimum(i, last), 0)))
        dst.append(pl.BlockSpec((slab, c), lambda i, last=last: (jnp.minimum(i, last), 0)))
        shapes.append(jax.ShapeDtypeStruct((r, c), BF16))
    return src, dst, shapes


def _in_proj(x, w_b, b_gate, logits, layer, cast=()):
    t = x.shape[0]
    tm = min(IN_PROJ_ROWS, t)
    assert t % tm == 0 and tm % CHUNK == 0
    rows = lambda w: pl.BlockSpec((tm, w), lambda i: (i, 0))
    out = lambda w, dt: jax.ShapeDtypeStruct((t, w), dt)
    cast_src, cast_dst, cast_shapes = _cast_specs(cast, layer, t // tm)
    return pl.pallas_call(
        functools.partial(_in_proj_kernel, layer, len(cast)),
        grid=(t // tm,),
        in_specs=[rows(D_MODEL),
                  _resident((D_MODEL, IN_COLS), lambda i: (0, 0)),
                  _resident((None, 1, GATE_W), lambda i: (layer, 0, 0)),
                  _resident(logits.shape, lambda i: (0, 0))] + cast_src,
        out_specs=[rows(PK_PARTS * PK_COLS), rows(2 * KV_W), rows(HG_W), rows(HG_W), rows(GATE_W),
                   pl.BlockSpec((tm // CHUNK, 1, 128), lambda i: (i, 0, 0))] + cast_dst,
        out_shape=[out(PK_PARTS * PK_COLS, BF16), out(2 * KV_W, F32), out(HG_W, F32), out(HG_W, F32),
                   out(GATE_W, F32), jax.ShapeDtypeStruct((t // CHUNK, 1, 128), jnp.int32)] + cast_shapes,
        scratch_shapes=[pltpu.VMEM((tm, D_MODEL), BF16)],
        compiler_params=_cparams("parallel"),
        name="in_proj",
    )(x, w_b, b_gate, logits, *cast)


def _attn_block(n, sink_ref, bias_ref, first, q_ref, kp_ref, kc_ref, vp_ref, vc_ref, o_ref):
    r = WINDOW
    top = lax.broadcasted_iota(jnp.int32, (2 * r, 1), 0) < r
    combos = [(g, half) for g in range(KV_HEADS) for half in range(2)]
    lanes = lambda g, half: slice((2 * g + half) * KV_W, (2 * g + half + 1) * KV_W)
    sink = [jnp.where(top, sink_ref[0, 4 * g + half], sink_ref[0, 4 * g + 2 + half]) * LOG2E for g, half in combos]
    own = slice(n * r, (n + 1) * r)

    def kv_rows(prev_ref, cur_ref, g, half):
        before = prev_ref[:, lanes(g, half)] if n == 0 else cur_ref[(n - 1) * r:n * r, lanes(g, half)]
        return jnp.concatenate([before, cur_ref[own, lanes(g, half)]], axis=0)

    qg = [jnp.concatenate([q_ref[own, 2 * g * 128:(2 * g + 1) * 128],
                           q_ref[own, (2 * g + 1) * 128:(2 * g + 2) * 128]], axis=0) for g in range(KV_HEADS)]
    b = bias_ref[first] if n == 0 else bias_ref[1]
    bias = jnp.concatenate([b, b], axis=0)
    s = [lax.dot_general(qg[g], kv_rows(kp_ref, kc_ref, g, half), (((1,), (1,)), ((), ())),
                         preferred_element_type=F32) + bias for g, half in combos]
    yield
    m = [jnp.maximum(jnp.max(t, axis=-1, keepdims=True), sk) for t, sk in zip(s, sink)]
    p = [jnp.exp2(t - mx) for t, mx in zip(s, m)]
    yield
    den = [jnp.sum(t, axis=-1, keepdims=True) + jnp.exp2(sk - mx) for t, sk, mx in zip(p, sink, m)]
    o = [jnp.dot(t.astype(BF16), kv_rows(vp_ref, vc_ref, g, half), preferred_element_type=F32) / d
         for t, d, (g, half) in zip(p, den, combos)]
    yield
    cols = []
    for g in range(KV_HEADS):
        acc = o[2 * g] + o[2 * g + 1]
        cols += [acc[:r], acc[r:]]
    o_ref[own, :] = jnp.concatenate(cols, axis=1).astype(o_ref.dtype)


def _band_bias():
    qi = lax.broadcasted_iota(jnp.int32, (WINDOW, 2 * WINDOW), 0)
    kj = lax.broadcasted_iota(jnp.int32, (WINDOW, 2 * WINDOW), 1)
    band = jnp.logical_and(kj >= qi, kj <= qi + WINDOW)
    first = jnp.logical_and(band, kj >= WINDOW)
    return jnp.where(jnp.stack([first, band]), 0.0, NEG_BIG).astype(F32)


def _attn_prompt_kernel(sink_ref, bias_ref, q_ref, kp_ref, kc_ref, vp_ref, vc_ref, o_ref):
    first = jnp.minimum(pl.program_id(1), 1)
    _interleave(_attn_block(n, sink_ref, bias_ref, first, q_ref, kp_ref, kc_ref, vp_ref, vc_ref, o_ref)
                for n in range(ATT_BLOCKS))


def _attn_prompt(pk, sink, batch, seq):
    steps = seq // (ATT_BLOCKS * WINDOW)
    assert seq % (ATT_BLOCKS * WINDOW) == 0
    cur = lambda part: pl.BlockSpec((ATT_BLOCKS * WINDOW, PK_COLS), lambda b, i: (b * steps + i, part))
    prev = lambda part: pl.BlockSpec(
        (WINDOW, PK_COLS), lambda b, i: ((b * steps + i) * ATT_BLOCKS - jnp.minimum(i, 1), part))
    return pl.pallas_call(
        _attn_prompt_kernel,
        grid=(batch, steps),
        in_specs=[pl.BlockSpec(memory_space=pltpu.SMEM),
                  _resident((2, WINDOW, 2 * WINDOW), lambda b, i: (0, 0, 0)),
                  cur(PK_Q), prev(PK_KS), cur(PK_KS), prev(PK_VS), cur(PK_VS)],
        out_specs=pl.BlockSpec((ATT_BLOCKS * WINDOW, ATT_W), lambda b, i: (b * steps + i, 0)),
        out_shape=jax.ShapeDtypeStruct((batch * seq, ATT_W), BF16),
        compiler_params=_cparams("parallel", "parallel"),
        name="attn_prompt",
    )(sink, _band_bias(), pk, pk, pk, pk, pk)


def _interleave(stage_generators):
    pending = list(stage_generators)
    while pending:
        for gen in list(pending):
            if next(gen, StopIteration) is StopIteration:
                pending.remove(gen)


def _attn_sample_kernel(sink_ref, q_ref, kvn_ref, ck_ref, cv_ref, *rest):
    o_ref, nk_ref, nv_ref = rest[-3:]
    _interleave(_attn_sample_group(grp, sink_ref, q_ref, kvn_ref, ck_ref, cv_ref, o_ref, nk_ref, nv_ref)
                for grp in range(SAMPLE_GROUPS))


def _attn_sample_group(grp, sink_ref, q_ref, kvn_ref, ck_ref, cv_ref, o_ref, nk_ref, nv_ref):
    rows = slice(grp * ROWS_PER_STEP, (grp + 1) * ROWS_PER_STEP)
    ck_ref, cv_ref, nk_ref, nv_ref = (r.at[grp * SEQ_PER_STEP:(grp + 1) * SEQ_PER_STEP]
                                      for r in (ck_ref, cv_ref, nk_ref, nv_ref))
    lane = lax.broadcasted_iota(jnp.int32, (ROWS_PER_STEP, KV_W), 1)
    q16 = q_ref[rows, :].astype(F32)
    blocks = []
    for h in range(N_HEADS):
        c, half, g = h // 2, h % 2, h // (N_HEADS // KV_HEADS)
        own = lane >= HEAD_DIM if half else lane < HEAD_DIM
        qh = jnp.where(own, q16[:, c * 128:(c + 1) * 128], 0.0)
        blocks.append(qh if half == g else pltpu.roll(qh, HEAD_DIM, 1))
    wt = jnp.concatenate(blocks, axis=0).astype(BF16)

    row = lax.broadcasted_iota(jnp.int32, (WINDOW, WINDOW), 0)
    col = lax.broadcasted_iota(jnp.int32, (WINDOW, WINDOW), 1)
    tok = jnp.bitwise_and(row, DEC_SEQ - 1)
    row_seq = jnp.bitwise_and(jnp.right_shift(row, 2), SEQ_PER_STEP - 1)
    pad = jnp.zeros((WINDOW - ROWS_PER_STEP, KV_W), F32)
    kn = jnp.concatenate([kvn_ref[rows, :KV_W], pad], axis=0)
    vn = jnp.concatenate([kvn_ref[rows, KV_W:], pad], axis=0)
    kn_t = jnp.transpose(kn)
    vn_t = jnp.transpose(vn)
    yield

    s_c = jnp.zeros((WINDOW, WINDOW), F32)
    for j in range(SEQ_PER_STEP):
        sj = jnp.dot(wt, ck_ref[j].astype(BF16), preferred_element_type=F32)
        s_c = jnp.where(row_seq == j, sj, s_c)
    s_n = jnp.dot(wt, kn_t.astype(BF16), preferred_element_type=F32)
    yield
    s_c = jnp.where(col >= tok, s_c, NEG_BIG)
    new_ok = jnp.logical_and(jnp.right_shift(col, 2) == row_seq, jnp.bitwise_and(col, DEC_SEQ - 1) <= tok)
    s_n = jnp.where(new_ok, s_n, NEG_BIG)
    head = jnp.right_shift(lax.broadcasted_iota(jnp.int32, (WINDOW, 1), 0), 4)
    sink = jnp.zeros((WINDOW, 1), F32)
    for h in range(N_HEADS):
        sink = jnp.where(head == h, sink_ref[0, h] * LOG2E, sink)
    m = jnp.maximum(jnp.maximum(jnp.max(s_c, axis=-1, keepdims=True), jnp.max(s_n, axis=-1, keepdims=True)), sink)
    p_c = jnp.exp2(s_c - m)
    p_n = jnp.exp2(s_n - m)
    den = jnp.sum(p_c, axis=-1, keepdims=True) + jnp.sum(p_n, axis=-1, keepdims=True) + jnp.exp2(sink - m)
    p_cb = p_c.astype(BF16)
    yield
    o = jnp.dot(p_n.astype(BF16), vn.astype(BF16), preferred_element_type=F32)
    for j in range(SEQ_PER_STEP):
        oj = lax.dot_general(p_cb, cv_ref[j].astype(BF16), (((1,), (1,)), ((), ())), preferred_element_type=F32)
        o = o + jnp.where(row_seq == j, oj, 0.0)
    yield
    o = o / den
    outs = []
    for c in range(4):
        lo, hi = o[32 * c:32 * c + 16], o[32 * c + 16:32 * c + 32]
        if c < 2:
            outs.append(jnp.where(lane < HEAD_DIM, lo, pltpu.roll(hi, HEAD_DIM, 1)))
        else:
            outs.append(jnp.where(lane < HEAD_DIM, pltpu.roll(lo, HEAD_DIM, 1), hi))
    o_ref[rows, :] = jnp.concatenate(outs, axis=1).astype(o_ref.dtype)
    yield
    for j in range(SEQ_PER_STEP):
        shift = (WINDOW - DEC_SEQ - DEC_SEQ * j) % WINDOW
        for new_t, cache_ref, dst in ((kn_t, ck_ref, nk_ref), (vn_t, cv_ref, nv_ref)):
            dst[j] = jnp.where(col >= WINDOW - DEC_SEQ, pltpu.roll(new_t, shift, 1),
                               pltpu.roll(cache_ref[j], WINDOW - DEC_SEQ, 1))


def _attn_sample(pk, kvn, sink, cache_kt, cache_vt, layer, prev_k, prev_v):
    depth, db = cache_kt.shape[:2]
    seqs = SAMPLE_GROUPS * SEQ_PER_STEP
    cache_spec = pl.BlockSpec((None, seqs, KV_W, WINDOW), lambda s: (layer, s, 0, 0))
    rows = lambda w: pl.BlockSpec((seqs * DEC_SEQ, w), lambda s: (s, 0))
    n_in = 5

    def call(extra, aliases):
        return pl.pallas_call(
            _attn_sample_kernel,
            grid=(db // seqs,),
            in_specs=[pl.BlockSpec(memory_space=pltpu.SMEM), rows(ATT_W), rows(2 * KV_W), cache_spec, cache_spec]
            + [pl.BlockSpec(memory_space=pl.ANY)] * len(extra),
            out_specs=[rows(ATT_W), cache_spec, cache_spec],
            out_shape=[jax.ShapeDtypeStruct((db * DEC_SEQ, ATT_W), BF16),
                       jax.ShapeDtypeStruct((depth, db, KV_W, WINDOW), F32),
                       jax.ShapeDtypeStruct((depth, db, KV_W, WINDOW), F32)],
            input_output_aliases=aliases,
            compiler_params=_cparams("parallel"),
            name="attn_sample",
        )(sink, pk, kvn, cache_kt, cache_vt, *extra)

    return _stacked_out(call, layer, {1: prev_k, 2: prev_v}, n_in)


def _hgrn_out(o, g_act, normw):
    ms = jnp.mean(o * o, axis=-1, keepdims=True)
    return o * lax.rsqrt(ms + RMS_EPS) * normw * g_act


def _cumsum_rows(x, tri):
    hi = x.astype(BF16)
    lo = (x - hi.astype(F32)).astype(BF16)
    dot = lambda t: jnp.dot(tri, t, preferred_element_type=F32)
    return dot(hi) + dot(lo)


def _heads(t):
    return [t[:, h * HG_D:(h + 1) * HG_D] for h in range(HG_HEADS)]


def _level_reference(b, n, row):
    if n == 2:
        return jnp.where(jnp.bitwise_and(row, 1) == 0, b, pltpu.roll(b, 1, 0))
    if n == 4:
        m4 = jnp.bitwise_and(row, 3)
        return jnp.where(m4 == 0, pltpu.roll(b, CHUNK - 1, 0),
                         jnp.where(m4 == 1, b, jnp.where(m4 == 2, pltpu.roll(b, 1, 0), pltpu.roll(b, 2, 0))))
    bounds = [m * n + n // 2 - 1 for m in range(CHUNK // n)]
    pieces = [jnp.broadcast_to(b[t:t + 1], (n, b.shape[1])) for t in bounds]
    return pieces[0] if len(pieces) == 1 else jnp.concatenate(pieces, axis=0)


def _scores(q, k, e_q, e_k):
    qs, ks = _heads(q * e_q.astype(BF16)), _heads(k * e_k.astype(BF16))
    return [lax.dot_general(a, c, (((1,), (1,)), ((), ())), preferred_element_type=F32) for a, c in zip(qs, ks)]


def _intra_chunk_scores(q, k, b, lvl, fast):
    if fast:
        b_mid, b_end = b[HALF - 1:HALF], b[CHUNK - 1:CHUNK]
        r = jnp.concatenate([jnp.broadcast_to(0.5 * b_mid, (HALF, HG_W)),
                             jnp.broadcast_to(0.5 * (b_mid + b_end), (HALF, HG_W))], axis=0)
        e = jnp.exp(b - r)
        e6 = jnp.exp(-jnp.abs(b - b_mid))
        return [jnp.where(lvl == 6, across, jnp.where(lvl >= -1, within, 0.0))
                for within, across in zip(_scores(q, k, e, 1.0 / e), _scores(q, k, e6, e6))]
    row = lax.broadcasted_iota(jnp.int32, (CHUNK, HG_W), 0)
    atts = [jnp.where(lvl == -1, jnp.sum(qk, axis=-1, keepdims=True), 0.0) for qk in _heads((q * k).astype(F32))]
    for level in range(7):
        e = jnp.exp(-jnp.abs(b - _level_reference(b, 2 << level, row)))
        atts = [jnp.where(lvl == level, p, att) for p, att in zip(_scores(q, k, e, e), atts)]
    return atts


def _hgrn_chunks(fast, hq_ref, hk_ref, lf_ref, hv_ref, hg_ref, normw_ref, o_ref, st_ref):
    row = lax.broadcasted_iota(jnp.int32, (CHUNK, CHUNK), 0)
    col = lax.broadcasted_iota(jnp.int32, (CHUNK, CHUNK), 1)
    tri = jnp.where(row >= col, 1.0, 0.0).astype(BF16)
    lvl = jnp.where(row >= col, 31 - lax.clz(jnp.bitwise_xor(row, col)), -2)
    subs = [slice(c * CHUNK, (c + 1) * CHUNK) for c in range(hq_ref.shape[0] // CHUNK)]
    q, k, v = ([ref[rs] for rs in subs] for ref in (hq_ref, hk_ref, hv_ref))
    b = [_cumsum_rows(lf_ref[rs], tri) for rs in subs]
    b_end = [t[CHUNK - 1:CHUNK] for t in b]
    atts = [_intra_chunk_scores(qc, kc, bc, lvl, fast) for qc, kc, bc in zip(q, k, b)]
    q_in = [_heads(qc * jnp.exp(bc).astype(BF16)) for qc, bc in zip(q, b)]
    k_out = [_heads(kc.astype(F32) * jnp.exp(be - bc)) for kc, bc, be in zip(k, b, b_end)]
    k_t = [[jnp.transpose(t).astype(BF16) for t in per_head] for per_head in k_out]
    decay = [[jnp.transpose(jnp.broadcast_to(t, (HG_D, HG_D))) for t in _heads(jnp.exp(be))] for be in b_end]
    vs = [_heads(vc) for vc in v]
    local = [[jnp.dot(a.astype(BF16), vh, preferred_element_type=F32) for a, vh in zip(ac, vc)]
             for ac, vc in zip(atts, vs)]
    grow = [[jnp.dot(kt, vh, preferred_element_type=F32) for kt, vh in zip(kc, vc)] for kc, vc in zip(k_t, vs)]
    state = [st_ref[0, h] for h in range(HG_HEADS)]
    for c, rs in enumerate(subs):
        gs = _heads(hg_ref[rs])
        outs = [_hgrn_out(jnp.dot(q_in[c][h], state[h].astype(BF16), preferred_element_type=F32) + local[c][h],
                          gs[h], normw_ref[...]) for h in range(HG_HEADS)]
        o_ref[rs] = jnp.concatenate(outs, axis=1).astype(o_ref.dtype)
        state = [decay[c][h] * state[h] + grow[c][h] for h in range(HG_HEADS)]
    for h in range(HG_HEADS):
        st_ref[0, h] = state[h]


def _hgrn_prompt_kernel(ok_ref, *refs):
    st_ref = refs[-1]

    @pl.when(pl.program_id(1) == 0)
    def _():
        st_ref[...] = jnp.zeros_like(st_ref)

    first = (pl.program_id(0) * pl.num_programs(1) + pl.program_id(1)) * HGRN_CHUNKS
    ok = ok_ref[first] != 0
    for c in range(1, HGRN_CHUNKS):
        ok = jnp.logical_and(ok, ok_ref[first + c] != 0)

    @pl.when(ok)
    def _():
        _hgrn_chunks(True, *refs)

    @pl.when(jnp.logical_not(ok))
    def _():
        _hgrn_chunks(False, *refs)


def _hgrn_prompt(chunk_ok, pk, lf, hg, normw, batch, seq):
    nc = seq // (HGRN_CHUNKS * CHUNK)
    assert seq % (HGRN_CHUNKS * CHUNK) == 0
    part = lambda j: pl.BlockSpec((HGRN_CHUNKS * CHUNK, HG_W), lambda b, i, ok: (b * nc + i, j))
    rows = part(0)
    return pl.pallas_call(
        _hgrn_prompt_kernel,
        grid_spec=pltpu.PrefetchScalarGridSpec(
            num_scalar_prefetch=1,
            grid=(batch, nc),
            in_specs=[part(PK_HQ), part(PK_HK), rows, part(PK_HV), rows,
                      pl.BlockSpec((1, HG_D), lambda b, i, ok: (0, 0))],
            out_specs=[rows, pl.BlockSpec((1, HG_HEADS, HG_D, HG_D), lambda b, i, ok: (b, 0, 0, 0))]),
        out_shape=[jax.ShapeDtypeStruct((batch * seq, HG_W), BF16),
                   jax.ShapeDtypeStruct((batch, HG_HEADS, HG_D, HG_D), F32)],
        compiler_params=_cparams("parallel", "arbitrary"),
        name="hgrn_prompt",
    )(chunk_ok, pk, pk, lf, pk, hg, normw)


def _hgrn_sample_kernel(hq_ref, hk_ref, lf_ref, hv_ref, hg_ref, normw_ref, st_ref, *rest):
    o_ref, ns_ref = rest[-2:]
    _interleave(_hgrn_sample_group(grp, hq_ref, hk_ref, lf_ref, hv_ref, hg_ref, normw_ref, st_ref, o_ref, ns_ref)
                for grp in range(SAMPLE_GROUPS))


def _hgrn_sample_group(grp, hq_ref, hk_ref, lf_ref, hv_ref, hg_ref, normw_ref, st_ref, o_ref, ns_ref):
    rows = slice(grp * ROWS_PER_STEP, (grp + 1) * ROWS_PER_STEP)
    st_ref, ns_ref = (r.at[grp * SEQ_PER_STEP:(grp + 1) * SEQ_PER_STEP] for r in (st_ref, ns_ref))
    q, k, v = (r[rows, :].astype(F32) for r in (hq_ref, hk_ref, hv_ref))
    logf = lf_ref[rows, :]
    row = lax.broadcasted_iota(jnp.int32, (ROWS_PER_STEP, HG_W), 0)
    tok = jnp.bitwise_and(row, DEC_SEQ - 1)
    seq_of_row = jnp.right_shift(row[:, :HG_D], 2)
    down = lambda t, d: pltpu.roll(t, d, 0)
    up = lambda t, d: pltpu.roll(t, ROWS_PER_STEP - d, 0)
    b = logf
    for d in range(1, DEC_SEQ):
        b = b + jnp.where(tok >= d, down(logf, d), 0.0)
    b_end = jnp.where(tok == 3, b, jnp.where(tok == 2, up(b, 1), jnp.where(tok == 1, up(b, 2), up(b, 3))))
    qe = (q * jnp.exp(b)).astype(BF16)
    kd = k * jnp.exp(b_end - b)
    e_end = jnp.exp(b_end)
    intra = [jnp.zeros((ROWS_PER_STEP, HG_D), F32) for _ in range(HG_HEADS)]
    for d in range(DEC_SEQ):
        kd_, bd_, vd_ = (k, b, v) if d == 0 else (down(k, d), down(b, d), down(v, d))
        w = jnp.where(tok >= d, q * kd_ * jnp.exp(jnp.minimum(b - bd_, 0.0)), 0.0)
        for h in range(HG_HEADS):
            sl = slice(h * HG_D, (h + 1) * HG_D)
            intra[h] = intra[h] + jnp.sum(w[:, sl], axis=-1, keepdims=True) * vd_[:, sl]
    pad = jnp.zeros((HG_D - 2 * ROWS_PER_STEP, HG_D), F32)
    colh = lax.broadcasted_iota(jnp.int32, (HG_D, HG_D), 1)
    normw = normw_ref[...]
    yield
    for h in range(HG_HEADS):
        sl = slice(h * HG_D, (h + 1) * HG_D)
        vb = jnp.concatenate([v[:, sl], jnp.zeros((HG_D - ROWS_PER_STEP, HG_D), F32)], axis=0).astype(BF16)
        tt = jnp.transpose(jnp.concatenate([kd[:, sl], e_end[:, sl], pad], axis=0))
        inter = jnp.zeros((ROWS_PER_STEP, HG_D), F32)
        for j in range(SEQ_PER_STEP):
            s0 = st_ref[j, h]
            oi = jnp.dot(qe[:, sl], s0.astype(BF16), preferred_element_type=F32)
            inter = jnp.where(seq_of_row == j, oi, inter)
            kd_t = jnp.where(jnp.right_shift(colh, 2) == j, tt, 0.0).astype(BF16)
            e_col = ROWS_PER_STEP + DEC_SEQ * j
            decay = jnp.broadcast_to(tt[:, e_col:e_col + 1], (HG_D, HG_D))
            ns_ref[j, h] = decay * s0 + jnp.dot(kd_t, vb, preferred_element_type=F32)
        o_ref[rows, sl] = _hgrn_out(inter + intra[h], hg_ref[rows, sl], normw).astype(o_ref.dtype)
        yield


def _hgrn_sample(pk, lf, hg, normw, state, layer, prev_state):
    depth, db = state.shape[:2]
    seqs = SAMPLE_GROUPS * SEQ_PER_STEP
    part = lambda j: pl.BlockSpec((seqs * DEC_SEQ, HG_W), lambda s: (s, j))
    rows = part(0)
    st_spec = pl.BlockSpec((None, seqs, HG_HEADS, HG_D, HG_D), lambda s: (layer, s, 0, 0, 0))
    n_in = 7

    def call(extra, aliases):
        return pl.pallas_call(
            _hgrn_sample_kernel,
            grid=(db // seqs,),
            in_specs=[part(PK_HQ), part(PK_HK), rows, part(PK_HV), rows,
                      pl.BlockSpec((1, HG_D), lambda s: (0, 0)), st_spec]
            + [pl.BlockSpec(memory_space=pl.ANY)] * len(extra),
            out_specs=[rows, st_spec],
            out_shape=[jax.ShapeDtypeStruct((db * DEC_SEQ, HG_W), BF16), jax.ShapeDtypeStruct(state.shape, F32)],
            input_output_aliases=aliases,
            compiler_params=_cparams("parallel"),
            name="hgrn_sample",
        )(pk, pk, lf, pk, hg, normw, state, *extra)

    return _stacked_out(call, layer, {1: prev_state}, n_in)


def _layer_norm(y, g, b):
    mu = jnp.mean(y, axis=-1, keepdims=True)
    yc = y - mu
    var = jnp.mean(yc * yc, axis=-1, keepdims=True)
    return yc * lax.rsqrt(var + LN_EPS) * g + b


def _post_kernel(alpha, n_cast, a_ref, h_ref, g_ref, x_ref, wua_ref, wuh_ref, wo_ref, l1g_ref, l1b_ref,
                 wf1_ref, wf2_ref, l2g_ref, l2b_ref, *refs):
    o_ref = refs[n_cast]
    _cast_rows(refs[:n_cast], refs[n_cast + 1:])
    groups = [slice(r0, r0 + POST_GROUP) for r0 in range(0, x_ref.shape[0], POST_GROUP)]
    dot = lambda lhs, w_ref: jnp.dot(lhs, w_ref[...], preferred_element_type=F32)
    ua = [dot(a_ref[rs], wua_ref) for rs in groups]
    uh = [dot(h_ref[rs], wuh_ref) for rs in groups]
    gate = jax.nn.sigmoid
    merged = [(gate(g_ref[rs, :D_MODEL]) * a + gate(g_ref[rs, D_MODEL:]) * h).astype(BF16)
              for rs, a, h in zip(groups, ua, uh)]
    m = [dot(t, wo_ref) for t in merged]
    x1 = [_layer_norm(alpha * x_ref[rs] + t, l1g_ref[...], l1b_ref[...]) for rs, t in zip(groups, m)]
    hid = [jnp.square(jnp.maximum(dot(t.astype(BF16), wf1_ref), 0.0)).astype(BF16) for t in x1]
    ff = [dot(t, wf2_ref) for t in hid]
    for rs, t, f in zip(groups, x1, ff):
        o_ref[rs] = _layer_norm(alpha * t + f, l2g_ref[...], l2b_ref[...])


def _post(a, h, g, x, wb, p, layer, alpha, cast=(), cast_layer=0):
    t = x.shape[0]
    tm = min(POST_ROWS, t)
    assert t % tm == 0
    rows = lambda w: pl.BlockSpec((tm, w), lambda i: (i, 0))
    vec = lambda w: _resident((None, 1, w), lambda i: (layer, 0, 0))
    mat = lambda w: _resident(w.shape, lambda i: (0, 0))
    cast_src, cast_dst, cast_shapes = _cast_specs(cast, cast_layer, t // tm)
    outs = pl.pallas_call(
        functools.partial(_post_kernel, alpha, len(cast)),
        grid=(t // tm,),
        in_specs=[rows(ATT_W), rows(HG_W), rows(GATE_W), rows(D_MODEL), mat(wb[0]), mat(wb[1]), mat(wb[2]),
                  vec(D_MODEL), vec(D_MODEL), mat(wb[3]), mat(wb[4]), vec(D_MODEL), vec(D_MODEL)] + cast_src,
        out_specs=[rows(D_MODEL)] + cast_dst,
        out_shape=[jax.ShapeDtypeStruct((t, D_MODEL), F32)] + cast_shapes,
        compiler_params=_cparams("parallel"),
        name="post",
    )(a, h, g, x, wb[0], wb[1], wb[2], p["ln1_g"], p["ln1_b"], wb[3], wb[4], p["ln2_g"], p["ln2_b"], *cast)
    return outs[0], outs[1:]


def kernel(x_prompt, x_sample, cache_k, cache_v, state_hgrn, w_in, b_gate, attn_sink, hgrn_lb_logits, hgrn_norm_w,
           w_up_attn, w_up_hgrn, w_out, ln1_g, ln1_b, w_ff1, w_ff2, ln2_g, ln2_b):
    depth = w_in.shape[0]
    batch, seq, _ = x_prompt.shape
    db, dec_seq, _ = x_sample.shape
    assert dec_seq == DEC_SEQ and seq % CHUNK == 0 and db % (SAMPLE_GROUPS * SEQ_PER_STEP) == 0
    assert cache_k.shape[2:] == (WINDOW, KV_HEADS, HEAD_DIM)
    alpha = (2 * depth) ** 0.25

    vec = lambda t: t.reshape(depth, 1, t.shape[-1])
    b_gate3 = vec(b_gate)
    p = dict(ln1_g=vec(ln1_g), ln1_b=vec(ln1_b), ln2_g=vec(ln2_g), ln2_b=vec(ln2_b))
    mixer_w = (w_up_attn, w_up_hgrn, w_out, w_ff1, w_ff2)
    w_in_b = w_in[0].astype(BF16)
    to_t = lambda c: jnp.transpose(c, (0, 1, 3, 4, 2)).reshape(depth, db, KV_W, WINDOW)
    from_t = lambda c: jnp.transpose(c.reshape(depth, db, KV_HEADS, HEAD_DIM, WINDOW), (0, 1, 4, 2, 3))
    ck_t, cv_t = to_t(cache_k), to_t(cache_v)

    xp = x_prompt.reshape(batch * seq, D_MODEL)
    xs = x_sample.reshape(db * DEC_SEQ, D_MODEL)
    pk, pv, ps = [], [], []
    sk = sv = ss = None
    for l in range(depth):
        sink = attn_sink[l].reshape(1, N_HEADS)
        normw = hgrn_norm_w[l].reshape(1, HG_D)
        packed, kv, lf, hg, g, chunk_ok, *wb = _in_proj(xp, w_in_b, b_gate3, hgrn_lb_logits, l, cast=mixer_w)
        a = _attn_prompt(packed, sink, batch, seq)
        h, st = _hgrn_prompt(chunk_ok[:, 0, 0], packed, lf, hg, normw, batch, seq)
        xp, next_w_in = _post(a, h, g, xp, wb, p, l, alpha, cast=(w_in,) if l + 1 < depth else (), cast_layer=l + 1)
        win = kv.reshape(batch, seq, 2 * KV_W)[:, seq - WINDOW:].reshape(batch, WINDOW, 2, KV_HEADS, HEAD_DIM)
        pk.append(win[:, :, 0])
        pv.append(win[:, :, 1])
        ps.append(st)
        packed, kvn, lf, hg, g, _ = _in_proj(xs, w_in_b, b_gate3, hgrn_lb_logits, l)
        a, sk, sv = _attn_sample(packed, kvn, sink, ck_t, cv_t, l, sk, sv)
        h, ss = _hgrn_sample(packed, lf, hg, normw, state_hgrn, l, ss)
        xs, _ = _post(a, h, g, xs, wb, p, l, alpha)
        if next_w_in:
            w_in_b = next_w_in[0]
    return (xp.reshape(batch, seq, D_MODEL), xs.reshape(db, DEC_SEQ, D_MODEL), jnp.stack(pk), jnp.stack(pv),
            jnp.stack(ps), from_t(sk), from_t(sv), ss)
```

```python
import functools

import jax
import jax.numpy as jnp
from jax import lax
from jax.experimental import pallas as pl
from jax.experimental.pallas import tpu as pltpu

F32 = jnp.float32
BF16 = jnp.bfloat16

D_MODEL = 1024
WINDOW = 128
N_HEADS = 8
KV_HEADS = 2
HEAD_DIM = 64
ATT_W = N_HEADS * HEAD_DIM
KV_W = KV_HEADS * HEAD_DIM
HG_HEADS = 4
HG_D = 128
HG_W = HG_HEADS * HG_D
GATE_W = 2 * D_MODEL
D_FF = 4 * D_MODEL
DEC_SEQ = 4
LN_EPS = 1e-5
RMS_EPS = 1e-6
NEG_BIG = -1e30
LB_FLOOR = 1e-30
LOG2E = 1.4426950408889634
QK_SCALE = HEAD_DIM ** -0.5 * LOG2E

AQ0, AK0, HQ0, HF0, HI0, HG0, GT0, IN_COLS = 0, 512, 768, 1280, 1792, 2304, 2816, 4864
PK_Q, PK_KS, PK_VS, PK_HQ, PK_HK, PK_HV, PK_PARTS = 0, 1, 2, 3, 4, 5, 6
PK_COLS = 512

ATT_BLOCKS = 8
CHUNK = 128
HGRN_CHUNKS = 8
HALF = CHUNK // 2
FAST_DECAY_LIMIT = 120.0
FAST_Q_LIMIT = 1e9
SEQ_PER_STEP = 4
ROWS_PER_STEP = SEQ_PER_STEP * DEC_SEQ
SAMPLE_GROUPS = 8
IN_PROJ_ROWS = 512
POST_ROWS = 512
POST_GROUP = 256
V7X_VMEM_LIMIT = 56 * 1024 * 1024


def _cparams(*sem):
    return pltpu.CompilerParams(dimension_semantics=sem, vmem_limit_bytes=V7X_VMEM_LIMIT)


def _resident(shape, index_map):
    return pl.BlockSpec(shape, index_map, pipeline_mode=pl.Buffered(1))


def _stacked_out(call, layer, stacked, n_in):
    extra, aliases = [], {}
    if layer > 0:
        for out_idx, prev in stacked.items():
            aliases[n_in + len(extra)] = out_idx
            extra.append(prev)
    return call(extra, aliases)


def _interleave(stage_generators):
    pending = list(stage_generators)
    while pending:
        for gen in list(pending):
            if next(gen, StopIteration) is StopIteration:
                pending.remove(gen)


def _lower_bound(logits_ref, layer):
    lg = logits_ref[...]
    e = jnp.exp(lg - jnp.max(lg, axis=0, keepdims=True))
    p = e / jnp.sum(e, axis=0, keepdims=True)
    lb = jnp.zeros((1, HG_W), F32)
    for i in range(1, layer + 1):
        lb = lb + p[i:i + 1]
    return lb


def _swish(z):
    return z * jax.nn.sigmoid(z)


def _forget_and_key(zf, lb):
    e = jnp.exp(-jnp.abs(zf))
    r = 1.0 / (1.0 + e)
    pos = zf >= 0
    sig_p = jnp.where(pos, r, e * r)
    sig_n = jnp.where(pos, e * r, r)
    logf = jnp.log(jnp.maximum(lb, LB_FLOOR) + (1.0 - lb) * sig_p)
    return logf, (1.0 - lb) * sig_n


def _cast_rows(src_refs, dst_refs):
    for src, dst in zip(src_refs, dst_refs):
        dst[...] = src[...].astype(BF16)


def _in_proj_kernel(layer, n_cast, x_ref, w_ref, bg_ref, logits_ref, *refs):
    cast_src, refs = refs[:n_cast], refs[n_cast:]
    (pk_ref, kv_ref, lf_ref, hg_ref, g_ref, ok_ref), refs = refs[:6], refs[6:]
    cast_dst, xb_ref = refs[:n_cast], refs[n_cast]
    _cast_rows(cast_src, cast_dst)
    part = lambda j: pk_ref.at[:, j * PK_COLS:(j + 1) * PK_COLS]
    outs = (part(PK_Q), kv_ref, part(PK_KS), part(PK_VS), part(PK_HQ), part(PK_HK), lf_ref, part(PK_HV), hg_ref,
            g_ref, ok_ref)
    for _ in _in_proj_stages(layer, x_ref, w_ref, bg_ref, logits_ref, outs, xb_ref):
        pass


def _in_proj_stages(layer, x_ref, w_ref, bg_ref, logits_ref, outs, xb_ref):
    q_ref, kv_ref, ks_ref, vs_ref, hq_ref, hk_ref, lf_ref, hv_ref, hg_ref, g_ref, ok_ref = outs
    xb = x_ref[...].astype(BF16)
    lb = _lower_bound(logits_ref, layer)
    chunks = x_ref.shape[0] // CHUNK
    q_ok = []
    decay_ok = []

    def put_q(z):
        q_ref[...] = (z * QK_SCALE).astype(BF16)

    def put_kv(z):
        kv_ref[...] = z
        ks_ref[...] = _split_heads(z[:, :KV_W])
        vs_ref[...] = _split_heads(z[:, KV_W:])

    def put_hq(z):
        hq = _swish(z)
        hq_ref[...] = hq.astype(BF16)
        for c in range(chunks):
            q_ok.append(jnp.max(jnp.abs(hq[c * CHUNK:(c + 1) * CHUNK]), keepdims=True) <= FAST_Q_LIMIT)

    def put_hf(z):
        logf, key = _forget_and_key(z, lb)
        lf_ref[...] = logf
        hk_ref[...] = key.astype(BF16)
        for lo in range(0, chunks * CHUNK, HALF):
            total = jnp.sum(logf[lo:lo + HALF], axis=0, keepdims=True)
            decay_ok.append(jnp.min(total, keepdims=True) >= -FAST_DECAY_LIMIT)

    def put_hv(z):
        hv_ref[...] = z.astype(BF16)

    def put_hg(z):
        hg_ref[...] = _swish(z)

    def put_gates(lo):
        def put(z):
            g_ref[:, lo:lo + 512] = z + bg_ref[:, lo:lo + 512]
        return put

    gates = [(GT0 + lo, GT0 + lo + 512, put_gates(lo)) for lo in range(0, GATE_W, 512)]
    stages = [(HF0, HI0, put_hf), gates[0], (HQ0, HF0, put_hq), gates[1], (AK0, HQ0, put_kv), gates[2],
              (HG0, GT0, put_hg), (AQ0, AK0, put_q), (HI0, HG0, put_hv), gates[3]]
    pending = None
    xb_ref[...] = xb
    for lo, hi, put in stages:
        z = jnp.dot(xb_ref[...], w_ref[:, lo:hi], preferred_element_type=F32)
        if pending is not None:
            pending[0](pending[1])
        pending = (put, z)
        yield
    pending[0](pending[1])
    for c in range(chunks):
        ok = jnp.logical_and(q_ok[c], jnp.logical_and(decay_ok[2 * c], decay_ok[2 * c + 1]))
        ok_ref[c] = jnp.broadcast_to(jnp.where(ok, 1, 0), (1, 128)).astype(jnp.int32)


def _split_heads(t):
    low = lax.broadcasted_iota(jnp.int32, t.shape, 1) < HEAD_DIM
    h0_lo = jnp.where(low, t, 0.0)
    h1_hi = jnp.where(low, 0.0, t)
    blocks = [h0_lo, pltpu.roll(h0_lo, HEAD_DIM, 1), pltpu.roll(h1_hi, HEAD_DIM, 1), h1_hi]
    return jnp.concatenate(blocks, axis=1).astype(BF16)


def _cast_specs(weights, layer, steps):
    src, dst, shapes = [], [], []
    for w in weights:
        _, r, c = w.shape
        slab = r // steps
        assert r % steps == 0 and (slab % 16 == 0 or steps == 1)
        src.append(pl.BlockSpec((None, slab, c), lambda i, layer=layer: (layer, i, 0)))
        dst.append(pl.BlockSpec((slab, c), lambda i: (i, 0)))
        shapes.append(jax.ShapeDtypeStruct((r, c), BF16))
    return src, dst, shapes


def _in_proj(x, w_b, b_gate, logits, layer, cast=()):
    t = x.shape[0]
    tm = min(IN_PROJ_ROWS, t)
    assert t % tm == 0 and tm % CHUNK == 0
    rows = lambda w: pl.BlockSpec((tm, w), lambda i: (i, 0))
    out = lambda w, dt: jax.ShapeDtypeStruct((t, w), dt)
    cast_src, cast_dst, cast_shapes = _cast_specs(cast, layer, t // tm)
    return pl.pallas_call(
        functools.partial(_in_proj_kernel, layer, len(cast)),
        grid=(t // tm,),
        in_specs=[rows(D_MODEL),
                  _resident((D_MODEL, IN_COLS), lambda i: (0, 0)),
                  _resident((None, 1, GATE_W), lambda i: (layer, 0, 0)),
                  _resident(logits.shape, lambda i: (0, 0))] + cast_src,
        out_specs=[rows(PK_PARTS * PK_COLS), rows(2 * KV_W), rows(HG_W), rows(HG_W), rows(GATE_W),
                   pl.BlockSpec((tm // CHUNK, 1, 128), lambda i: (i, 0, 0))] + cast_dst,
        out_shape=[out(PK_PARTS * PK_COLS, BF16), out(2 * KV_W, F32), out(HG_W, F32), out(HG_W, F32),
                   out(GATE_W, F32), jax.ShapeDtypeStruct((t // CHUNK, 1, 128), jnp.int32)] + cast_shapes,
        scratch_shapes=[pltpu.VMEM((tm, D_MODEL), BF16)],
        compiler_params=_cparams("parallel"),
        name="in_proj",
    )(x, w_b, b_gate, logits, *cast)


def _attn_block(n, sink_ref, bias_ref, first, q_ref, kp_ref, kc_ref, vp_ref, vc_ref, o_ref):
    r = WINDOW
    top = lax.broadcasted_iota(jnp.int32, (2 * r, 1), 0) < r
    combos = [(g, half) for g in range(KV_HEADS) for half in range(2)]
    lanes = lambda g, half: slice((2 * g + half) * KV_W, (2 * g + half + 1) * KV_W)
    sink = [jnp.where(top, sink_ref[0, 4 * g + half], sink_ref[0, 4 * g + 2 + half]) * LOG2E for g, half in combos]
    own = slice(n * r, (n + 1) * r)

    def kv_rows(prev_ref, cur_ref, g, half):
        before = prev_ref[:, lanes(g, half)] if n == 0 else cur_ref[(n - 1) * r:n * r, lanes(g, half)]
        return jnp.concatenate([before, cur_ref[own, lanes(g, half)]], axis=0)

    qg = [jnp.concatenate([q_ref[own, 2 * g * 128:(2 * g + 1) * 128],
                           q_ref[own, (2 * g + 1) * 128:(2 * g + 2) * 128]], axis=0) for g in range(KV_HEADS)]
    b = bias_ref[first] if n == 0 else bias_ref[1]
    bias = jnp.concatenate([b, b], axis=0)
    s = [lax.dot_general(qg[g], kv_rows(kp_ref, kc_ref, g, half), (((1,), (1,)), ((), ())),
                         preferred_element_type=F32) + bias for g, half in combos]
    yield
    m = [jnp.maximum(jnp.max(t, axis=-1, keepdims=True), sk) for t, sk in zip(s, sink)]
    p = [jnp.exp2(t - mx) for t, mx in zip(s, m)]
    yield
    den = [jnp.sum(t, axis=-1, keepdims=True) + jnp.exp2(sk - mx) for t, sk, mx in zip(p, sink, m)]
    o = [jnp.dot(t.astype(BF16), kv_rows(vp_ref, vc_ref, g, half), preferred_element_type=F32) / d
         for t, d, (g, half) in zip(p, den, combos)]
    yield
    cols = []
    for g in range(KV_HEADS):
        acc = o[2 * g] + o[2 * g + 1]
        cols += [acc[:r], acc[r:]]
    o_ref[own, :] = jnp.concatenate(cols, axis=1).astype(o_ref.dtype)


def _band_bias():
    qi = lax.broadcasted_iota(jnp.int32, (WINDOW, 2 * WINDOW), 0)
    kj = lax.broadcasted_iota(jnp.int32, (WINDOW, 2 * WINDOW), 1)
    band = jnp.logical_and(kj >= qi, kj <= qi + WINDOW)
    first = jnp.logical_and(band, kj >= WINDOW)
    return jnp.where(jnp.stack([first, band]), 0.0, NEG_BIG).astype(F32)


def _attn_prompt_kernel(sink_ref, bias_ref, q_ref, kp_ref, kc_ref, vp_ref, vc_ref, o_ref):
    first = jnp.minimum(pl.program_id(1), 1)
    _interleave(_attn_block(n, sink_ref, bias_ref, first, q_ref, kp_ref, kc_ref, vp_ref, vc_ref, o_ref)
                for n in range(ATT_BLOCKS))


def _attn_prompt(pk, sink, batch, seq):
    steps = seq // (ATT_BLOCKS * WINDOW)
    assert seq % (ATT_BLOCKS * WINDOW) == 0
    cur = lambda part: pl.BlockSpec((ATT_BLOCKS * WINDOW, PK_COLS), lambda b, i: (b * steps + i, part))
    prev = lambda part: pl.BlockSpec(
        (WINDOW, PK_COLS), lambda b, i: ((b * steps + i) * ATT_BLOCKS - jnp.minimum(i, 1), part))
    return pl.pallas_call(
        _attn_prompt_kernel,
        grid=(batch, steps),
        in_specs=[pl.BlockSpec(memory_space=pltpu.SMEM),
                  _resident((2, WINDOW, 2 * WINDOW), lambda b, i: (0, 0, 0)),
                  cur(PK_Q), prev(PK_KS), cur(PK_KS), prev(PK_VS), cur(PK_VS)],
        out_specs=pl.BlockSpec((ATT_BLOCKS * WINDOW, ATT_W), lambda b, i: (b * steps + i, 0)),
        out_shape=jax.ShapeDtypeStruct((batch * seq, ATT_W), BF16),
        compiler_params=_cparams("parallel", "parallel"),
        name="attn_prompt",
    )(sink, _band_bias(), pk, pk, pk, pk, pk)


def _attn_sample_kernel(sink_ref, q_ref, kvn_ref, ck_ref, cv_ref, *rest):
    o_ref, nk_ref, nv_ref = rest[-3:]
    _interleave(_attn_sample_group(grp, sink_ref, q_ref, kvn_ref, ck_ref, cv_ref, o_ref, nk_ref, nv_ref)
                for grp in range(SAMPLE_GROUPS))


def _attn_sample_group(grp, sink_ref, q_ref, kvn_ref, ck_ref, cv_ref, o_ref, nk_ref, nv_ref):
    rows = slice(grp * ROWS_PER_STEP, (grp + 1) * ROWS_PER_STEP)
    ck_ref, cv_ref, nk_ref, nv_ref = (r.at[grp * SEQ_PER_STEP:(grp + 1) * SEQ_PER_STEP]
                                      for r in (ck_ref, cv_ref, nk_ref, nv_ref))
    lane = lax.broadcasted_iota(jnp.int32, (ROWS_PER_STEP, KV_W), 1)
    q16 = q_ref[rows, :].astype(F32)
    blocks = []
    for h in range(N_HEADS):
        c, half, g = h // 2, h % 2, h // (N_HEADS // KV_HEADS)
        own = lane >= HEAD_DIM if half else lane < HEAD_DIM
        qh = jnp.where(own, q16[:, c * 128:(c + 1) * 128], 0.0)
        blocks.append(qh if half == g else pltpu.roll(qh, HEAD_DIM, 1))
    wt = jnp.concatenate(blocks, axis=0).astype(BF16)

    row = lax.broadcasted_iota(jnp.int32, (WINDOW, WINDOW), 0)
    col = lax.broadcasted_iota(jnp.int32, (WINDOW, WINDOW), 1)
    tok = jnp.bitwise_and(row, DEC_SEQ - 1)
    row_seq = jnp.bitwise_and(jnp.right_shift(row, 2), SEQ_PER_STEP - 1)
    pad = jnp.zeros((WINDOW - ROWS_PER_STEP, KV_W), F32)
    kn = jnp.concatenate([kvn_ref[rows, :KV_W], pad], axis=0)
    vn = jnp.concatenate([kvn_ref[rows, KV_W:], pad], axis=0)
    kn_t = jnp.transpose(kn)
    vn_t = jnp.transpose(vn)
    yield

    s_c = jnp.zeros((WINDOW, WINDOW), F32)
    for j in range(SEQ_PER_STEP):
        sj = jnp.dot(wt, ck_ref[j].astype(BF16), preferred_element_type=F32)
        s_c = jnp.where(row_seq == j, sj, s_c)
    s_n = jnp.dot(wt, kn_t.astype(BF16), preferred_element_type=F32)
    yield
    s_c = jnp.where(col >= tok, s_c, NEG_BIG)
    new_ok = jnp.logical_and(jnp.right_shift(col, 2) == row_seq, jnp.bitwise_and(col, DEC_SEQ - 1) <= tok)
    s_n = jnp.where(new_ok, s_n, NEG_BIG)
    head = jnp.right_shift(lax.broadcasted_iota(jnp.int32, (WINDOW, 1), 0), 4)
    sink = jnp.zeros((WINDOW, 1), F32)
    for h in range(N_HEADS):
        sink = jnp.where(head == h, sink_ref[0, h] * LOG2E, sink)
    m = jnp.maximum(jnp.maximum(jnp.max(s_c, axis=-1, keepdims=True), jnp.max(s_n, axis=-1, keepdims=True)), sink)
    p_c = jnp.exp2(s_c - m)
    p_n = jnp.exp2(s_n - m)
    den = jnp.sum(p_c, axis=-1, keepdims=True) + jnp.sum(p_n, axis=-1, keepdims=True) + jnp.exp2(sink - m)
    p_cb = p_c.astype(BF16)
    yield
    o = jnp.dot(p_n.astype(BF16), vn.astype(BF16), preferred_element_type=F32)
    for j in range(SEQ_PER_STEP):
        oj = lax.dot_general(p_cb, cv_ref[j].astype(BF16), (((1,), (1,)), ((), ())), preferred_element_type=F32)
        o = o + jnp.where(row_seq == j, oj, 0.0)
    yield
    o = o / den
    outs = []
    for c in range(4):
        lo, hi = o[32 * c:32 * c + 16], o[32 * c + 16:32 * c + 32]
        if c < 2:
            outs.append(jnp.where(lane < HEAD_DIM, lo, pltpu.roll(hi, HEAD_DIM, 1)))
        else:
            outs.append(jnp.where(lane < HEAD_DIM, pltpu.roll(lo, HEAD_DIM, 1), hi))
    o_ref[rows, :] = jnp.concatenate(outs, axis=1).astype(o_ref.dtype)
    yield
    for j in range(SEQ_PER_STEP):
        shift = (WINDOW - DEC_SEQ - DEC_SEQ * j) % WINDOW
        for new_t, cache_ref, dst in ((kn_t, ck_ref, nk_ref), (vn_t, cv_ref, nv_ref)):
            dst[j] = jnp.where(col >= WINDOW - DEC_SEQ, pltpu.roll(new_t, shift, 1),
                               pltpu.roll(cache_ref[j], WINDOW - DEC_SEQ, 1))


def _attn_sample(pk, kvn, sink, cache_kt, cache_vt, layer, prev_k, prev_v):
    depth, db = cache_kt.shape[:2]
    seqs = SAMPLE_GROUPS * SEQ_PER_STEP
    cache_spec = pl.BlockSpec((None, seqs, KV_W, WINDOW), lambda s: (layer, s, 0, 0))
    rows = lambda w: pl.BlockSpec((seqs * DEC_SEQ, w), lambda s: (s, 0))
    n_in = 5

    def call(extra, aliases):
        return pl.pallas_call(
            _attn_sample_kernel,
            grid=(db // seqs,),
            in_specs=[pl.BlockSpec(memory_space=pltpu.SMEM), rows(ATT_W), rows(2 * KV_W), cache_spec, cache_spec]
            + [pl.BlockSpec(memory_space=pl.ANY)] * len(extra),
            out_specs=[rows(ATT_W), cache_spec, cache_spec],
            out_shape=[jax.ShapeDtypeStruct((db * DEC_SEQ, ATT_W), BF16),
                       jax.ShapeDtypeStruct((depth, db, KV_W, WINDOW), F32),
                       jax.ShapeDtypeStruct((depth, db, KV_W, WINDOW), F32)],
            input_output_aliases=aliases,
            compiler_params=_cparams("parallel"),
            name="attn_sample",
        )(sink, pk, kvn, cache_kt, cache_vt, *extra)

    return _stacked_out(call, layer, {1: prev_k, 2: prev_v}, n_in)


def _hgrn_out(o, g_act, normw):
    ms = jnp.mean(o * o, axis=-1, keepdims=True)
    return o * lax.rsqrt(ms + RMS_EPS) * normw * g_act


def _cumsum_rows(x, tri):
    hi = x.astype(BF16)
    lo = (x - hi.astype(F32)).astype(BF16)
    dot = lambda t: jnp.dot(tri, t, preferred_element_type=F32)
    return dot(hi) + dot(lo)


def _heads(t):
    return [t[:, h * HG_D:(h + 1) * HG_D] for h in range(HG_HEADS)]


def _level_reference(b, n, row):
    if n == 2:
        return jnp.where(jnp.bitwise_and(row, 1) == 0, b, pltpu.roll(b, 1, 0))
    if n == 4:
        m4 = jnp.bitwise_and(row, 3)
        return jnp.where(m4 == 0, pltpu.roll(b, CHUNK - 1, 0),
                         jnp.where(m4 == 1, b, jnp.where(m4 == 2, pltpu.roll(b, 1, 0), pltpu.roll(b, 2, 0))))
    bounds = [m * n + n // 2 - 1 for m in range(CHUNK // n)]
    pieces = [jnp.broadcast_to(b[t:t + 1], (n, b.shape[1])) for t in bounds]
    return pieces[0] if len(pieces) == 1 else jnp.concatenate(pieces, axis=0)


def _scores(q, k, e_q, e_k):
    qs, ks = _heads(q * e_q.astype(BF16)), _heads(k * e_k.astype(BF16))
    return [lax.dot_general(a, c, (((1,), (1,)), ((), ())), preferred_element_type=F32) for a, c in zip(qs, ks)]


def _intra_chunk_scores(q, k, b, lvl, fast):
    if fast:
        b_mid, b_end = b[HALF - 1:HALF], b[CHUNK - 1:CHUNK]
        r = jnp.concatenate([jnp.broadcast_to(0.5 * b_mid, (HALF, HG_W)),
                             jnp.broadcast_to(0.5 * (b_mid + b_end), (HALF, HG_W))], axis=0)
        e = jnp.exp(b - r)
        e6 = jnp.exp(-jnp.abs(b - b_mid))
        return [jnp.where(lvl == 6, across, jnp.where(lvl >= -1, within, 0.0))
                for within, across in zip(_scores(q, k, e, 1.0 / e), _scores(q, k, e6, e6))]
    row = lax.broadcasted_iota(jnp.int32, (CHUNK, HG_W), 0)
    atts = [jnp.where(lvl == -1, jnp.sum(qk, axis=-1, keepdims=True), 0.0) for qk in _heads((q * k).astype(F32))]
    for level in range(7):
        e = jnp.exp(-jnp.abs(b - _level_reference(b, 2 << level, row)))
        atts = [jnp.where(lvl == level, p, att) for p, att in zip(_scores(q, k, e, e), atts)]
    return atts


def _hgrn_chunks(fast, hq_ref, hk_ref, lf_ref, hv_ref, hg_ref, normw_ref, o_ref, st_ref):
    row = lax.broadcasted_iota(jnp.int32, (CHUNK, CHUNK), 0)
    col = lax.broadcasted_iota(jnp.int32, (CHUNK, CHUNK), 1)
    tri = jnp.where(row >= col, 1.0, 0.0).astype(BF16)
    lvl = jnp.where(row >= col, 31 - lax.clz(jnp.bitwise_xor(row, col)), -2)
    subs = [slice(c * CHUNK, (c + 1) * CHUNK) for c in range(hq_ref.shape[0] // CHUNK)]
    q, k, v = ([ref[rs] for rs in subs] for ref in (hq_ref, hk_ref, hv_ref))
    b = [_cumsum_rows(lf_ref[rs], tri) for rs in subs]
    b_end = [t[CHUNK - 1:CHUNK] for t in b]
    atts = [_intra_chunk_scores(qc, kc, bc, lvl, fast) for qc, kc, bc in zip(q, k, b)]
    q_in = [_heads(qc * jnp.exp(bc).astype(BF16)) for qc, bc in zip(q, b)]
    k_out = [_heads(kc.astype(F32) * jnp.exp(be - bc)) for kc, bc, be in zip(k, b, b_end)]
    k_t = [[jnp.transpose(t).astype(BF16) for t in per_head] for per_head in k_out]
    decay = [[jnp.transpose(jnp.broadcast_to(t, (HG_D, HG_D))) for t in _heads(jnp.exp(be))] for be in b_end]
    vs = [_heads(vc) for vc in v]
    local = [[jnp.dot(a.astype(BF16), vh, preferred_element_type=F32) for a, vh in zip(ac, vc)]
             for ac, vc in zip(atts, vs)]
    grow = [[jnp.dot(kt, vh, preferred_element_type=F32) for kt, vh in zip(kc, vc)] for kc, vc in zip(k_t, vs)]
    state = [st_ref[0, h] for h in range(HG_HEADS)]
    for c, rs in enumerate(subs):
        gs = _heads(hg_ref[rs])
        outs = [_hgrn_out(jnp.dot(q_in[c][h], state[h].astype(BF16), preferred_element_type=F32) + local[c][h],
                          gs[h], normw_ref[...]) for h in range(HG_HEADS)]
        o_ref[rs] = jnp.concatenate(outs, axis=1).astype(o_ref.dtype)
        state = [decay[c][h] * state[h] + grow[c][h] for h in range(HG_HEADS)]
    for h in range(HG_HEADS):
        st_ref[0, h] = state[h]


def _hgrn_prompt_kernel(ok_ref, *refs):
    st_ref = refs[-1]

    @pl.when(pl.program_id(1) == 0)
    def _():
        st_ref[...] = jnp.zeros_like(st_ref)

    first = (pl.program_id(0) * pl.num_programs(1) + pl.program_id(1)) * HGRN_CHUNKS
    ok = ok_ref[first] != 0
    for c in range(1, HGRN_CHUNKS):
        ok = jnp.logical_and(ok, ok_ref[first + c] != 0)

    @pl.when(ok)
    def _():
        _hgrn_chunks(True, *refs)

    @pl.when(jnp.logical_not(ok))
    def _():
        _hgrn_chunks(False, *refs)


def _hgrn_prompt(chunk_ok, pk, lf, hg, normw, batch, seq):
    nc = seq // (HGRN_CHUNKS * CHUNK)
    assert seq % (HGRN_CHUNKS * CHUNK) == 0
    part = lambda j: pl.BlockSpec((HGRN_CHUNKS * CHUNK, HG_W), lambda b, i, ok: (b * nc + i, j))
    rows = part(0)
    return pl.pallas_call(
        _hgrn_prompt_kernel,
        grid_spec=pltpu.PrefetchScalarGridSpec(
            num_scalar_prefetch=1,
            grid=(batch, nc),
            in_specs=[part(PK_HQ), part(PK_HK), rows, part(PK_HV), rows,
                      pl.BlockSpec((1, HG_D), lambda b, i, ok: (0, 0))],
            out_specs=[rows, pl.BlockSpec((1, HG_HEADS, HG_D, HG_D), lambda b, i, ok: (b, 0, 0, 0))]),
        out_shape=[jax.ShapeDtypeStruct((batch * seq, HG_W), BF16),
                   jax.ShapeDtypeStruct((batch, HG_HEADS, HG_D, HG_D), F32)],
        compiler_params=_cparams("parallel", "arbitrary"),
        name="hgrn_prompt",
    )(chunk_ok, pk, pk, lf, pk, hg, normw)


def _hgrn_sample_kernel(hq_ref, hk_ref, lf_ref, hv_ref, hg_ref, normw_ref, st_ref, *rest):
    o_ref, ns_ref = rest[-2:]
    _interleave(_hgrn_sample_group(grp, hq_ref, hk_ref, lf_ref, hv_ref, hg_ref, normw_ref, st_ref, o_ref, ns_ref)
                for grp in range(SAMPLE_GROUPS))


def _hgrn_sample_group(grp, hq_ref, hk_ref, lf_ref, hv_ref, hg_ref, normw_ref, st_ref, o_ref, ns_ref):
    rows = slice(grp * ROWS_PER_STEP, (grp + 1) * ROWS_PER_STEP)
    st_ref, ns_ref = (r.at[grp * SEQ_PER_STEP:(grp + 1) * SEQ_PER_STEP] for r in (st_ref, ns_ref))
    q, k, v = (r[rows, :].astype(F32) for r in (hq_ref, hk_ref, hv_ref))
    logf = lf_ref[rows, :]
    row = lax.broadcasted_iota(jnp.int32, (ROWS_PER_STEP, HG_W), 0)
    tok = jnp.bitwise_and(row, DEC_SEQ - 1)
    seq_of_row = jnp.right_shift(row[:, :HG_D], 2)
    down = lambda t, d: pltpu.roll(t, d, 0)
    up = lambda t, d: pltpu.roll(t, ROWS_PER_STEP - d, 0)
    b = logf
    for d in range(1, DEC_SEQ):
        b = b + jnp.where(tok >= d, down(logf, d), 0.0)
    b_end = jnp.where(tok == 3, b, jnp.where(tok == 2, up(b, 1), jnp.where(tok == 1, up(b, 2), up(b, 3))))
    qe = (q * jnp.exp(b)).astype(BF16)
    kd = k * jnp.exp(b_end - b)
    e_end = jnp.exp(b_end)
    intra = [jnp.zeros((ROWS_PER_STEP, HG_D), F32) for _ in range(HG_HEADS)]
    for d in range(DEC_SEQ):
        kd_, bd_, vd_ = (k, b, v) if d == 0 else (down(k, d), down(b, d), down(v, d))
        w = jnp.where(tok >= d, q * kd_ * jnp.exp(jnp.minimum(b - bd_, 0.0)), 0.0)
        for h in range(HG_HEADS):
            sl = slice(h * HG_D, (h + 1) * HG_D)
            intra[h] = intra[h] + jnp.sum(w[:, sl], axis=-1, keepdims=True) * vd_[:, sl]
    pad = jnp.zeros((HG_D - 2 * ROWS_PER_STEP, HG_D), F32)
    colh = lax.broadcasted_iota(jnp.int32, (HG_D, HG_D), 1)
    normw = normw_ref[...]
    yield
    for h in range(HG_HEADS):
        sl = slice(h * HG_D, (h + 1) * HG_D)
        vb = jnp.concatenate([v[:, sl], jnp.zeros((HG_D - ROWS_PER_STEP, HG_D), F32)], axis=0).astype(BF16)
        tt = jnp.transpose(jnp.concatenate([kd[:, sl], e_end[:, sl], pad], axis=0))
        inter = jnp.zeros((ROWS_PER_STEP, HG_D), F32)
        for j in range(SEQ_PER_STEP):
            s0 = st_ref[j, h]
            oi = jnp.dot(qe[:, sl], s0.astype(BF16), preferred_element_type=F32)
            inter = jnp.where(seq_of_row == j, oi, inter)
            kd_t = jnp.where(jnp.right_shift(colh, 2) == j, tt, 0.0).astype(BF16)
            e_col = ROWS_PER_STEP + DEC_SEQ * j
            decay = jnp.broadcast_to(tt[:, e_col:e_col + 1], (HG_D, HG_D))
            ns_ref[j, h] = decay * s0 + jnp.dot(kd_t, vb, preferred_element_type=F32)
        o_ref[rows, sl] = _hgrn_out(inter + intra[h], hg_ref[rows, sl], normw).astype(o_ref.dtype)
        yield


def _hgrn_sample(pk, lf, hg, normw, state, layer, prev_state):
    depth, db = state.shape[:2]
    seqs = SAMPLE_GROUPS * SEQ_PER_STEP
    part = lambda j: pl.BlockSpec((seqs * DEC_SEQ, HG_W), lambda s: (s, j))
    rows = part(0)
    st_spec = pl.BlockSpec((None, seqs, HG_HEADS, HG_D, HG_D), lambda s: (layer, s, 0, 0, 0))
    n_in = 7

    def call(extra, aliases):
        return pl.pallas_call(
            _hgrn_sample_kernel,
            grid=(db // seqs,),
            in_specs=[part(PK_HQ), part(PK_HK), rows, part(PK_HV), rows,
                      pl.BlockSpec((1, HG_D), lambda s: (0, 0)), st_spec]
            + [pl.BlockSpec(memory_space=pl.ANY)] * len(extra),
            out_specs=[rows, st_spec],
            out_shape=[jax.ShapeDtypeStruct((db * DEC_SEQ, HG_W), BF16), jax.ShapeDtypeStruct(state.shape, F32)],
            input_output_aliases=aliases,
            compiler_params=_cparams("parallel"),
            name="hgrn_sample",
        )(pk, pk, lf, pk, hg, normw, state, *extra)

    return _stacked_out(call, layer, {1: prev_state}, n_in)


def _layer_norm(y, g, b):
    mu = jnp.mean(y, axis=-1, keepdims=True)
    yc = y - mu
    var = jnp.mean(yc * yc, axis=-1, keepdims=True)
    return yc * lax.rsqrt(var + LN_EPS) * g + b


def _post_kernel(alpha, n_cast, a_ref, h_ref, g_ref, x_ref, wua_ref, wuh_ref, wo_ref, l1g_ref, l1b_ref,
                 wf1_ref, wf2_ref, l2g_ref, l2b_ref, *refs):
    o_ref = refs[n_cast]
    _cast_rows(refs[:n_cast], refs[n_cast + 1:])
    groups = [slice(r0, r0 + POST_GROUP) for r0 in range(0, x_ref.shape[0], POST_GROUP)]
    dot = lambda lhs, w_ref: jnp.dot(lhs, w_ref[...], preferred_element_type=F32)
    ua = [dot(a_ref[rs], wua_ref) for rs in groups]
    uh = [dot(h_ref[rs], wuh_ref) for rs in groups]
    gate = jax.nn.sigmoid
    merged = [(gate(g_ref[rs, :D_MODEL]) * a + gate(g_ref[rs, D_MODEL:]) * h).astype(BF16)
              for rs, a, h in zip(groups, ua, uh)]
    m = [dot(t, wo_ref) for t in merged]
    x1 = [_layer_norm(alpha * x_ref[rs] + t, l1g_ref[...], l1b_ref[...]) for rs, t in zip(groups, m)]
    hid = [jnp.square(jnp.maximum(dot(t.astype(BF16), wf1_ref), 0.0)).astype(BF16) for t in x1]
    ff = [dot(t, wf2_ref) for t in hid]
    for rs, t, f in zip(groups, x1, ff):
        o_ref[rs] = _layer_norm(alpha * t + f, l2g_ref[...], l2b_ref[...])


def _post(a, h, g, x, wb, p, layer, alpha, cast=(), cast_layer=0):
    t = x.shape[0]
    tm = min(POST_ROWS, t)
    assert t % tm == 0
    rows = lambda w: pl.BlockSpec((tm, w), lambda i: (i, 0))
    vec = lambda w: _resident((None, 1, w), lambda i: (layer, 0, 0))
    mat = lambda w: _resident(w.shape, lambda i: (0, 0))
    cast_src, cast_dst, cast_shapes = _cast_specs(cast, cast_layer, t // tm)
    outs = pl.pallas_call(
        functools.partial(_post_kernel, alpha, len(cast)),
        grid=(t // tm,),
        in_specs=[rows(ATT_W), rows(HG_W), rows(GATE_W), rows(D_MODEL), mat(wb[0]), mat(wb[1]), mat(wb[2]),
                  vec(D_MODEL), vec(D_MODEL), mat(wb[3]), mat(wb[4]), vec(D_MODEL), vec(D_MODEL)] + cast_src,
        out_specs=[rows(D_MODEL)] + cast_dst,
        out_shape=[jax.ShapeDtypeStruct((t, D_MODEL), F32)] + cast_shapes,
        compiler_params=_cparams("parallel"),
        name="post",
    )(a, h, g, x, wb[0], wb[1], wb[2], p["ln1_g"], p["ln1_b"], wb[3], wb[4], p["ln2_g"], p["ln2_b"], *cast)
    return outs[0], outs[1:]


def kernel(x_prompt, x_sample, cache_k, cache_v, state_hgrn, w_in, b_gate, attn_sink, hgrn_lb_logits, hgrn_norm_w,
           w_up_attn, w_up_hgrn, w_out, ln1_g, ln1_b, w_ff1, w_ff2, ln2_g, ln2_b):
    depth = w_in.shape[0]
    batch, seq, _ = x_prompt.shape
    db, dec_seq, _ = x_sample.shape
    assert dec_seq == DEC_SEQ and seq % CHUNK == 0 and db % (SAMPLE_GROUPS * SEQ_PER_STEP) == 0
    assert cache_k.shape[2:] == (WINDOW, KV_HEADS, HEAD_DIM)
    alpha = (2 * depth) ** 0.25

    vec = lambda t: t.reshape(depth, 1, t.shape[-1])
    b_gate3 = vec(b_gate)
    p = dict(ln1_g=vec(ln1_g), ln1_b=vec(ln1_b), ln2_g=vec(ln2_g), ln2_b=vec(ln2_b))
    mixer_w = (w_up_attn, w_up_hgrn, w_out, w_ff1, w_ff2)
    w_in_b = w_in[0].astype(BF16)
    to_t = lambda c: jnp.transpose(c, (0, 1, 3, 4, 2)).reshape(depth, db, KV_W, WINDOW)
    from_t = lambda c: jnp.transpose(c.reshape(depth, db, KV_HEADS, HEAD_DIM, WINDOW), (0, 1, 4, 2, 3))
    ck_t, cv_t = to_t(cache_k), to_t(cache_v)

    xp = x_prompt.reshape(batch * seq, D_MODEL)
    xs = x_sample.reshape(db * DEC_SEQ, D_MODEL)
    pk, pv, ps = [], [], []
    sk = sv = ss = None
    for l in range(depth):
        sink = attn_sink[l].reshape(1, N_HEADS)
        normw = hgrn_norm_w[l].reshape(1, HG_D)
        packed, kv, lf, hg, g, chunk_ok, *wb = _in_proj(xp, w_in_b, b_gate3, hgrn_lb_logits, l, cast=mixer_w)
        a = _attn_prompt(packed, sink, batch, seq)
        h, st = _hgrn_prompt(chunk_ok[:, 0, 0], packed, lf, hg, normw, batch, seq)
        xp, next_w_in = _post(a, h, g, xp, wb, p, l, alpha, cast=(w_in,) if l + 1 < depth else (), cast_layer=l + 1)
        win = kv.reshape(batch, seq, 2 * KV_W)[:, seq - WINDOW:].reshape(batch, WINDOW, 2, KV_HEADS, HEAD_DIM)
        pk.append(win[:, :, 0])
        pv.append(win[:, :, 1])
        ps.append(st)
        packed, kvn, lf, hg, g, _ = _in_proj(xs, w_in_b, b_gate3, hgrn_lb_logits, l)
        a, sk, sv = _attn_sample(packed, kvn, sink, ck_t, cv_t, l, sk, sv)
        h, ss = _hgrn_sample(packed, lf, hg, normw, state_hgrn, l, ss)
        xs, _ = _post(a, h, g, xs, wb, p, l, alpha)
        if next_w_in:
            w_in_b = next_w_in[0]
    return (xp.reshape(batch, seq, D_MODEL), xs.reshape(db, DEC_SEQ, D_MODEL), jnp.stack(pk), jnp.stack(pv),
            jnp.stack(ps), from_t(sk), from_t(sv), ss)
```

```python
import functools

import jax
import jax.numpy as jnp
from jax import lax
from jax.experimental import pallas as pl
from jax.experimental.pallas import tpu as pltpu

F32 = jnp.float32
BF16 = jnp.bfloat16

D_MODEL = 1024
WINDOW = 128
N_HEADS = 8
KV_HEADS = 2
HEAD_DIM = 64
ATT_W = N_HEADS * HEAD_DIM
KV_W = KV_HEADS * HEAD_DIM
HG_HEADS = 4
HG_D = 128
HG_W = HG_HEADS * HG_D
GATE_W = 2 * D_MODEL
D_FF = 4 * D_MODEL
DEC_SEQ = 4
LN_EPS = 1e-5
RMS_EPS = 1e-6
NEG_BIG = -1e30
LB_FLOOR = 1e-30
LOG2E = 1.4426950408889634
QK_SCALE = HEAD_DIM ** -0.5 * LOG2E

AQ0, AK0, HQ0, HF0, HI0, HG0, GT0, IN_COLS = 0, 512, 768, 1280, 1792, 2304, 2816, 4864
PK_Q, PK_KS, PK_VS, PK_HQ, PK_HK, PK_HV, PK_PARTS = 0, 1, 2, 3, 4, 5, 6
PK_COLS = 512

ATT_BLOCKS = 8
CHUNK = 128
HGRN_CHUNKS = 8
HALF = CHUNK // 2
FAST_DECAY_LIMIT = 120.0
FAST_Q_LIMIT = 1e9
SEQ_PER_STEP = 4
ROWS_PER_STEP = SEQ_PER_STEP * DEC_SEQ
SAMPLE_GROUPS = 8
IN_PROJ_ROWS = 512
POST_ROWS = 512
POST_GROUP = 256
V7X_VMEM_LIMIT = 56 * 1024 * 1024


def _cparams(*sem):
    return pltpu.CompilerParams(dimension_semantics=sem, vmem_limit_bytes=V7X_VMEM_LIMIT)


def _resident(shape, index_map):
    return pl.BlockSpec(shape, index_map, pipeline_mode=pl.Buffered(1))


def _stacked_out(call, layer, stacked, n_in):
    extra, aliases = [], {}
    if layer > 0:
        for out_idx, prev in stacked.items():
            aliases[n_in + len(extra)] = out_idx
            extra.append(prev)
    return call(extra, aliases)


def _interleave(stage_generators):
    pending = list(stage_generators)
    while pending:
        for gen in list(pending):
            if next(gen, StopIteration) is StopIteration:
                pending.remove(gen)


def _lower_bound(logits_ref, layer):
    lg = logits_ref[...]
    e = jnp.exp(lg - jnp.max(lg, axis=0, keepdims=True))
    p = e / jnp.sum(e, axis=0, keepdims=True)
    lb = jnp.zeros((1, HG_W), F32)
    for i in range(1, layer + 1):
        lb = lb + p[i:i + 1]
    return lb


def _swish(z):
    return z * jax.nn.sigmoid(z)


def _forget_and_key(zf, lb):
    e = jnp.exp(-jnp.abs(zf))
    r = 1.0 / (1.0 + e)
    pos = zf >= 0
    sig_p = jnp.where(pos, r, e * r)
    sig_n = jnp.where(pos, e * r, r)
    logf = jnp.log(jnp.maximum(lb, LB_FLOOR) + (1.0 - lb) * sig_p)
    return logf, (1.0 - lb) * sig_n


def _cast_rows(src_refs, dst_refs):
    for src, dst in zip(src_refs, dst_refs):
        dst[...] = src[...].astype(BF16)


def _in_proj_kernel(layer, n_cast, x_ref, w_ref, bg_ref, logits_ref, *refs):
    cast_src, refs = refs[:n_cast], refs[n_cast:]
    (pk_ref, kv_ref, lf_ref, hg_ref, g_ref, ok_ref), refs = refs[:6], refs[6:]
    cast_dst, xb_ref = refs[:n_cast], refs[n_cast]
    _cast_rows(cast_src, cast_dst)
    part = lambda j: pk_ref.at[:, j * PK_COLS:(j + 1) * PK_COLS]
    outs = (part(PK_Q), kv_ref, part(PK_KS), part(PK_VS), part(PK_HQ), part(PK_HK), lf_ref, part(PK_HV), hg_ref,
            g_ref, ok_ref)
    for _ in _in_proj_stages(layer, x_ref, w_ref, bg_ref, logits_ref, outs, xb_ref):
        pass


def _in_proj_stages(layer, x_ref, w_ref, bg_ref, logits_ref, outs, xb_ref):
    q_ref, kv_ref, ks_ref, vs_ref, hq_ref, hk_ref, lf_ref, hv_ref, hg_ref, g_ref, ok_ref = outs
    xb = x_ref[...].astype(BF16)
    lb = _lower_bound(logits_ref, layer)
    chunks = x_ref.shape[0] // CHUNK
    q_ok = []
    decay_ok = []

    def put_q(z):
        q_ref[...] = (z * QK_SCALE).astype(BF16)

    def put_kv(z):
        kv_ref[...] = z
        ks_ref[...] = _split_heads(z[:, :KV_W])
        vs_ref[...] = _split_heads(z[:, KV_W:])

    def put_hq(z):
        hq = _swish(z)
        hq_ref[...] = hq.astype(BF16)
        for c in range(chunks):
            q_ok.append(jnp.max(jnp.abs(hq[c * CHUNK:(c + 1) * CHUNK]), keepdims=True) <= FAST_Q_LIMIT)

    def put_hf(z):
        logf, key = _forget_and_key(z, lb)
        lf_ref[...] = logf
        hk_ref[...] = key.astype(BF16)
        for lo in range(0, chunks * CHUNK, HALF):
            total = jnp.sum(logf[lo:lo + HALF], axis=0, keepdims=True)
            decay_ok.append(jnp.min(total, keepdims=True) >= -FAST_DECAY_LIMIT)

    def put_hv(z):
        hv_ref[...] = z.astype(BF16)

    def put_hg(z):
        hg_ref[...] = _swish(z)

    def put_gates(lo):
        def put(z):
            g_ref[:, lo:lo + 512] = z + bg_ref[:, lo:lo + 512]
        return put

    gates = [(GT0 + lo, GT0 + lo + 512, put_gates(lo)) for lo in range(0, GATE_W, 512)]
    stages = [(HF0, HI0, put_hf), gates[0], (HQ0, HF0, put_hq), gates[1], (AK0, HQ0, put_kv), gates[2],
              (HG0, GT0, put_hg), (AQ0, AK0, put_q), (HI0, HG0, put_hv), gates[3]]
    pending = None
    xb_ref[...] = xb
    for lo, hi, put in stages:
        z = jnp.dot(xb_ref[...], w_ref[:, lo:hi], preferred_element_type=F32)
        if pending is not None:
            pending[0](pending[1])
        pending = (put, z)
        yield
    pending[0](pending[1])
    for c in range(chunks):
        ok = jnp.logical_and(q_ok[c], jnp.logical_and(decay_ok[2 * c], decay_ok[2 * c + 1]))
        ok_ref[c] = jnp.broadcast_to(jnp.where(ok, 1, 0), (1, 128)).astype(jnp.int32)


def _split_heads(t):
    low = lax.broadcasted_iota(jnp.int32, t.shape, 1) < HEAD_DIM
    h0_lo = jnp.where(low, t, 0.0)
    h1_hi = jnp.where(low, 0.0, t)
    blocks = [h0_lo, pltpu.roll(h0_lo, HEAD_DIM, 1), pltpu.roll(h1_hi, HEAD_DIM, 1), h1_hi]
    return jnp.concatenate(blocks, axis=1).astype(BF16)


def _cast_specs(weights, layer, steps):
    src, dst, shapes = [], [], []
    for w in weights:
        _, r, c = w.shape
        slab = r // steps
        assert r % steps == 0 and (slab % 16 == 0 or steps == 1)
        src.append(pl.BlockSpec((None, slab, c), lambda i, layer=layer: (layer, i, 0)))
        dst.append(pl.BlockSpec((slab, c), lambda i: (i, 0)))
        shapes.append(jax.ShapeDtypeStruct((r, c), BF16))
    return src, dst, shapes


def _in_proj(x, w_b, b_gate, logits, layer, cast=()):
    t = x.shape[0]
    tm = min(IN_PROJ_ROWS, t)
    assert t % tm == 0 and tm % CHUNK == 0
    rows = lambda w: pl.BlockSpec((tm, w), lambda i: (i, 0))
    out = lambda w, dt: jax.ShapeDtypeStruct((t, w), dt)
    cast_src, cast_dst, cast_shapes = _cast_specs(cast, layer, t // tm)
    return pl.pallas_call(
        functools.partial(_in_proj_kernel, layer, len(cast)),
        grid=(t // tm,),
        in_specs=[rows(D_MODEL),
                  _resident((D_MODEL, IN_COLS), lambda i: (0, 0)),
                  _resident((None, 1, GATE_W), lambda i: (layer, 0, 0)),
                  _resident(logits.shape, lambda i: (0, 0))] + cast_src,
        out_specs=[rows(PK_PARTS * PK_COLS), rows(2 * KV_W), rows(HG_W), rows(HG_W), rows(GATE_W),
                   pl.BlockSpec((tm // CHUNK, 1, 128), lambda i: (i, 0, 0))] + cast_dst,
        out_shape=[out(PK_PARTS * PK_COLS, BF16), out(2 * KV_W, F32), out(HG_W, F32), out(HG_W, F32),
                   out(GATE_W, F32), jax.ShapeDtypeStruct((t // CHUNK, 1, 128), jnp.int32)] + cast_shapes,
        scratch_shapes=[pltpu.VMEM((tm, D_MODEL), BF16)],
        compiler_params=_cparams("parallel"),
        name="in_proj",
    )(x, w_b, b_gate, logits, *cast)


def _attn_block(n, sink_ref, bias_ref, first, q_ref, kp_ref, kc_ref, vp_ref, vc_ref, o_ref):
    r = WINDOW
    top = lax.broadcasted_iota(jnp.int32, (2 * r, 1), 0) < r
    combos = [(g, half) for g in range(KV_HEADS) for half in range(2)]
    lanes = lambda g, half: slice((2 * g + half) * KV_W, (2 * g + half + 1) * KV_W)
    sink = [jnp.where(top, sink_ref[0, 4 * g + half], sink_ref[0, 4 * g + 2 + half]) * LOG2E for g, half in combos]
    own = slice(n * r, (n + 1) * r)

    def kv_rows(prev_ref, cur_ref, g, half):
        before = prev_ref[:, lanes(g, half)] if n == 0 else cur_ref[(n - 1) * r:n * r, lanes(g, half)]
        return jnp.concatenate([before, cur_ref[own, lanes(g, half)]], axis=0)

    qg = [jnp.concatenate([q_ref[own, 2 * g * 128:(2 * g + 1) * 128],
                           q_ref[own, (2 * g + 1) * 128:(2 * g + 2) * 128]], axis=0) for g in range(KV_HEADS)]
    b = bias_ref[first] if n == 0 else bias_ref[1]
    bias = jnp.concatenate([b, b], axis=0)
    s = [lax.dot_general(qg[g], kv_rows(kp_ref, kc_ref, g, half), (((1,), (1,)), ((), ())),
                         preferred_element_type=F32) + bias for g, half in combos]
    yield
    m = [jnp.maximum(jnp.max(t, axis=-1, keepdims=True), sk) for t, sk in zip(s, sink)]
    p = [jnp.exp2(t - mx) for t, mx in zip(s, m)]
    yield
    den = [jnp.sum(t, axis=-1, keepdims=True) + jnp.exp2(sk - mx) for t, sk, mx in zip(p, sink, m)]
    o = [jnp.dot(t.astype(BF16), kv_rows(vp_ref, vc_ref, g, half), preferred_element_type=F32) / d
         for t, d, (g, half) in zip(p, den, combos)]
    yield
    cols = []
    for g in range(KV_HEADS):
        acc = o[2 * g] + o[2 * g + 1]
        cols += [acc[:r], acc[r:]]
    o_ref[own, :] = jnp.concatenate(cols, axis=1).astype(o_ref.dtype)


def _band_bias():
    qi = lax.broadcasted_iota(jnp.int32, (WINDOW, 2 * WINDOW), 0)
    kj = lax.broadcasted_iota(jnp.int32, (WINDOW, 2 * WINDOW), 1)
    band = jnp.logical_and(kj >= qi, kj <= qi + WINDOW)
    first = jnp.logical_and(band, kj >= WINDOW)
    return jnp.where(jnp.stack([first, band]), 0.0, NEG_BIG).astype(F32)


def _attn_prompt_kernel(sink_ref, bias_ref, q_ref, kp_ref, kc_ref, vp_ref, vc_ref, o_ref):
    first = jnp.minimum(pl.program_id(1), 1)
    _interleave(_attn_block(n, sink_ref, bias_ref, first, q_ref, kp_ref, kc_ref, vp_ref, vc_ref, o_ref)
                for n in range(ATT_BLOCKS))


def _attn_prompt(pk, sink, batch, seq):
    steps = seq // (ATT_BLOCKS * WINDOW)
    assert seq % (ATT_BLOCKS * WINDOW) == 0
    cur = lambda part: pl.BlockSpec((ATT_BLOCKS * WINDOW, PK_COLS), lambda b, i: (b * steps + i, part))
    prev = lambda part: pl.BlockSpec(
        (WINDOW, PK_COLS), lambda b, i: ((b * steps + i) * ATT_BLOCKS - jnp.minimum(i, 1), part))
    return pl.pallas_call(
        _attn_prompt_kernel,
        grid=(batch, steps),
        in_specs=[pl.BlockSpec(memory_space=pltpu.SMEM),
                  _resident((2, WINDOW, 2 * WINDOW), lambda b, i: (0, 0, 0)),
                  cur(PK_Q), prev(PK_KS), cur(PK_KS), prev(PK_VS), cur(PK_VS)],
        out_specs=pl.BlockSpec((ATT_BLOCKS * WINDOW, ATT_W), lambda b, i: (b * steps + i, 0)),
        out_shape=jax.ShapeDtypeStruct((batch * seq, ATT_W), BF16),
        compiler_params=_cparams("parallel", "parallel"),
        name="attn_prompt",
    )(sink, _band_bias(), pk, pk, pk, pk, pk)


def _attn_sample_kernel(sink_ref, q_ref, kvn_ref, ck_ref, cv_ref, *rest):
    o_ref, nk_ref, nv_ref = rest[-3:]
    _interleave(_attn_sample_group(grp, sink_ref, q_ref, kvn_ref, ck_ref, cv_ref, o_ref, nk_ref, nv_ref)
                for grp in range(SAMPLE_GROUPS))


def _attn_sample_group(grp, sink_ref, q_ref, kvn_ref, ck_ref, cv_ref, o_ref, nk_ref, nv_ref):
    rows = slice(grp * ROWS_PER_STEP, (grp + 1) * ROWS_PER_STEP)
    ck_ref, cv_ref, nk_ref, nv_ref = (r.at[grp * SEQ_PER_STEP:(grp + 1) * SEQ_PER_STEP]
                                      for r in (ck_ref, cv_ref, nk_ref, nv_ref))
    lane = lax.broadcasted_iota(jnp.int32, (ROWS_PER_STEP, KV_W), 1)
    q16 = q_ref[rows, :].astype(F32)
    blocks = []
    for h in range(N_HEADS):
        c, half, g = h // 2, h % 2, h // (N_HEADS // KV_HEADS)
        own = lane >= HEAD_DIM if half else lane < HEAD_DIM
        qh = jnp.where(own, q16[:, c * 128:(c + 1) * 128], 0.0)
        blocks.append(qh if half == g else pltpu.roll(qh, HEAD_DIM, 1))
    wt = jnp.concatenate(blocks, axis=0).astype(BF16)

    row = lax.broadcasted_iota(jnp.int32, (WINDOW, WINDOW), 0)
    col = lax.broadcasted_iota(jnp.int32, (WINDOW, WINDOW), 1)
    tok = jnp.bitwise_and(row, DEC_SEQ - 1)
    row_seq = jnp.bitwise_and(jnp.right_shift(row, 2), SEQ_PER_STEP - 1)
    pad = jnp.zeros((WINDOW - ROWS_PER_STEP, KV_W), F32)
    kn = jnp.concatenate([kvn_ref[rows, :KV_W], pad], axis=0)
    vn = jnp.concatenate([kvn_ref[rows, KV_W:], pad], axis=0)
    kn_t = jnp.transpose(kn)
    vn_t = jnp.transpose(vn)
    yield

    s_c = jnp.zeros((WINDOW, WINDOW), F32)
    for j in range(SEQ_PER_STEP):
        sj = jnp.dot(wt, ck_ref[j].astype(BF16), preferred_element_type=F32)
        s_c = jnp.where(row_seq == j, sj, s_c)
    s_n = jnp.dot(wt, kn_t.astype(BF16), preferred_element_type=F32)
    yield
    s_c = jnp.where(col >= tok, s_c, NEG_BIG)
    new_ok = jnp.logical_and(jnp.right_shift(col, 2) == row_seq, jnp.bitwise_and(col, DEC_SEQ - 1) <= tok)
    s_n = jnp.where(new_ok, s_n, NEG_BIG)
    head = jnp.right_shift(lax.broadcasted_iota(jnp.int32, (WINDOW, 1), 0), 4)
    sink = jnp.zeros((WINDOW, 1), F32)
    for h in range(N_HEADS):
        sink = jnp.where(head == h, sink_ref[0, h] * LOG2E, sink)
    m = jnp.maximum(jnp.maximum(jnp.max(s_c, axis=-1, keepdims=True), jnp.max(s_n, axis=-1, keepdims=True)), sink)
    p_c = jnp.exp2(s_c - m)
    p_n = jnp.exp2(s_n - m)
    den = jnp.sum(p_c, axis=-1, keepdims=True) + jnp.sum(p_n, axis=-1, keepdims=True) + jnp.exp2(sink - m)
    p_cb = p_c.astype(BF16)
    yield
    o = jnp.dot(p_n.astype(BF16), vn.astype(BF16), preferred_element_type=F32)
    for j in range(SEQ_PER_STEP):
        oj = lax.dot_general(p_cb, cv_ref[j].astype(BF16), (((1,), (1,)), ((), ())), preferred_element_type=F32)
        o = o + jnp.where(row_seq == j, oj, 0.0)
    yield
    o = o / den
    outs = []
    for c in range(4):
        lo, hi = o[32 * c:32 * c + 16], o[32 * c + 16:32 * c + 32]
        if c < 2:
            outs.append(jnp.where(lane < HEAD_DIM, lo, pltpu.roll(hi, HEAD_DIM, 1)))
        else:
            outs.append(jnp.where(lane < HEAD_DIM, pltpu.roll(lo, HEAD_DIM, 1), hi))
    o_ref[rows, :] = jnp.concatenate(outs, axis=1).astype(o_ref.dtype)
    yield
    for j in range(SEQ_PER_STEP):
        shift = (WINDOW - DEC_SEQ - DEC_SEQ * j) % WINDOW
        for new_t, cache_ref, dst in ((kn_t, ck_ref, nk_ref), (vn_t, cv_ref, nv_ref)):
            dst[j] = jnp.where(col >= WINDOW - DEC_SEQ, pltpu.roll(new_t, shift, 1),
                               pltpu.roll(cache_ref[j], WINDOW - DEC_SEQ, 1))


def _attn_sample(pk, kvn, sink, cache_kt, cache_vt, layer, prev_k, prev_v):
    depth, db = cache_kt.shape[:2]
    seqs = SAMPLE_GROUPS * SEQ_PER_STEP
    cache_spec = pl.BlockSpec((None, seqs, KV_W, WINDOW), lambda s: (layer, s, 0, 0))
    rows = lambda w: pl.BlockSpec((seqs * DEC_SEQ, w), lambda s: (s, 0))
    n_in = 5

    def call(extra, aliases):
        return pl.pallas_call(
            _attn_sample_kernel,
            grid=(db // seqs,),
            in_specs=[pl.BlockSpec(memory_space=pltpu.SMEM), rows(ATT_W), rows(2 * KV_W), cache_spec, cache_spec]
            + [pl.BlockSpec(memory_space=pl.ANY)] * len(extra),
            out_specs=[rows(ATT_W), cache_spec, cache_spec],
            out_shape=[jax.ShapeDtypeStruct((db * DEC_SEQ, ATT_W), BF16),
                       jax.ShapeDtypeStruct((depth, db, KV_W, WINDOW), F32),
                       jax.ShapeDtypeStruct((depth, db, KV_W, WINDOW), F32)],
            input_output_aliases=aliases,
            compiler_params=_cparams("parallel"),
            name="attn_sample",
        )(sink, pk, kvn, cache_kt, cache_vt, *extra)

    return _stacked_out(call, layer, {1: prev_k, 2: prev_v}, n_in)


def _hgrn_out(o, g_act, normw):
    ms = jnp.mean(o * o, axis=-1, keepdims=True)
    return o * lax.rsqrt(ms + RMS_EPS) * normw * g_act


def _cumsum_rows(x, tri):
    hi = x.astype(BF16)
    lo = (x - hi.astype(F32)).astype(BF16)
    dot = lambda t: jnp.dot(tri, t, preferred_element_type=F32)
    return dot(hi) + dot(lo)


def _heads(t):
    return [t[:, h * HG_D:(h + 1) * HG_D] for h in range(HG_HEADS)]


def _level_reference(b, n, row):
    if n == 2:
        return jnp.where(jnp.bitwise_and(row, 1) == 0, b, pltpu.roll(b, 1, 0))
    if n == 4:
        m4 = jnp.bitwise_and(row, 3)
        return jnp.where(m4 == 0, pltpu.roll(b, CHUNK - 1, 0),
                         jnp.where(m4 == 1, b, jnp.where(m4 == 2, pltpu.roll(b, 1, 0), pltpu.roll(b, 2, 0))))
    bounds = [m * n + n // 2 - 1 for m in range(CHUNK // n)]
    pieces = [jnp.broadcast_to(b[t:t + 1], (n, b.shape[1])) for t in bounds]
    return pieces[0] if len(pieces) == 1 else jnp.concatenate(pieces, axis=0)


def _scores(q, k, e_q, e_k):
    qs, ks = _heads(q * e_q.astype(BF16)), _heads(k * e_k.astype(BF16))
    return [lax.dot_general(a, c, (((1,), (1,)), ((), ())), preferred_element_type=F32) for a, c in zip(qs, ks)]


def _intra_chunk_scores(q, k, b, lvl, fast):
    if fast:
        b_mid, b_end = b[HALF - 1:HALF], b[CHUNK - 1:CHUNK]
        r = jnp.concatenate([jnp.broadcast_to(0.5 * b_mid, (HALF, HG_W)),
                             jnp.broadcast_to(0.5 * (b_mid + b_end), (HALF, HG_W))], axis=0)
        e = jnp.exp(b - r)
        e6 = jnp.exp(-jnp.abs(b - b_mid))
        return [jnp.where(lvl == 6, across, jnp.where(lvl >= -1, within, 0.0))
                for within, across in zip(_scores(q, k, e, 1.0 / e), _scores(q, k, e6, e6))]
    row = lax.broadcasted_iota(jnp.int32, (CHUNK, HG_W), 0)
    atts = [jnp.where(lvl == -1, jnp.sum(qk, axis=-1, keepdims=True), 0.0) for qk in _heads((q * k).astype(F32))]
    for level in range(7):
        e = jnp.exp(-jnp.abs(b - _level_reference(b, 2 << level, row)))
        atts = [jnp.where(lvl == level, p, att) for p, att in zip(_scores(q, k, e, e), atts)]
    return atts


def _hgrn_chunks(fast, hq_ref, hk_ref, lf_ref, hv_ref, hg_ref, normw_ref, o_ref, st_ref):
    row = lax.broadcasted_iota(jnp.int32, (CHUNK, CHUNK), 0)
    col = lax.broadcasted_iota(jnp.int32, (CHUNK, CHUNK), 1)
    tri = jnp.where(row >= col, 1.0, 0.0).astype(BF16)
    lvl = jnp.where(row >= col, 31 - lax.clz(jnp.bitwise_xor(row, col)), -2)
    subs = [slice(c * CHUNK, (c + 1) * CHUNK) for c in range(hq_ref.shape[0] // CHUNK)]
    q, k, v = ([ref[rs] for rs in subs] for ref in (hq_ref, hk_ref, hv_ref))
    b = [_cumsum_rows(lf_ref[rs], tri) for rs in subs]
    b_end = [t[CHUNK - 1:CHUNK] for t in b]
    atts = [_intra_chunk_scores(qc, kc, bc, lvl, fast) for qc, kc, bc in zip(q, k, b)]
    q_in = [_heads(qc * jnp.exp(bc).astype(BF16)) for qc, bc in zip(q, b)]
    k_out = [_heads(kc.astype(F32) * jnp.exp(be - bc)) for kc, bc, be in zip(k, b, b_end)]
    k_t = [[jnp.transpose(t).astype(BF16) for t in per_head] for per_head in k_out]
    decay = [[jnp.transpose(jnp.broadcast_to(t, (HG_D, HG_D))) for t in _heads(jnp.exp(be))] for be in b_end]
    vs = [_heads(vc) for vc in v]
    local = [[jnp.dot(a.astype(BF16), vh, preferred_element_type=F32) for a, vh in zip(ac, vc)]
             for ac, vc in zip(atts, vs)]
    grow = [[jnp.dot(kt, vh, preferred_element_type=F32) for kt, vh in zip(kc, vc)] for kc, vc in zip(k_t, vs)]
    state = [st_ref[0, h] for h in range(HG_HEADS)]
    for c, rs in enumerate(subs):
        gs = _heads(hg_ref[rs])
        outs = [_hgrn_out(jnp.dot(q_in[c][h], state[h].astype(BF16), preferred_element_type=F32) + local[c][h],
                          gs[h], normw_ref[...]) for h in range(HG_HEADS)]
        o_ref[rs] = jnp.concatenate(outs, axis=1).astype(o_ref.dtype)
        state = [decay[c][h] * state[h] + grow[c][h] for h in range(HG_HEADS)]
    for h in range(HG_HEADS):
        st_ref[0, h] = state[h]


def _hgrn_prompt_kernel(ok_ref, *refs):
    st_ref = refs[-1]

    @pl.when(pl.program_id(1) == 0)
    def _():
        st_ref[...] = jnp.zeros_like(st_ref)

    first = (pl.program_id(0) * pl.num_programs(1) + pl.program_id(1)) * HGRN_CHUNKS
    ok = ok_ref[first] != 0
    for c in range(1, HGRN_CHUNKS):
        ok = jnp.logical_and(ok, ok_ref[first + c] != 0)

    @pl.when(ok)
    def _():
        _hgrn_chunks(True, *refs)

    @pl.when(jnp.logical_not(ok))
    def _():
        _hgrn_chunks(False, *refs)


def _hgrn_prompt(chunk_ok, pk, lf, hg, normw, batch, seq):
    nc = seq // (HGRN_CHUNKS * CHUNK)
    assert seq % (HGRN_CHUNKS * CHUNK) == 0
    part = lambda j: pl.BlockSpec((HGRN_CHUNKS * CHUNK, HG_W), lambda b, i, ok: (b * nc + i, j))
    rows = part(0)
    return pl.pallas_call(
        _hgrn_prompt_kernel,
        grid_spec=pltpu.PrefetchScalarGridSpec(
            num_scalar_prefetch=1,
            grid=(batch, nc),
            in_specs=[part(PK_HQ), part(PK_HK), rows, part(PK_HV), rows,
                      pl.BlockSpec((1, HG_D), lambda b, i, ok: (0, 0))],
            out_specs=[rows, pl.BlockSpec((1, HG_HEADS, HG_D, HG_D), lambda b, i, ok: (b, 0, 0, 0))]),
        out_shape=[jax.ShapeDtypeStruct((batch * seq, HG_W), BF16),
                   jax.ShapeDtypeStruct((batch, HG_HEADS, HG_D, HG_D), F32)],
        compiler_params=_cparams("parallel", "arbitrary"),
        name="hgrn_prompt",
    )(chunk_ok, pk, pk, lf, pk, hg, normw)


def _hgrn_sample_kernel(hq_ref, hk_ref, lf_ref, hv_ref, hg_ref, normw_ref, st_ref, *rest):
    o_ref, ns_ref = rest[-2:]
    _interleave(_hgrn_sample_group(grp, hq_ref, hk_ref, lf_ref, hv_ref, hg_ref, normw_ref, st_ref, o_ref, ns_ref)
                for grp in range(SAMPLE_GROUPS))


def _hgrn_sample_group(grp, hq_ref, hk_ref, lf_ref, hv_ref, hg_ref, normw_ref, st_ref, o_ref, ns_ref):
    rows = slice(grp * ROWS_PER_STEP, (grp + 1) * ROWS_PER_STEP)
    st_ref, ns_ref = (r.at[grp * SEQ_PER_STEP:(grp + 1) * SEQ_PER_STEP] for r in (st_ref, ns_ref))
    q, k, v = (r[rows, :].astype(F32) for r in (hq_ref, hk_ref, hv_ref))
    logf = lf_ref[rows, :]
    row = lax.broadcasted_iota(jnp.int32, (ROWS_PER_STEP, HG_W), 0)
    tok = jnp.bitwise_and(row, DEC_SEQ - 1)
    seq_of_row = jnp.right_shift(row[:, :HG_D], 2)
    down = lambda t, d: pltpu.roll(t, d, 0)
    up = lambda t, d: pltpu.roll(t, ROWS_PER_STEP - d, 0)
    b = logf
    for d in range(1, DEC_SEQ):
        b = b + jnp.where(tok >= d, down(logf, d), 0.0)
    b_end = jnp.where(tok == 3, b, jnp.where(tok == 2, up(b, 1), jnp.where(tok == 1, up(b, 2), up(b, 3))))
    qe = (q * jnp.exp(b)).astype(BF16)
    kd = k * jnp.exp(b_end - b)
    e_end = jnp.exp(b_end)
    intra = [jnp.zeros((ROWS_PER_STEP, HG_D), F32) for _ in range(HG_HEADS)]
    for d in range(DEC_SEQ):
        kd_, bd_, vd_ = (k, b, v) if d == 0 else (down(k, d), down(b, d), down(v, d))
        w = jnp.where(tok >= d, q * kd_ * jnp.exp(jnp.minimum(b - bd_, 0.0)), 0.0)
        for h in range(HG_HEADS):
            sl = slice(h * HG_D, (h + 1) * HG_D)
            intra[h] = intra[h] + jnp.sum(w[:, sl], axis=-1, keepdims=True) * vd_[:, sl]
    pad = jnp.zeros((HG_D - 2 * ROWS_PER_STEP, HG_D), F32)
    colh = lax.broadcasted_iota(jnp.int32, (HG_D, HG_D), 1)
    normw = normw_ref[...]
    yield
    for h in range(HG_HEADS):
        sl = slice(h * HG_D, (h + 1) * HG_D)
        vb = jnp.concatenate([v[:, sl], jnp.zeros((HG_D - ROWS_PER_STEP, HG_D), F32)], axis=0).astype(BF16)
        tt = jnp.transpose(jnp.concatenate([kd[:, sl], e_end[:, sl], pad], axis=0))
        inter = jnp.zeros((ROWS_PER_STEP, HG_D), F32)
        for j in range(SEQ_PER_STEP):
            s0 = st_ref[j, h]
            oi = jnp.dot(qe[:, sl], s0.astype(BF16), preferred_element_type=F32)
            inter = jnp.where(seq_of_row == j, oi, inter)
            kd_t = jnp.where(jnp.right_shift(colh, 2) == j, tt, 0.0).astype(BF16)
            e_col = ROWS_PER_STEP + DEC_SEQ * j
            decay = jnp.broadcast_to(tt[:, e_col:e_col + 1], (HG_D, HG_D))
            ns_ref[j, h] = decay * s0 + jnp.dot(kd_t, vb, preferred_element_type=F32)
        o_ref[rows, sl] = _hgrn_out(inter + intra[h], hg_ref[rows, sl], normw).astype(o_ref.dtype)
        yield


def _hgrn_sample(pk, lf, hg, normw, state, layer, prev_state):
    depth, db = state.shape[:2]
    seqs = SAMPLE_GROUPS * SEQ_PER_STEP
    part = lambda j: pl.BlockSpec((seqs * DEC_SEQ, HG_W), lambda s: (s, j))
    rows = part(0)
    st_spec = pl.BlockSpec((None, seqs, HG_HEADS, HG_D, HG_D), lambda s: (layer, s, 0, 0, 0))
    n_in = 7

    def call(extra, aliases):
        return pl.pallas_call(
            _hgrn_sample_kernel,
            grid=(db // seqs,),
            in_specs=[part(PK_HQ), part(PK_HK), rows, part(PK_HV), rows,
                      pl.BlockSpec((1, HG_D), lambda s: (0, 0)), st_spec]
            + [pl.BlockSpec(memory_space=pl.ANY)] * len(extra),
            out_specs=[rows, st_spec],
            out_shape=[jax.ShapeDtypeStruct((db * DEC_SEQ, HG_W), BF16), jax.ShapeDtypeStruct(state.shape, F32)],
            input_output_aliases=aliases,
            compiler_params=_cparams("parallel"),
            name="hgrn_sample",
        )(pk, pk, lf, pk, hg, normw, state, *extra)

    return _stacked_out(call, layer, {1: prev_state}, n_in)


def _layer_norm(y, g, b):
    mu = jnp.mean(y, axis=-1, keepdims=True)
    yc = y - mu
    var = jnp.mean(yc * yc, axis=-1, keepdims=True)
    return yc * lax.rsqrt(var + LN_EPS) * g + b


def _post_kernel(alpha, n_cast, a_ref, h_ref, g_ref, x_ref, wua_ref, wuh_ref, wo_ref, l1g_ref, l1b_ref,
                 wf1_ref, wf2_ref, l2g_ref, l2b_ref, *refs):
    o_ref = refs[n_cast]
    _cast_rows(refs[:n_cast], refs[n_cast + 1:])
    groups = [slice(r0, r0 + POST_GROUP) for r0 in range(0, x_ref.shape[0], POST_GROUP)]
    dot = lambda lhs, w_ref: jnp.dot(lhs, w_ref[...], preferred_element_type=F32)
    ua = [dot(a_ref[rs], wua_ref) for rs in groups]
    uh = [dot(h_ref[rs], wuh_ref) for rs in groups]
    gate = jax.nn.sigmoid
    merged = [(gate(g_ref[rs, :D_MODEL]) * a + gate(g_ref[rs, D_MODEL:]) * h).astype(BF16)
              for rs, a, h in zip(groups, ua, uh)]
    m = [dot(t, wo_ref) for t in merged]
    x1 = [_layer_norm(alpha * x_ref[rs] + t, l1g_ref[...], l1b_ref[...]) for rs, t in zip(groups, m)]
    x1b = [t.astype(BF16) for t in x1]
    half = D_FF // 2
    ff = None
    for lo in (0, half):
        hid = [jnp.square(jnp.maximum(jnp.dot(t, wf1_ref[:, lo:lo + half], preferred_element_type=F32), 0.0)
                          ).astype(BF16) for t in x1b]
        part = [jnp.dot(t, wf2_ref[lo:lo + half, :], preferred_element_type=F32) for t in hid]
        ff = part if ff is None else [a + c for a, c in zip(ff, part)]
    for rs, t, f in zip(groups, x1, ff):
        o_ref[rs] = _layer_norm(alpha * t + f, l2g_ref[...], l2b_ref[...])


def _post(a, h, g, x, wb, p, layer, alpha, cast=(), cast_layer=0):
    t = x.shape[0]
    tm = min(POST_ROWS, t)
    assert t % tm == 0
    rows = lambda w: pl.BlockSpec((tm, w), lambda i: (i, 0))
    vec = lambda w: _resident((None, 1, w), lambda i: (layer, 0, 0))
    mat = lambda w: _resident(w.shape, lambda i: (0, 0))
    cast_src, cast_dst, cast_shapes = _cast_specs(cast, cast_layer, t // tm)
    outs = pl.pallas_call(
        functools.partial(_post_kernel, alpha, len(cast)),
        grid=(t // tm,),
        in_specs=[rows(ATT_W), rows(HG_W), rows(GATE_W), rows(D_MODEL), mat(wb[0]), mat(wb[1]), mat(wb[2]),
                  vec(D_MODEL), vec(D_MODEL), mat(wb[3]), mat(wb[4]), vec(D_MODEL), vec(D_MODEL)] + cast_src,
        out_specs=[rows(D_MODEL)] + cast_dst,
        out_shape=[jax.ShapeDtypeStruct((t, D_MODEL), F32)] + cast_shapes,
        compiler_params=_cparams("parallel"),
        name="post",
    )(a, h, g, x, wb[0], wb[1], wb[2], p["ln1_g"], p["ln1_b"], wb[3], wb[4], p["ln2_g"], p["ln2_b"], *cast)
    return outs[0], outs[1:]


def kernel(x_prompt, x_sample, cache_k, cache_v, state_hgrn, w_in, b_gate, attn_sink, hgrn_lb_logits, hgrn_norm_w,
           w_up_attn, w_up_hgrn, w_out, ln1_g, ln1_b, w_ff1, w_ff2, ln2_g, ln2_b):
    depth = w_in.shape[0]
    batch, seq, _ = x_prompt.shape
    db, dec_seq, _ = x_sample.shape
    assert dec_seq == DEC_SEQ and seq % CHUNK == 0 and db % (SAMPLE_GROUPS * SEQ_PER_STEP) == 0
    assert cache_k.shape[2:] == (WINDOW, KV_HEADS, HEAD_DIM)
    alpha = (2 * depth) ** 0.25

    vec = lambda t: t.reshape(depth, 1, t.shape[-1])
    b_gate3 = vec(b_gate)
    p = dict(ln1_g=vec(ln1_g), ln1_b=vec(ln1_b), ln2_g=vec(ln2_g), ln2_b=vec(ln2_b))
    mixer_w = (w_up_attn, w_up_hgrn, w_out, w_ff1, w_ff2)
    w_in_b = w_in[0].astype(BF16)
    to_t = lambda c: jnp.transpose(c, (0, 1, 3, 4, 2)).reshape(depth, db, KV_W, WINDOW)
    from_t = lambda c: jnp.transpose(c.reshape(depth, db, KV_HEADS, HEAD_DIM, WINDOW), (0, 1, 4, 2, 3))
    ck_t, cv_t = to_t(cache_k), to_t(cache_v)

    xp = x_prompt.reshape(batch * seq, D_MODEL)
    xs = x_sample.reshape(db * DEC_SEQ, D_MODEL)
    pk, pv, ps = [], [], []
    sk = sv = ss = None
    for l in range(depth):
        sink = attn_sink[l].reshape(1, N_HEADS)
        normw = hgrn_norm_w[l].reshape(1, HG_D)
        packed, kv, lf, hg, g, chunk_ok, *wb = _in_proj(xp, w_in_b, b_gate3, hgrn_lb_logits, l, cast=mixer_w)
        a = _attn_prompt(packed, sink, batch, seq)
        h, st = _hgrn_prompt(chunk_ok[:, 0, 0], packed, lf, hg, normw, batch, seq)
        xp, next_w_in = _post(a, h, g, xp, wb, p, l, alpha, cast=(w_in,) if l + 1 < depth else (), cast_layer=l + 1)
        win = kv.reshape(batch, seq, 2 * KV_W)[:, seq - WINDOW:].reshape(batch, WINDOW, 2, KV_HEADS, HEAD_DIM)
        pk.append(win[:, :, 0])
        pv.append(win[:, :, 1])
        ps.append(st)
        packed, kvn, lf, hg, g, _ = _in_proj(xs, w_in_b, b_gate3, hgrn_lb_logits, l)
        a, sk, sv = _attn_sample(packed, kvn, sink, ck_t, cv_t, l, sk, sv)
        h, ss = _hgrn_sample(packed, lf, hg, normw, state_hgrn, l, ss)
        xs, _ = _post(a, h, g, xs, wb, p, l, alpha)
        if next_w_in:
            w_in_b = next_w_in[0]
    return (xp.reshape(batch, seq, D_MODEL), xs.reshape(db, DEC_SEQ, D_MODEL), jnp.stack(pk), jnp.stack(pv),
            jnp.stack(ps), from_t(sk), from_t(sv), ss)
```
